```python
import jax, jax.numpy as jnp
from jax import lax
import numpy as np

D_MODEL = 1024
BATCH = 4
SEQ = 8192
DEPTH = 1
DEC_BATCH = 128
DEC_SEQ = 1
PAST_LEN = 8192
PAGE_SIZE = 128

M_HEADS = 4
M_DQK = 64
M_DV = 128
M_CHUNK = 64
GATE_CAP = 15.0
N_HEADS = 8
N_KV = 2
HEAD_DIM = 64
HPG = N_HEADS // N_KV
CMP_BLOCK = 32
SEL_BLOCK = 64
TOP_K = 16
WINDOW = 512
Q_BLOCK = 128
ROPE_DIM = HEAD_DIM // 4
ROPE_THETA = 500000.0
D_FF = ((8 * D_MODEL // 3 + 255) // 256) * 256
EPS = 1e-6
NEG = -1e30

MQK = M_HEADS * M_DQK
MV = M_HEADS * M_DV
NQ = N_HEADS * HEAD_DIM
NKV = N_KV * HEAD_DIM
SPLITS = (('mq', MQK), ('mk', MQK), ('mv', MV), ('mo', MV), ('mi', M_HEADS), ('mf', M_HEADS),
          ('nq', NQ), ('nkv', 6 * NKV), ('ng', 3 * N_HEADS), ('ga', D_MODEL), ('gb', D_MODEL))
D_IN = sum(n for _, n in SPLITS)

kernel_name = 'hybrid_mlstm_nsa_decode_step'


def rms_norm(x, g):
    xf = x.astype(jnp.float32)
    y = xf * lax.rsqrt(jnp.mean(xf * xf, axis=-1, keepdims=True) + EPS)
    return (y * g.astype(jnp.float32)).astype(x.dtype)


def rope(x, pos):
    half = ROPE_DIM // 2
    inv = ROPE_THETA ** (-jnp.arange(half, dtype=jnp.float32) * 2.0 / ROPE_DIM)
    ang = pos.astype(jnp.float32)[:, None] * inv[None, :]
    cos = jnp.cos(ang)[:, None, :]
    sin = jnp.sin(ang)[:, None, :]
    xr = x[..., :ROPE_DIM].astype(jnp.float32)
    x1, x2 = xr[..., :half], xr[..., half:]
    rot = jnp.concatenate([x1 * cos - x2 * sin, x2 * cos + x1 * sin], axis=-1)
    return jnp.concatenate([rot.astype(x.dtype), x[..., ROPE_DIM:]], axis=-1)


def masked_softmax(s, mask):
    s = jnp.where(mask, s.astype(jnp.float32), NEG)
    p = jax.nn.softmax(s, axis=-1)
    return jnp.where(mask, p, 0.0)


def project(h, w_in, b_in):
    proj = h @ w_in + b_in
    offs = [int(o) for o in np.cumsum([n for _, n in SPLITS])[:-1]]
    return dict(zip([nm for nm, _ in SPLITS], jnp.split(proj, offs, axis=-1)))


def mlstm_chunkwise(q, k, v, i_pre, log_f, C0, n0, m0):
    B, L = q.shape[0], q.shape[1]
    T = min(M_CHUNK, L)
    NC = -(-L // T)
    pad = NC * T - L

    def prep(a, fill):
        a = a.astype(jnp.float32)
        if pad:
            a = jnp.pad(a, [(0, 0), (0, pad)] + [(0, 0)] * (a.ndim - 2), constant_values=fill)
        a = jnp.moveaxis(a.reshape((B, NC, T) + a.shape[2:]), 1, 0)
        return jnp.swapaxes(a, 2, 3)

    qc, kc, vc = prep(q, 0.0), prep(k, 0.0), prep(v, 0.0)
    ic, fc = prep(i_pre, NEG), prep(log_f, 0.0)
    causal = jnp.tril(jnp.ones((T, T), dtype=bool))

    def step(carry, xs):
        C, n, m = carry
        qb, kb, vb, ib, fb = xs
        b = jnp.cumsum(fb, axis=-1)
        d = jnp.where(causal, b[..., :, None] - b[..., None, :] + ib[..., None, :], NEG)
        m_inter = b + m[..., None]
        m_t = jnp.maximum(m_inter, jnp.max(d, axis=-1))
        w = jnp.einsum('bhtd,bhsd->bhts', qb, kb) * jnp.exp(d - m_t[..., None])
        a_inter = jnp.exp(m_inter - m_t)
        num = jnp.einsum('bhts,bhsv->bhtv', w, vb) + a_inter[..., None] * jnp.einsum('bhtd,bhdv->bhtv', qb, C)
        den = jnp.sum(w, axis=-1) + a_inter * jnp.einsum('bhtd,bhd->bht', qb, n)
        h = num / jnp.maximum(jnp.abs(den), jnp.exp(-m_t))[..., None]
        b_T = b[..., -1]
        g = b_T[..., None] - b + ib
        m_new = jnp.maximum(b_T + m, jnp.max(g, axis=-1))
        a_in = jnp.exp(g - m_new[..., None])
        a_st = jnp.exp(b_T + m - m_new)
        C_new = a_st[..., None, None] * C + jnp.einsum('bhs,bhsd,bhsv->bhdv', a_in, kb, vb)
        n_new = a_st[..., None] * n + jnp.einsum('bhs,bhsd->bhd', a_in, kb)
        return (C_new, n_new, m_new), h

    init = (C0.astype(jnp.float32), n0.astype(jnp.float32), m0.astype(jnp.float32))
    (C, n, m), h = lax.scan(step, init, (qc, kc, vc, ic, fc))
    h = jnp.transpose(h, (1, 0, 3, 2, 4)).reshape(B, NC * T, M_HEADS, M_DV)[:, :L]
    return h, C, n, m


def mlstm_branch(parts, C0, n0, m0, norm_w):
    B, L = parts['mq'].shape[:2]
    q = parts['mq'].reshape(B, L, M_HEADS, M_DQK) * (M_DQK ** -0.5)
    k = parts['mk'].reshape(B, L, M_HEADS, M_DQK)
    v = parts['mv'].reshape(B, L, M_HEADS, M_DV)
    i_pre = GATE_CAP * jnp.tanh(parts['mi'].astype(jnp.float32) / GATE_CAP)
    log_f = jax.nn.log_sigmoid(GATE_CAP * jnp.tanh(parts['mf'].astype(jnp.float32) / GATE_CAP))
    h, C, n, m = mlstm_chunkwise(q, k, v, i_pre, log_f, C0, n0, m0)
    h = rms_norm(h, norm_w.reshape(M_HEADS, M_DV)).reshape(B, L, MV)
    y = (jax.nn.sigmoid(parts['mo'].astype(jnp.float32)) * h).astype(parts['mo'].dtype)
    return y, C, n, m


def nsa_inputs(parts, pos):
    B, T = parts['nq'].shape[:2]
    q = rope(parts['nq'].reshape(B, T, N_HEADS, HEAD_DIM), pos)
    kv = parts['nkv'].reshape(B, T, 6, N_KV, HEAD_DIM)
    k_c, v_c = rope(kv[:, :, 0], pos), kv[:, :, 1]
    k_s, v_s = rope(kv[:, :, 2], pos), kv[:, :, 3]
    k_w, v_w = rope(kv[:, :, 4], pos), kv[:, :, 5]
    gates = jax.nn.sigmoid(parts['ng'].astype(jnp.float32)).reshape(B, T, N_HEADS, 3)
    return q, gates, k_c, v_c, k_s, v_s, k_w, v_w


def compress(rows, w):
    B, L = rows.shape[:2]
    ncmp = L // CMP_BLOCK
    blocks = rows[:, :ncmp * CMP_BLOCK].reshape(B, ncmp, CMP_BLOCK, N_KV, HEAD_DIM)
    return jnp.einsum('bjigd,id->bjgd', blocks, w)


def to_blocks(rows):
    B, L = rows.shape[:2]
    nsel = -(-L // SEL_BLOCK)
    rows = jnp.pad(rows, ((0, 0), (0, nsel * SEL_BLOCK - L), (0, 0), (0, 0)))
    return jnp.transpose(rows.reshape(B, nsel, SEL_BLOCK, N_KV, HEAD_DIM), (0, 3, 1, 2, 4))


def nsa_attend(q, q_pos, kc, vc, k_blk, v_blk, k_win, v_win, win_pos, gates):
    B, T = q.shape[:2]
    qg = q.astype(jnp.float32).reshape(B, T, N_KV, HPG, HEAD_DIM) * (HEAD_DIM ** -0.5)
    ncmp = kc.shape[1]
    valid_c = ((jnp.arange(ncmp) + 1) * CMP_BLOCK - 1)[None, :] <= q_pos[:, None]
    p_c = masked_softmax(jnp.einsum('btghd,bjgd->btghj', qg, kc), valid_c[None, :, None, None, :])
    o_c = jnp.einsum('btghj,bjgd->btghd', p_c, vc)
    nsel = k_blk.shape[2]
    ratio = SEL_BLOCK // CMP_BLOCK
    imp = jnp.sum(p_c, axis=3)
    imp = jnp.pad(imp, ((0, 0), (0, 0), (0, 0), (0, nsel * ratio - ncmp)))
    imp = imp.reshape(B, T, N_KV, nsel, ratio).sum(-1)
    blk = jnp.arange(nsel)
    forced = (blk[None, :] == (q_pos // SEL_BLOCK)[:, None]) | (blk[None, :] == 0)
    future = blk[None, :] * SEL_BLOCK > q_pos[:, None]
    imp = jnp.where(forced[None, :, None, :], jnp.inf, jnp.where(future[None, :, None, :], -jnp.inf, imp))
    k_eff = min(TOP_K, nsel)
    _, idx = lax.top_k(imp, k_eff)
    bi = jnp.arange(B)[:, None, None, None]
    gi = jnp.arange(N_KV)[None, None, :, None]
    ks = k_blk[bi, gi, idx].reshape(B, T, N_KV, k_eff * SEL_BLOCK, HEAD_DIM)
    vs = v_blk[bi, gi, idx].reshape(B, T, N_KV, k_eff * SEL_BLOCK, HEAD_DIM)
    kpos = (idx[..., None] * SEL_BLOCK + jnp.arange(SEL_BLOCK)).reshape(B, T, N_KV, 1, k_eff * SEL_BLOCK)
    p_s = masked_softmax(jnp.einsum('btghd,btgnd->btghn', qg, ks), kpos <= q_pos[None, :, None, None, None])
    o_s = jnp.einsum('btghn,btgnd->btghd', p_s, vs)
    dpos = q_pos[:, None] - win_pos[None, :]
    valid_w = (dpos >= 0) & (dpos < WINDOW) & (win_pos >= 0)[None, :]
    p_w = masked_softmax(jnp.einsum('btghd,blgd->btghl', qg, k_win), valid_w[None, :, None, None, :])
    o_w = jnp.einsum('btghl,blgd->btghd', p_w, v_win)
    g = gates.reshape(B, T, N_KV, HPG, 3)
    out = g[..., 0:1] * o_c + g[..., 1:2] * o_s + g[..., 2:3] * o_w
    return out.reshape(B, T, NQ).astype(q.dtype)


def nsa_prompt(parts, pos, w_cmp_k, w_cmp_v):
    q, gates, k_c, v_c, k_s, v_s, k_w, v_w = nsa_inputs(parts, pos)
    B, S = q.shape[:2]
    kc, vc = compress(k_c, w_cmp_k), compress(v_c, w_cmp_v)
    k_blk, v_blk = to_blocks(k_s), to_blocks(v_s)
    k_wp = jnp.pad(k_w, ((0, 0), (WINDOW, 0), (0, 0), (0, 0)))
    v_wp = jnp.pad(v_w, ((0, 0), (WINDOW, 0), (0, 0), (0, 0)))
    qb = min(Q_BLOCK, S)
    nqb = S // qb

    def one_block(i):
        start = i * qb
        qpos = start + jnp.arange(qb, dtype=jnp.int32)
        wpos = start - WINDOW + jnp.arange(qb + WINDOW, dtype=jnp.int32)
        return nsa_attend(lax.dynamic_slice_in_dim(q, start, qb, axis=1), qpos, kc, vc, k_blk, v_blk,
                          lax.dynamic_slice_in_dim(k_wp, start, qb + WINDOW, axis=1),
                          lax.dynamic_slice_in_dim(v_wp, start, qb + WINDOW, axis=1), wpos,
                          lax.dynamic_slice_in_dim(gates, start, qb, axis=1))

    out = lax.map(one_block, jnp.arange(nqb, dtype=jnp.int32))
    out = jnp.moveaxis(out, 0, 1).reshape(B, S, NQ)
    win = jnp.stack([k_w, v_w], axis=2)[:, -min(WINDOW, S):]
    return out, jnp.stack([k_c, v_c], axis=2), jnp.stack([k_s, v_s], axis=2), win


def nsa_sample(parts, pos, cache_cmp, cache_slc, cache_win, page_table, w_cmp_k, w_cmp_v):
    q, gates, k_c, v_c, k_s, v_s, k_w, v_w = nsa_inputs(parts, pos)
    B, T = q.shape[:2]
    past_len = page_table.shape[1] * PAGE_SIZE
    cmp_past = cache_cmp[page_table].reshape(B, past_len, 2, N_KV, HEAD_DIM)
    slc_past = cache_slc[page_table].reshape(B, past_len, 2, N_KV, HEAD_DIM)
    kc = compress(jnp.concatenate([cmp_past[:, :, 0], k_c], axis=1), w_cmp_k)
    vc = compress(jnp.concatenate([cmp_past[:, :, 1], v_c], axis=1), w_cmp_v)
    k_blk = to_blocks(jnp.concatenate([slc_past[:, :, 0], k_s], axis=1))
    v_blk = to_blocks(jnp.concatenate([slc_past[:, :, 1], v_s], axis=1))
    wb = cache_win.shape[1]
    k_win = jnp.concatenate([cache_win[:, :, 0], k_w], axis=1)
    v_win = jnp.concatenate([cache_win[:, :, 1], v_w], axis=1)
    wpos = past_len - wb + jnp.arange(wb + T, dtype=jnp.int32)
    out = nsa_attend(q, pos, kc, vc, k_blk, v_blk, k_win, v_win, wpos, gates)
    new_win = jnp.concatenate([cache_win, jnp.stack([k_w, v_w], axis=2)], axis=1)[:, -min(WINDOW, past_len + T):]
    return out, jnp.stack([k_c, v_c], axis=2), jnp.stack([k_s, v_s], axis=2), new_win


def layer(x, c, mix_fn, g_pre_mix, g_post_mix, g_pre_ffn, g_post_ffn, w_ada, b_ada, w_in, b_in,
          w_proj_a, w_proj_b, w_out, w_up, w_down):
    mod = jax.nn.silu(c) @ w_ada + b_ada
    sh1, sc1, ga1, sh2, sc2, ga2 = jnp.split(mod[:, None, :], 6, axis=-1)
    h = rms_norm(x, g_pre_mix) * (1.0 + sc1) + sh1
    parts = project(h, w_in, b_in)
    y_a, y_b, state = mix_fn(parts)
    merged = jax.nn.sigmoid(parts['ga']) * (y_a @ w_proj_a) + jax.nn.sigmoid(parts['gb']) * (y_b @ w_proj_b)
    x = x + ga1 * rms_norm(merged @ w_out, g_post_mix)
    h = rms_norm(x, g_pre_ffn) * (1.0 + sc2) + sh2
    gate, up = jnp.split(h @ w_up, 2, axis=-1)
    x = x + ga2 * rms_norm((jax.nn.silu(gate) * up) @ w_down, g_post_ffn)
    return x, state


def setup_inputs(seed: int = 0) -> dict:
    key = jax.random.key(seed)
    ks = jax.random.split(key, 32)
    n_pages = PAST_LEN // PAGE_SIZE
    n_used = DEC_BATCH * n_pages
    n_pool = (n_used * 5 + 3) // 4
    wb = min(WINDOW, PAST_LEN)

    def nrm(k, shape, s):
        return jax.random.normal(k, shape, jnp.float32) * s

    f_lo = 0
    off = 0
    for name, n in SPLITS:
        if name == 'mf':
            f_lo = off
        off += n
    b_in = nrm(ks[17], (DEPTH, D_IN), 0.02).at[:, f_lo:f_lo + M_HEADS].add(3.0)
    page_table = jax.random.permutation(ks[10], n_pool)[:n_used].reshape(DEC_BATCH, n_pages).astype(jnp.int32)
    return {
        'x_prompt': nrm(ks[0], (BATCH, SEQ, D_MODEL), 1.0),
        'x_sample': nrm(ks[1], (DEC_BATCH, DEC_SEQ, D_MODEL), 1.0),
        'c_prompt': nrm(ks[2], (BATCH, D_MODEL), 1.0),
        'c_sample': nrm(ks[3], (DEC_BATCH, D_MODEL), 1.0),
        'cache_cmp_kv': nrm(ks[4], (DEPTH, n_pool, PAGE_SIZE, 2, N_KV, HEAD_DIM), 1.0),
        'cache_slc_kv': nrm(ks[5], (DEPTH, n_pool, PAGE_SIZE, 2, N_KV, HEAD_DIM), 1.0),
        'cache_win_kv': nrm(ks[6], (DEPTH, DEC_BATCH, wb, 2, N_KV, HEAD_DIM), 1.0),
        'state_mlstm_C': nrm(ks[7], (DEPTH, DEC_BATCH, M_HEADS, M_DQK, M_DV), 1.0),
        'state_mlstm_n': nrm(ks[8], (DEPTH, DEC_BATCH, M_HEADS, M_DQK), 1.0),
        'state_mlstm_m': nrm(ks[9], (DEPTH, DEC_BATCH, M_HEADS), 1.0),
        'page_table': page_table,
        'g_pre_mix': 1.0 + nrm(ks[11], (DEPTH, D_MODEL), 0.05),
        'g_post_mix': 1.0 + nrm(ks[12], (DEPTH, D_MODEL), 0.05),
        'g_pre_ffn': 1.0 + nrm(ks[13], (DEPTH, D_MODEL), 0.05),
        'g_post_ffn': 1.0 + nrm(ks[14], (DEPTH, D_MODEL), 0.05),
        'w_ada': nrm(ks[15], (DEPTH, D_MODEL, 6 * D_MODEL), 0.5 * D_MODEL ** -0.5),
        'b_ada': nrm(ks[16], (DEPTH, 6 * D_MODEL), 0.02),
        'w_in': nrm(ks[18], (DEPTH, D_MODEL, D_IN), D_MODEL ** -0.5),
        'b_in': b_in,
        'mlstm_norm_w': 1.0 + nrm(ks[19], (DEPTH, MV), 0.05),
        'w_cmp_k': (1.0 + nrm(ks[20], (DEPTH, CMP_BLOCK, HEAD_DIM), 0.1)) / CMP_BLOCK,
        'w_cmp_v': (1.0 + nrm(ks[21], (DEPTH, CMP_BLOCK, HEAD_DIM), 0.1)) / CMP_BLOCK,
        'w_proj_a': nrm(ks[22], (DEPTH, MV, D_MODEL), MV ** -0.5),
        'w_proj_b': nrm(ks[23], (DEPTH, NQ, D_MODEL), NQ ** -0.5),
        'w_out': nrm(ks[24], (DEPTH, D_MODEL, D_MODEL), D_MODEL ** -0.5),
        'w_up': nrm(ks[25], (DEPTH, D_MODEL, 2 * D_FF), D_MODEL ** -0.5),
        'w_down': nrm(ks[26], (DEPTH, D_FF, D_MODEL), D_FF ** -0.5),
    }


def reference(x_prompt, x_sample, c_prompt, c_sample, cache_cmp_kv, cache_slc_kv, cache_win_kv,
              state_mlstm_C, state_mlstm_n, state_mlstm_m, page_table,
              g_pre_mix, g_post_mix, g_pre_ffn, g_post_ffn, w_ada, b_ada, w_in, b_in, mlstm_norm_w,
              w_cmp_k, w_cmp_v, w_proj_a, w_proj_b, w_out, w_up, w_down):
    Bp, S = x_prompt.shape[:2]
    Bs, Ts = x_sample.shape[:2]
    past_len = page_table.shape[1] * PAGE_SIZE
    pos_p = jnp.arange(S, dtype=jnp.int32)
    pos_s = past_len + jnp.arange(Ts, dtype=jnp.int32)
    xp, xs = x_prompt, x_sample
    st_p = [[] for _ in range(6)]
    st_s = [[] for _ in range(6)]
    for l in range(DEPTH):
        def mix_prompt(parts, l=l):
            zC = jnp.zeros((Bp, M_HEADS, M_DQK, M_DV), jnp.float32)
            zn = jnp.zeros((Bp, M_HEADS, M_DQK), jnp.float32)
            zm = jnp.zeros((Bp, M_HEADS), jnp.float32)
            y_a, C, n, m = mlstm_branch(parts, zC, zn, zm, mlstm_norm_w[l])
            y_b, cmp_rows, slc_rows, win = nsa_prompt(parts, pos_p, w_cmp_k[l], w_cmp_v[l])
            return y_a, y_b, (cmp_rows, slc_rows, win, C, n, m)

        def mix_sample(parts, l=l):
            y_a, C, n, m = mlstm_branch(parts, state_mlstm_C[l], state_mlstm_n[l], state_mlstm_m[l], mlstm_norm_w[l])
            y_b, cmp_rows, slc_rows, win = nsa_sample(parts, pos_s, cache_cmp_kv[l], cache_slc_kv[l], cache_win_kv[l],
                                                      page_table, w_cmp_k[l], w_cmp_v[l])
            return y_a, y_b, (cmp_rows, slc_rows, win, C, n, m)

        weights = (g_pre_mix[l], g_post_mix[l], g_pre_ffn[l], g_post_ffn[l], w_ada[l], b_ada[l], w_in[l], b_in[l],
                   w_proj_a[l], w_proj_b[l], w_out[l], w_up[l], w_down[l])
        xp, sp = layer(xp, c_prompt, mix_prompt, *weights)
        xs, ss = layer(xs, c_sample, mix_sample, *weights)
        for j in range(6):
            st_p[j].append(sp[j])
            st_s[j].append(ss[j])
    cmp_p, slc_p, win_p, C_p, n_p, m_p = [jnp.stack(a, axis=0) for a in st_p]
    cmp_s, slc_s, win_s, C_s, n_s, m_s = [jnp.stack(a, axis=0) for a in st_s]
    return (xp, xs, cmp_p, slc_p, win_p, C_p, n_p, m_p, cmp_s, slc_s, win_s, C_s, n_s, m_s)
```

```python
import functools

import jax
import jax.numpy as jnp
from jax import lax
from jax.experimental import pallas as pl
from jax.experimental.pallas import tpu as pltpu

F32 = jnp.float32
BF16 = jnp.bfloat16

M_HEADS, M_DQK, M_DV = 4, 64, 128
GATE_CAP = 15.0
N_HEADS, N_KV, HEAD_DIM = 8, 2, 64
HPG = N_HEADS // N_KV
CMP_BLOCK, SEL_BLOCK, TOP_K, WINDOW = 32, 64, 16, 512
ROPE_DIM = HEAD_DIM // 4
ROPE_THETA = 500000.0
PAGE_SIZE = 128
EPS = 1e-6
NEG = -1e30
MQK, MV = M_HEADS * M_DQK, M_HEADS * M_DV
NQ, NKV = N_HEADS * HEAD_DIM, N_KV * HEAD_DIM
ROW_W = 2 * NKV

LANES = 128
VMEM_LIMIT = 48 * 1024 * 1024

TM_PROJ = 512
T_CHUNK = 128
TQ = 128
TKS = 256
TKW = 128
PAGES_PER_STEP = 8
FFN_CK = 256

IMP_FORCED, IMP_FUTURE, IMP_TAKEN = 1e30, -1e30, -2e30


def _dot(a, b):
    return jnp.dot(a, b, preferred_element_type=F32)


def _dot_nt(a, b):
    return lax.dot_general(a, b, (((1,), (1,)), ((), ())), preferred_element_type=F32)


def _rms(x, g):
    return x * lax.rsqrt(jnp.mean(x * x, axis=-1, keepdims=True) + EPS) * g


def _log_sigmoid(x):
    return jnp.minimum(x, 0.0) - jnp.log1p(jnp.exp(-jnp.abs(x)))


def _cparams(*sem):
    return pltpu.CompilerParams(dimension_semantics=sem, vmem_limit_bytes=VMEM_LIMIT)


def _adaln_kernel(c_ref, w_ref, b_ref, o_ref):
    c = c_ref[...]
    a = (c * jax.nn.sigmoid(c)).astype(BF16)
    o_ref[...] = _dot(a, w_ref[...].astype(BF16)) + b_ref[...]


def _adaln(c, w, b):
    r, d = c.shape
    n = w.shape[1]
    tn = 1536
    return pl.pallas_call(
        _adaln_kernel,
        grid=(n // tn,),
        in_specs=[pl.BlockSpec((r, d), lambda j: (0, 0)),
                  pl.BlockSpec((d, tn), lambda j: (0, j)),
                  pl.BlockSpec((1, tn), lambda j: (0, j))],
        out_specs=pl.BlockSpec((r, tn), lambda j: (0, j)),
        out_shape=jax.ShapeDtypeStruct((r, n), F32),
        compiler_params=_cparams("arbitrary"),
        name="adaln",
    )(c, w, b.reshape(1, n))


_C_MQ, _C_MK, _C_MV, _C_MO, _C_NQ, _C_NKV, _C_GAB, _C_END = 0, 256, 512, 1024, 1536, 2048, 2816, 4864


def _proj_kernel(x_ref, g_ref, sc_ref, sh_ref, wm_ref, bm_ref, wtg_ref, btg_ref, wtb_ref, btb_ref,
                 ra_ref, rp_ref, rm_ref, wc_ref,
                 mq_ref, mk_ref, mv_ref, mo_ref, nq_ref, cmp_ref, slc_ref, win_ref, ks_ref, kw_ref,
                 gab_ref, kcvc_ref, gt_ref, ngt_ref, mkt_ref, vst_ref, vwt_ref, *, tks, tkw):
    tm = x_ref.shape[0]
    x = x_ref[...]
    h = _rms(x, g_ref[...]) * (1.0 + sc_ref[0]) + sh_ref[0]
    hb = h.astype(BF16)

    def mm(lo, hi):
        return _dot(hb, wm_ref[:, lo:hi]) + bm_ref[:, lo:hi]

    ra, rp, rm = ra_ref[...], rp_ref[...], rm_ref[...]

    def rope(xc):
        return xc * ra + pltpu.roll(xc, 8, 1) * rp + pltpu.roll(xc, LANES - 8, 1) * rm

    scale_m = M_DQK ** -0.5
    scale_n = HEAD_DIM ** -0.5
    mq_ref[...] = (mm(_C_MQ, _C_MK) * scale_m).astype(BF16)
    mk_ref[...] = mm(_C_MK, _C_MV).astype(BF16)
    mv_ref[...] = mm(_C_MV, _C_MO).astype(BF16)
    mo_ref[...] = mm(_C_MO, _C_NQ)

    nq = mm(_C_NQ, _C_NKV)
    for c in range(NQ // LANES):
        r = (rope(nq[:, c * LANES:(c + 1) * LANES]) * scale_n).astype(BF16)
        nq_ref[2 * c] = r[:, :HEAD_DIM]
        nq_ref[2 * c + 1] = r[:, HEAD_DIM:]

    nkv = mm(_C_NKV, _C_GAB)
    kc = rope(nkv[:, 0:128])
    vc = nkv[:, 128:256]
    ksr = rope(nkv[:, 256:384])
    kwr = rope(nkv[:, 512:640])
    cmp_ref[:, 0:NKV] = kc
    cmp_ref[:, NKV:ROW_W] = vc
    slc_ref[:, 0:NKV] = ksr
    slc_ref[:, NKV:ROW_W] = nkv[:, 384:512]
    win_ref[:, 0:NKV] = kwr
    win_ref[:, NKV:ROW_W] = nkv[:, 640:768]
    for g in range(N_KV):
        ks_ref[g] = ksr[:, g * HEAD_DIM:(g + 1) * HEAD_DIM].astype(BF16)
        kw_ref[g] = kwr[:, g * HEAD_DIM:(g + 1) * HEAD_DIM].astype(BF16)

    wc = wc_ref[...]
    nb = tm // CMP_BLOCK
    kcvc_ref[:, 0:NKV] = jnp.sum(kc.reshape(nb, CMP_BLOCK, NKV) * wc[None, :, 0:NKV], axis=1)
    kcvc_ref[:, NKV:ROW_W] = jnp.sum(vc.reshape(nb, CMP_BLOCK, NKV) * wc[None, :, NKV:ROW_W], axis=1)

    gab_ref[...] = mm(_C_GAB, _C_END)

    tg = _dot_nt(wtg_ref[...], hb) + btg_ref[...]
    gt_ref[...] = tg[0:8]
    ngt_ref[0] = tg[16:32]
    ngt_ref[1] = tg[32:48]
    tb = _dot_nt(wtb_ref[...], hb) + btb_ref[...]
    mkt_ref[...] = tb[0:MQK].astype(BF16)
    for j in range(tm // tks):
        vst_ref[j] = tb[MQK:MQK + NKV, j * tks:(j + 1) * tks].astype(BF16)
    for j in range(tm // tkw):
        vwt_ref[j] = tb[MQK + NKV:MQK + 2 * NKV, j * tkw:(j + 1) * tkw].astype(BF16)


def _in_proj(x, g_pre, sc, sh, wts, rope_tabs, tm, rows_per_mod):
    r, d = x.shape
    nt = r // tm
    tks, tkw = min(TKS, tm), min(TKW, tm)
    tpm = rows_per_mod // tm
    tpr = rope_tabs[0].shape[0] // tm
    mr = sc.shape[1]
    row = lambda i: (i, 0)
    const2 = lambda i: (0, 0)
    in_specs = [
        pl.BlockSpec((tm, d), row),
        pl.BlockSpec((1, d), const2),
        pl.BlockSpec((1, mr, d), lambda i: (i // tpm, 0, 0)),
        pl.BlockSpec((1, mr, d), lambda i: (i // tpm, 0, 0)),
        pl.BlockSpec((d, _C_END), const2),
        pl.BlockSpec((1, _C_END), const2),
        pl.BlockSpec((48, d), const2),
        pl.BlockSpec((48, 1), const2),
        pl.BlockSpec((512, d), const2),
        pl.BlockSpec((512, 1), const2),
        pl.BlockSpec((tm, LANES), lambda i: (i % tpr, 0)),
        pl.BlockSpec((tm, LANES), lambda i: (i % tpr, 0)),
        pl.BlockSpec((tm, LANES), lambda i: (i % tpr, 0)),
        pl.BlockSpec((CMP_BLOCK, ROW_W), const2),
    ]
    outs = [
        ("mq", (r, MQK), BF16, pl.BlockSpec((tm, MQK), row)),
        ("mk", (r, MQK), BF16, pl.BlockSpec((tm, MQK), row)),
        ("mv", (r, MV), BF16, pl.BlockSpec((tm, MV), row)),
        ("mo", (r, MV), F32, pl.BlockSpec((tm, MV), row)),
        ("nq", (N_HEADS, r, HEAD_DIM), BF16, pl.BlockSpec((N_HEADS, tm, HEAD_DIM), lambda i: (0, i, 0))),
        ("cmp", (r, ROW_W), F32, pl.BlockSpec((tm, ROW_W), row)),
        ("slc", (r, ROW_W), F32, pl.BlockSpec((tm, ROW_W), row)),
        ("win", (r, ROW_W), F32, pl.BlockSpec((tm, ROW_W), row)),
        ("ks", (N_KV, r, HEAD_DIM), BF16, pl.BlockSpec((N_KV, tm, HEAD_DIM), lambda i: (0, i, 0))),
        ("kw", (N_KV, r, HEAD_DIM), BF16, pl.BlockSpec((N_KV, tm, HEAD_DIM), lambda i: (0, i, 0))),
        ("gab", (r, 2 * d), F32, pl.BlockSpec((tm, 2 * d), row)),
        ("kcvc", (r // CMP_BLOCK, ROW_W), F32, pl.BlockSpec((tm // CMP_BLOCK, ROW_W), row)),
        ("gt", (8, r), F32, pl.BlockSpec((8, tm), lambda i: (0, i))),
        ("ngt", (N_KV, 16, r), F32, pl.BlockSpec((N_KV, 16, tm), lambda i: (0, 0, i))),
        ("mkt", (MQK, r), BF16, pl.BlockSpec((MQK, tm), lambda i: (0, i))),
        ("vst", (r // tks, NKV, tks), BF16, pl.BlockSpec((tm // tks, NKV, tks), lambda i: (i, 0, 0))),
        ("vwt", (r // tkw, NKV, tkw), BF16, pl.BlockSpec((tm // tkw, NKV, tkw), lambda i: (i, 0, 0))),
    ]
    res = pl.pallas_call(
        functools.partial(_proj_kernel, tks=tks, tkw=tkw),
        grid=(nt,),
        in_specs=in_specs,
        out_specs=[o[3] for o in outs],
        out_shape=[jax.ShapeDtypeStruct(o[1], o[2]) for o in outs],
        compiler_params=_cparams("parallel"),
        name="in_proj",
    )(x, g_pre.reshape(1, d), sc, sh, wts["w_main"], wts["b_main"], wts["w_tg"], wts["b_tg"],
      wts["w_tb"], wts["b_tb"], *rope_tabs, wts["w_cmp"])
    return {o[0]: v for o, v in zip(outs, res)}


def _mlstm_chunk_kernel(q_ref, k_ref, kt_ref, v_ref, mo_ref, gt_ref, nw_ref,
                        ya_ref, c_out_ref, m_out_ref, caug, mstate):
    t = q_ref.shape[0]
    c = pl.program_id(1)

    @pl.when(c == 0)
    def _():
        caug[...] = jnp.zeros_like(caug)
        mstate[...] = jnp.zeros_like(mstate)

    gc = GATE_CAP * jnp.tanh(gt_ref[...] / GATE_CAP)
    lane = lax.broadcasted_iota(jnp.int32, (8, t), 1)
    b = _log_sigmoid(gc)
    k = 1
    while k < t:
        b = b + jnp.where(lane >= k, pltpu.roll(b, k, 1), 0.0)
        k *= 2

    row = lax.broadcasted_iota(jnp.int32, (t, t), 0)
    col = lax.broadcasted_iota(jnp.int32, (t, t), 1)
    causal = col <= row
    eye = col == row
    lane_v = lax.broadcasted_iota(jnp.int32, (t, M_DV), 1)

    def to_col(r):
        return jnp.sum(jnp.where(eye, r, 0.0), axis=1, keepdims=True)

    for h in range(M_HEADS):
        qs, vs_ = slice(h * M_DQK, (h + 1) * M_DQK), slice(h * M_DV, (h + 1) * M_DV)
        irow = gc[h:h + 1, :]
        brow = b[M_HEADS + h:M_HEADS + h + 1, :]
        bcol = to_col(brow)
        mprev = mstate[h:h + 1, 0:1]
        d = jnp.where(causal, bcol - brow + irow, NEG)
        minter = bcol + mprev
        mt = jnp.maximum(minter, jnp.max(d, axis=1, keepdims=True))
        qh, kh, vh = q_ref[:, qs], k_ref[:, qs], v_ref[:, vs_]
        w = _dot_nt(qh, kh) * jnp.exp(d - mt)
        ainter = jnp.exp(minter - mt)
        ca = caug[h]
        qc = _dot(qh, ca.astype(BF16))
        num = _dot(w.astype(BF16), vh) + ainter * qc[:, 0:M_DV]
        den = jnp.sum(w, axis=1, keepdims=True) + ainter * qc[:, M_DV:M_DV + 1]
        hh = num / jnp.maximum(jnp.abs(den), jnp.exp(-mt))
        y = _rms(hh, nw_ref[:, vs_])
        ya_ref[:, vs_] = (jax.nn.sigmoid(mo_ref[:, vs_]) * y).astype(BF16)
        b_last = brow[:, t - 1:t]
        grow = b_last - brow + irow
        mnew = jnp.maximum(b_last + mprev, jnp.max(grow, axis=1, keepdims=True))
        ain = to_col(jnp.exp(grow - mnew))
        ast = jnp.exp(b_last + mprev - mnew)
        vsa = jnp.concatenate([ain * vh.astype(F32), jnp.where(lane_v == 0, ain, 0.0)], axis=1)
        caug[h] = ast * ca + _dot(kt_ref[qs, :], vsa.astype(BF16))
        mstate[h:h + 1, :] = jnp.broadcast_to(mnew, (1, LANES))

    @pl.when(c == pl.num_programs(1) - 1)
    def _():
        c_out_ref[0] = caug[...]
        m_out_ref[0] = mstate[...]


def _mlstm_prompt(p, norm_w, nb, s):
    t = T_CHUNK
    assert s % t == 0
    nc = s // t
    r = nb * s
    rowc = lambda b, c: (b * nc + c, 0)
    ya, caug, mst = pl.pallas_call(
        _mlstm_chunk_kernel,
        grid=(nb, nc),
        in_specs=[pl.BlockSpec((t, MQK), rowc),
                  pl.BlockSpec((t, MQK), rowc),
                  pl.BlockSpec((MQK, t), lambda b, c: (0, b * nc + c)),
                  pl.BlockSpec((t, MV), rowc),
                  pl.BlockSpec((t, MV), rowc),
                  pl.BlockSpec((8, t), lambda b, c: (0, b * nc + c)),
                  pl.BlockSpec((1, MV), lambda b, c: (0, 0))],
        out_specs=[pl.BlockSpec((t, MV), rowc),
                   pl.BlockSpec((1, M_HEADS, M_DQK, 2 * M_DV), lambda b, c: (b, 0, 0, 0)),
                   pl.BlockSpec((1, 8, LANES), lambda b, c: (b, 0, 0))],
        out_shape=[jax.ShapeDtypeStruct((r, MV), BF16),
                   jax.ShapeDtypeStruct((nb, M_HEADS, M_DQK, 2 * M_DV), F32),
                   jax.ShapeDtypeStruct((nb, 8, LANES), F32)],
        scratch_shapes=[pltpu.VMEM((M_HEADS, M_DQK, 2 * M_DV), F32), pltpu.VMEM((8, LANES), F32)],
        compiler_params=_cparams("parallel", "arbitrary"),
        name="mlstm_prompt",
    )(p["mq"], p["mk"], p["mkt"], p["mv"], p["mo"], p["gt"], norm_w.reshape(1, MV))
    return ya, caug[..., :M_DV], caug[..., M_DV], mst[:, :M_HEADS, 0]


def _mlstm_step_kernel(q_ref, k_ref, v_ref, mo_ref, g_ref, c_ref, n_ref, m_ref, nw_ref,
                       ya_ref, co_ref, no_ref, mo_out_ref):
    bb = q_ref.shape[0]
    q, k, v = q_ref[...], k_ref[...], v_ref[...]
    gc = GATE_CAP * jnp.tanh(g_ref[...] / GATE_CAP)
    lf = _log_sigmoid(gc)
    m0 = m_ref[...]
    eye = lax.broadcasted_iota(jnp.int32, (M_DQK, M_DQK), 0) == lax.broadcasted_iota(jnp.int32, (M_DQK, M_DQK), 1)

    def to_col(r):
        return jnp.sum(jnp.where(eye, r, 0.0), axis=1, keepdims=True)

    for b in range(bb):
        for h in range(M_HEADS):
            qs, vs_ = slice(h * M_DQK, (h + 1) * M_DQK), slice(h * M_DV, (h + 1) * M_DV)
            qh, kh, vh = q[b:b + 1, qs], k[b:b + 1, qs], v[b:b + 1, vs_]
            c0 = c_ref[b, h]
            n0 = n_ref[b, h:h + 1, :]
            ipre = gc[b:b + 1, h:h + 1]
            minter = lf[b:b + 1, M_HEADS + h:M_HEADS + h + 1] + m0[b:b + 1, h:h + 1]
            mt = jnp.maximum(minter, ipre)
            ain = jnp.exp(ipre - mt)
            ast = jnp.exp(minter - mt)
            w = jnp.sum(qh * kh, axis=1, keepdims=True) * ain
            qc = jnp.sum(to_col(qh) * c0, axis=0, keepdims=True)
            num = w * vh + ast * qc
            den = w + ast * jnp.sum(qh * n0, axis=1, keepdims=True)
            hh = num / jnp.maximum(jnp.abs(den), jnp.exp(-mt))
            y = _rms(hh, nw_ref[:, vs_])
            ya_ref[b:b + 1, vs_] = (jax.nn.sigmoid(mo_ref[b:b + 1, vs_]) * y).astype(ya_ref.dtype)
            co_ref[b, h] = ast * c0 + to_col(kh) * (ain * vh)
            no_ref[b, h:h + 1, :] = ast * n0 + ain * kh
            mo_out_ref[b:b + 1, h:h + 1] = mt


def _mlstm_step(p, gcol, c0, n0, m0, norm_w):
    nb = c0.shape[0]
    bb = 8
    row = lambda i: (i, 0)
    return pl.pallas_call(
        _mlstm_step_kernel,
        grid=(nb // bb,),
        in_specs=[pl.BlockSpec((bb, MQK), row), pl.BlockSpec((bb, MQK), row), pl.BlockSpec((bb, MV), row),
                  pl.BlockSpec((bb, MV), row), pl.BlockSpec((bb, 8), row),
                  pl.BlockSpec((bb, M_HEADS, M_DQK, M_DV), lambda i: (i, 0, 0, 0)),
                  pl.BlockSpec((bb, M_HEADS, M_DQK), lambda i: (i, 0, 0)),
                  pl.BlockSpec((bb, M_HEADS), row),
                  pl.BlockSpec((1, MV), lambda i: (0, 0))],
        out_specs=[pl.BlockSpec((bb, MV), row),
                   pl.BlockSpec((bb, M_HEADS, M_DQK, M_DV), lambda i: (i, 0, 0, 0)),
                   pl.BlockSpec((bb, M_HEADS, M_DQK), lambda i: (i, 0, 0)),
                   pl.BlockSpec((bb, M_HEADS), row)],
        out_shape=[jax.ShapeDtypeStruct((nb, MV), F32),
                   jax.ShapeDtypeStruct(c0.shape, F32),
                   jax.ShapeDtypeStruct(n0.shape, F32),
                   jax.ShapeDtypeStruct(m0.shape, F32)],
        compiler_params=_cparams("parallel"),
        name="mlstm_step",
    )(p["mq"].astype(F32), p["mk"].astype(F32), p["mv"].astype(F32), p["mo"], gcol, c0, n0, m0,
      norm_w.reshape(1, MV))


def _online_update(carry, s, vt):
    m, l, acc = carry
    mn = jnp.maximum(m, jnp.max(s, axis=0, keepdims=True))
    alpha = jnp.exp(m - mn)
    p = jnp.exp(s - mn)
    l = alpha * l + jnp.sum(p, axis=0, keepdims=True)
    acc = alpha * acc + _dot(vt, p.astype(BF16))
    return mn, l, acc


def _nsa_prompt_kernel(q_ref, kc_ref, vct_ref, ks_ref, vst_ref, kw_ref, vwt_ref, ng_ref, o_ref, bias_ref):
    tq = q_ref.shape[1]
    r4 = HPG * tq
    tks, tkw = vst_ref.shape[-1], vwt_ref.shape[-1]
    ncmp = kc_ref.shape[0]
    nsel = ncmp // 2
    i = pl.program_id(2)
    t0 = i * tq
    q = q_ref[...].reshape(r4, HEAD_DIM)
    lane_t = t0 + (lax.broadcasted_iota(jnp.int32, (1, r4), 1) & (tq - 1))

    sc = _dot_nt(kc_ref[...], q)
    rr = lax.broadcasted_iota(jnp.int32, (ncmp, 1), 0)
    jc = jnp.where(rr < nsel, 2 * rr, 2 * (rr - nsel) + 1)
    valid = ((jc + 1) * CMP_BLOCK - 1) <= lane_t
    sm = jnp.where(valid, sc, NEG)
    e = jnp.exp(sm - jnp.max(sm, axis=0, keepdims=True))
    pc = jnp.where(valid, e / jnp.sum(e, axis=0, keepdims=True), 0.0)
    oc = _dot(vct_ref[...], pc.astype(BF16))
    pp = pc[0:nsel] + pc[nsel:ncmp]
    imp = pp[:, 0:tq]
    for h in range(1, HPG):
        imp = imp + pp[:, h * tq:(h + 1) * tq]

    jb = lax.broadcasted_iota(jnp.int32, (nsel, tq), 0)
    tt = t0 + lax.broadcasted_iota(jnp.int32, (nsel, tq), 1)
    forced = (jb == (tt >> 6)) | (jb == 0)
    future = jb * SEL_BLOCK > tt
    key = jnp.where(forced, IMP_FORCED, jnp.where(future, IMP_FUTURE, imp))
    jf = jb.astype(F32)
    bias = jnp.full((nsel, tq), NEG, F32)
    for _ in range(min(TOP_K, nsel)):
        cur = jnp.max(key, axis=0, keepdims=True)
        first = jnp.min(jnp.where(key == cur, jf, float(nsel)), axis=0, keepdims=True)
        pick = jf == first
        bias = jnp.where(pick, 0.0, bias)
        key = jnp.where(pick, IMP_TAKEN, key)
    bias_ref[...] = jnp.concatenate([bias] * HPG, axis=1)

    init = (jnp.full((1, r4), NEG, F32), jnp.zeros((1, r4), F32), jnp.zeros((HEAD_DIM, r4), F32))
    nblk = tks // SEL_BLOCK

    def sel_scores(kt):
        kk = ks_ref[pl.ds(pl.multiple_of(kt * tks, tks), tks), :]
        s = _dot_nt(kk, q)
        rows = [jnp.broadcast_to(bias_ref[pl.ds(kt * nblk + u, 1), :], (SEL_BLOCK, r4)) for u in range(nblk)]
        return s + jnp.concatenate(rows, axis=0)

    def sel_body(kt, carry):
        return _online_update(carry, sel_scores(kt), vst_ref[kt])

    kd = t0 // tks
    carry = lax.fori_loop(0, kd, sel_body, init)
    kpos = kd * tks + lax.broadcasted_iota(jnp.int32, (tks, 1), 0)
    sd = jnp.where(kpos <= lane_t, sel_scores(kd), NEG)
    _, ls, accs = _online_update(carry, sd, vst_ref[kd])

    def win_body(wt, carry):
        kk = kw_ref[pl.ds(pl.multiple_of(wt * tkw, tkw), tkw), :]
        dpos = lane_t - (wt * tkw + lax.broadcasted_iota(jnp.int32, (tkw, 1), 0))
        s = jnp.where((dpos >= 0) & (dpos < WINDOW), _dot_nt(kk, q), NEG)
        return _online_update(carry, s, vwt_ref[wt])

    w_hi = (t0 + tq - 1) // tkw
    w_lo = jnp.maximum(t0 - WINDOW, 0) // tkw
    _, lw, accw = lax.fori_loop(w_lo, w_hi + 1, win_body, init)

    ng = jax.nn.sigmoid(ng_ref[...])

    def gate(br):
        return jnp.concatenate([jnp.broadcast_to(ng[h * 3 + br:h * 3 + br + 1, :], (1, tq))
                                for h in range(HPG)], axis=1)

    out_t = gate(0) * oc + gate(1) * (accs / ls) + gate(2) * (accw / lw)
    stacked = jnp.concatenate([out_t[:, h * tq:(h + 1) * tq] for h in range(HPG)], axis=0)
    o_ref[...] = stacked.T.astype(o_ref.dtype)


def _nsa_prompt(p, kc, vct, nb, s):
    tq = TQ
    assert s % max(tq, TKS) == 0 and tq % TKW == 0 and tq & (tq - 1) == 0
    nqb = s // tq
    r = nb * s
    ncmp = s // CMP_BLOCK
    gw = HPG * HEAD_DIM
    vst = p["vst"].reshape(r // TKS, N_KV, HEAD_DIM, TKS)
    vwt = p["vwt"].reshape(r // TKW, N_KV, HEAD_DIM, TKW)
    return pl.pallas_call(
        _nsa_prompt_kernel,
        grid=(nb, N_KV, nqb),
        in_specs=[pl.BlockSpec((HPG, tq, HEAD_DIM), lambda b, g, i: (g, b * nqb + i, 0)),
                  pl.BlockSpec((None, None, ncmp, HEAD_DIM), lambda b, g, i: (b, g, 0, 0)),
                  pl.BlockSpec((None, None, HEAD_DIM, ncmp), lambda b, g, i: (b, g, 0, 0)),
                  pl.BlockSpec((None, s, HEAD_DIM), lambda b, g, i: (g, b, 0)),
                  pl.BlockSpec((s // TKS, None, HEAD_DIM, TKS), lambda b, g, i: (b, g, 0, 0)),
                  pl.BlockSpec((None, s, HEAD_DIM), lambda b, g, i: (g, b, 0)),
                  pl.BlockSpec((s // TKW, None, HEAD_DIM, TKW), lambda b, g, i: (b, g, 0, 0)),
                  pl.BlockSpec((None, 16, tq), lambda b, g, i: (g, 0, b * nqb + i))],
        out_specs=pl.BlockSpec((tq, gw), lambda b, g, i: (b * nqb + i, g)),
        out_shape=jax.ShapeDtypeStruct((r, NQ), BF16),
        scratch_shapes=[pltpu.VMEM((s // SEL_BLOCK, HPG * tq), F32)],
        compiler_params=_cparams("parallel", "parallel", "arbitrary"),
        name="nsa_prompt",
    )(p["nq"], kc, vct, p["ks"], vst, p["kw"], vwt, p["ngt"])


def _prompt_cmp_operands(kcvc, nb, s):
    ncmp = s // CMP_BLOCK
    a = kcvc.reshape(nb, ncmp // 2, 2, 2, N_KV, HEAD_DIM)
    a = jnp.transpose(a, (0, 3, 4, 2, 1, 5)).reshape(nb, 2, N_KV, ncmp, HEAD_DIM)
    kc = a[:, 0].astype(BF16)
    vct = jnp.swapaxes(a[:, 1], -1, -2).astype(BF16)
    return kc, vct


def _nsa_dec_cmp_kernel(pt_ref, *refs, pos):
    del pt_ref
    pages = refs[:PAGES_PER_STEP]
    wlo_ref, whi_ref, q_ref, oc_ref, sel_ref, kcvc = refs[PAGES_PER_STEP:]
    pc = pl.program_id(1)
    rows = jnp.concatenate([pg[...] for pg in pages], axis=0)
    npair = rows.shape[0] // SEL_BLOCK
    r3 = rows.reshape(npair, SEL_BLOCK, ROW_W)
    half = kcvc.shape[0] // 2
    kcvc[pl.ds(pl.multiple_of(pc * npair, npair), npair), :] = jnp.sum(r3 * wlo_ref[...][None], axis=1)
    kcvc[pl.ds(pl.multiple_of(half + pc * npair, npair), npair), :] = jnp.sum(r3 * whi_ref[...][None], axis=1)

    @pl.when(pc == pl.num_programs(1) - 1)
    def _():
        ncmp = kcvc.shape[0]
        q = q_ref[...]
        kv = kcvc[...].astype(BF16)
        s = _dot_nt(q.astype(BF16), kv)
        cc = lax.broadcasted_iota(jnp.int32, (1, ncmp), 1)
        jc = jnp.where(cc < half, 2 * cc, 2 * (cc - half) + 1)
        valid = ((jc + 1) * CMP_BLOCK - 1) <= pos
        sm = jnp.where(valid, s, NEG)
        e = jnp.exp(sm - jnp.max(sm, axis=1, keepdims=True))
        p = jnp.where(valid, e / jnp.sum(e, axis=1, keepdims=True), 0.0)
        oc_ref[...] = _dot(p.astype(BF16), kv)
        pp = p[:, 0:half] + p[:, half:ncmp]
        jl = lax.broadcasted_iota(jnp.int32, (1, half), 1)
        ii = lax.broadcasted_iota(jnp.int32, (half, half), 0)
        jj = lax.broadcasted_iota(jnp.int32, (half, half), 1)
        k_past = min(TOP_K, half + 1) - 1
        for g in range(N_KV):
            imp = jnp.sum(pp[g * HPG:(g + 1) * HPG], axis=0, keepdims=True)
            forced = (jl == pos // SEL_BLOCK) | (jl == 0)
            key = jnp.where(forced, IMP_FORCED, jnp.where(jl * SEL_BLOCK > pos, IMP_FUTURE, imp))
            kcol = jnp.sum(jnp.where(ii == jj, key, 0.0), axis=1, keepdims=True)
            ahead = (kcol > key) | ((kcol == key) & (ii < jj))
            rank = jnp.sum(jnp.where(ahead, 1.0, 0.0), axis=0, keepdims=True)
            sel = jnp.where(rank < k_past, 1.0, 0.0)
            sel_ref[g * HPG:(g + 1) * HPG, :] = jnp.broadcast_to(sel, (HPG, half))


def _page_specs(n_pages):
    def spec(kk):
        return pl.BlockSpec((None, PAGE_SIZE, ROW_W), lambda b, pc, pt: (pt[b, pc * PAGES_PER_STEP + kk], 0, 0))
    del n_pages
    return [spec(kk) for kk in range(PAGES_PER_STEP)]


def _nsa_dec_cmp(cache, page_table, qblk, wlo, whi):
    nb, n_pages = page_table.shape
    assert n_pages % PAGES_PER_STEP == 0
    npc = n_pages // PAGES_PER_STEP
    past = n_pages * PAGE_SIZE
    ncmp = past // CMP_BLOCK
    nselp = ncmp // 2
    const2 = lambda b, pc, pt: (0, 0)
    grid_spec = pltpu.PrefetchScalarGridSpec(
        num_scalar_prefetch=1,
        grid=(nb, npc),
        in_specs=_page_specs(n_pages) + [
            pl.BlockSpec((SEL_BLOCK, ROW_W), const2),
            pl.BlockSpec((SEL_BLOCK, ROW_W), const2),
            pl.BlockSpec((None, N_HEADS, ROW_W), lambda b, pc, pt: (b, 0, 0))],
        out_specs=[pl.BlockSpec((None, N_HEADS, ROW_W), lambda b, pc, pt: (b, 0, 0)),
                   pl.BlockSpec((None, N_HEADS, nselp), lambda b, pc, pt: (b, 0, 0))],
        scratch_shapes=[pltpu.VMEM((ncmp, ROW_W), F32)],
    )
    return pl.pallas_call(
        functools.partial(_nsa_dec_cmp_kernel, pos=past),
        grid_spec=grid_spec,
        out_shape=[jax.ShapeDtypeStruct((nb, N_HEADS, ROW_W), F32),
                   jax.ShapeDtypeStruct((nb, N_HEADS, nselp), F32)],
        compiler_params=_cparams("parallel", "arbitrary"),
        name="nsa_decode_cmp",
    )(page_table, *([cache] * PAGES_PER_STEP), wlo, whi, qblk)


def _nsa_dec_sel_kernel(pt_ref, *refs, pos, wb):
    del pt_ref
    pages = refs[:PAGES_PER_STEP]
    (q_ref, sel_ref, exp_ref, oc_ref, ng_ref, nslc_ref, nwin_ref, win_ref,
     o_ref, wout_ref, m_s, l_s, acc_s) = refs[PAGES_PER_STEP:]
    pc = pl.program_id(1)
    q = q_ref[...]
    qb = q.astype(BF16)

    @pl.when(pc == 0)
    def _():
        m_s[...] = jnp.full_like(m_s, NEG)
        l_s[...] = jnp.zeros_like(l_s)
        acc_s[...] = jnp.zeros_like(acc_s)

    rows = jnp.concatenate([pg[...] for pg in pages], axis=0).astype(BF16)
    keep = _dot(sel_ref[...].astype(BF16), exp_ref[...])
    s = jnp.where(keep > 0.5, _dot_nt(qb, rows), NEG)
    m = m_s[:, 0:1]
    mn = jnp.maximum(m, jnp.max(s, axis=1, keepdims=True))
    alpha = jnp.exp(m - mn)
    p = jnp.exp(s - mn)
    l_s[...] = alpha * l_s[...] + jnp.sum(p, axis=1, keepdims=True)
    acc_s[...] = alpha * acc_s[...] + _dot(p.astype(BF16), rows)
    m_s[...] = jnp.broadcast_to(mn, m_s.shape)

    @pl.when(pc == pl.num_programs(1) - 1)
    def _():
        nslc = nslc_ref[...]
        sn = jnp.sum(q * nslc, axis=1, keepdims=True)
        m1 = m_s[:, 0:1]
        m2 = jnp.maximum(m1, sn)
        a1, pn = jnp.exp(m1 - m2), jnp.exp(sn - m2)
        o_sel = (a1 * acc_s[...] + pn * nslc) / (a1 * l_s[:, 0:1] + pn)
        win = win_ref[...]
        nwin = nwin_ref[...]
        wbf = win.astype(BF16)
        sw = _dot_nt(qb, wbf)
        dpos = pos - (pos - wb + lax.broadcasted_iota(jnp.int32, (1, wb), 1))
        okw = (dpos >= 0) & (dpos < WINDOW) & (pos - dpos >= 0)
        sw = jnp.where(okw, sw, NEG)
        swn = jnp.sum(q * nwin, axis=1, keepdims=True)
        mw = jnp.maximum(jnp.max(sw, axis=1, keepdims=True), swn)
        pw = jnp.exp(sw - mw)
        pwn = jnp.exp(swn - mw)
        o_win = (_dot(pw.astype(BF16), wbf) + pwn * nwin) / (jnp.sum(pw, axis=1, keepdims=True) + pwn)
        gates = jax.nn.sigmoid(ng_ref[...])
        o_ref[...] = gates[:, 0:1] * oc_ref[...] + gates[:, 1:2] * o_sel + gates[:, 2:3] * o_win
        rid = lax.broadcasted_iota(jnp.int32, (wb, 1), 0)
        wout_ref[...] = jnp.where(rid == wb - 1, nwin, pltpu.roll(win, wb - 1, 0))


def _nsa_dec_sel(cache, page_table, qblk, sel, expand, oc, ng, nslc, nwin, cache_win):
    nb, n_pages = page_table.shape
    npc = n_pages // PAGES_PER_STEP
    past = n_pages * PAGE_SIZE
    wb = cache_win.shape[1]
    assert wb == WINDOW and past >= WINDOW
    nselp = sel.shape[-1]
    keys = PAGES_PER_STEP * PAGE_SIZE
    perb = lambda b, pc, pt: (b, 0, 0)
    grid_spec = pltpu.PrefetchScalarGridSpec(
        num_scalar_prefetch=1,
        grid=(nb, npc),
        in_specs=_page_specs(n_pages) + [
            pl.BlockSpec((None, N_HEADS, ROW_W), perb),
            pl.BlockSpec((None, N_HEADS, nselp), perb),
            pl.BlockSpec((None, nselp, keys), lambda b, pc, pt: (pc, 0, 0)),
            pl.BlockSpec((None, N_HEADS, ROW_W), perb),
            pl.BlockSpec((None, N_HEADS, LANES), perb),
            pl.BlockSpec((None, 1, ROW_W), perb),
            pl.BlockSpec((None, 1, ROW_W), perb),
            pl.BlockSpec((None, wb, ROW_W), perb)],
        out_specs=[pl.BlockSpec((None, N_HEADS, ROW_W), perb),
                   pl.BlockSpec((None, wb, ROW_W), perb)],
        scratch_shapes=[pltpu.VMEM((N_HEADS, LANES), F32), pltpu.VMEM((N_HEADS, LANES), F32),
                        pltpu.VMEM((N_HEADS, ROW_W), F32)],
    )
    return pl.pallas_call(
        functools.partial(_nsa_dec_sel_kernel, pos=past, wb=wb),
        grid_spec=grid_spec,
        out_shape=[jax.ShapeDtypeStruct((nb, N_HEADS, ROW_W), F32),
                   jax.ShapeDtypeStruct((nb, wb, ROW_W), F32)],
        compiler_params=_cparams("parallel", "arbitrary"),
        name="nsa_decode_sel",
    )(page_table, *([cache] * PAGES_PER_STEP), qblk, sel, expand, oc, ng, nslc, nwin, cache_win)


def _merge_kernel(x_ref, ya_ref, yb_ref, gab_ref, ga_ref, wa_ref, wb_ref, wo_ref, g_ref, o_ref):
    d = x_ref.shape[1]
    pa = _dot(ya_ref[...], wa_ref[...])
    pb = _dot(yb_ref[...], wb_ref[...])
    merged = jax.nn.sigmoid(gab_ref[:, 0:d]) * pa + jax.nn.sigmoid(gab_ref[:, d:2 * d]) * pb
    z = _dot(merged.astype(BF16), wo_ref[...])
    o_ref[...] = x_ref[...] + ga_ref[0] * _rms(z, g_ref[...])


def _merge(x, ya, yb, gab, ga1, w_b, wts, g_post, tm, rows_per_mod):
    r, d = x.shape
    wyb = yb.shape[1]
    tpm = rows_per_mod // tm
    mr = ga1.shape[1]
    row = lambda i: (i, 0)
    const2 = lambda i: (0, 0)
    return pl.pallas_call(
        _merge_kernel,
        grid=(r // tm,),
        in_specs=[pl.BlockSpec((tm, d), row), pl.BlockSpec((tm, MV), row), pl.BlockSpec((tm, wyb), row),
                  pl.BlockSpec((tm, 2 * d), row),
                  pl.BlockSpec((1, mr, d), lambda i: (i // tpm, 0, 0)),
                  pl.BlockSpec((MV, d), const2), pl.BlockSpec((wyb, d), const2), pl.BlockSpec((d, d), const2),
                  pl.BlockSpec((1, d), const2)],
        out_specs=pl.BlockSpec((tm, d), row),
        out_shape=jax.ShapeDtypeStruct((r, d), F32),
        compiler_params=_cparams("parallel"),
        name="merge",
    )(x, ya, yb, gab, ga1, wts["w_a"], w_b, wts["w_o"], g_post.reshape(1, d))


def _ffn_kernel(x_ref, sc_ref, sh_ref, ga_ref, g1_ref, g2_ref, wu_ref, wd_ref, o_ref):
    dff = wd_ref.shape[0]
    x = x_ref[...]
    hb = (_rms(x, g1_ref[...]) * (1.0 + sc_ref[0]) + sh_ref[0]).astype(BF16)
    acc = jnp.zeros(x.shape, F32)
    for c in range(dff // FFN_CK):
        lo, hi = c * FFN_CK, (c + 1) * FFN_CK
        gate = _dot(hb, wu_ref[:, lo:hi])
        up = _dot(hb, wu_ref[:, dff + lo:dff + hi])
        acc = acc + _dot((gate * jax.nn.sigmoid(gate) * up).astype(BF16), wd_ref[lo:hi, :])
    o_ref[...] = x + ga_ref[0] * _rms(acc, g2_ref[...])


def _ffn(x, sc, sh, ga2, g_pre, g_post, wts, tm, rows_per_mod):
    r, d = x.shape
    dff = wts["w_down"].shape[0]
    assert dff % FFN_CK == 0
    tpm = rows_per_mod // tm
    mr = sc.shape[1]
    row = lambda i: (i, 0)
    const2 = lambda i: (0, 0)
    mod = pl.BlockSpec((1, mr, d), lambda i: (i // tpm, 0, 0))
    return pl.pallas_call(
        _ffn_kernel,
        grid=(r // tm,),
        in_specs=[pl.BlockSpec((tm, d), row), mod, mod, mod,
                  pl.BlockSpec((1, d), const2), pl.BlockSpec((1, d), const2),
                  pl.BlockSpec((d, 2 * dff), const2), pl.BlockSpec((dff, d), const2)],
        out_specs=pl.BlockSpec((tm, d), row),
        out_shape=jax.ShapeDtypeStruct((r, d), F32),
        compiler_params=_cparams("parallel"),
        name="ffn",
    )(x, sc, sh, ga2, g_pre.reshape(1, d), g_post.reshape(1, d), wts["w_up"], wts["w_down"])


def _prep_weights(w_in, b_in, w_cmp_k, w_cmp_v, w_proj_a, w_proj_b, w_out, w_up, w_down):
    o_mi, o_nq, o_ng, o_ga = 4 * 256 + 2 * 256, 1544, 2824, 2848
    o_nkv = o_nq + NQ
    main_cols = [slice(0, o_mi), slice(o_nq, o_ng), slice(o_ga, w_in.shape[1])]
    w_main = jnp.concatenate([w_in[:, s] for s in main_cols], axis=1).astype(BF16)
    b_main = jnp.concatenate([b_in[s] for s in main_cols]).reshape(1, -1)
    z = lambda n: jnp.zeros((w_in.shape[0], n), w_in.dtype)
    ng0, ng1 = slice(o_ng, o_ng + 12), slice(o_ng + 12, o_ng + 24)
    tg_cols = [w_in[:, o_mi:o_mi + 8], z(8), w_in[:, ng0], z(4), w_in[:, ng1], z(4)]
    tg_b = [b_in[o_mi:o_mi + 8], jnp.zeros(8), b_in[ng0], jnp.zeros(4), b_in[ng1], jnp.zeros(4)]
    vs_, vw_ = slice(o_nkv + 3 * NKV, o_nkv + 4 * NKV), slice(o_nkv + 5 * NKV, o_nkv + 6 * NKV)
    tb_cols = [w_in[:, 256:512], w_in[:, vs_], w_in[:, vw_]]
    tb_b = [b_in[256:512], b_in[vs_], b_in[vw_]]
    w_cmp = jnp.concatenate([w_cmp_k, w_cmp_k, w_cmp_v, w_cmp_v], axis=1)
    zc = jnp.zeros_like(w_cmp)
    return {
        "w_main": w_main, "b_main": b_main,
        "w_tg": jnp.concatenate(tg_cols, axis=1).T.astype(BF16),
        "b_tg": jnp.concatenate(tg_b).reshape(-1, 1).astype(F32),
        "w_tb": jnp.concatenate(tb_cols, axis=1).T.astype(BF16),
        "b_tb": jnp.concatenate(tb_b).reshape(-1, 1),
        "w_cmp": w_cmp,
        "w_cmp_lo": jnp.concatenate([w_cmp, zc], axis=0),
        "w_cmp_hi": jnp.concatenate([zc, w_cmp], axis=0),
        "w_a": w_proj_a.astype(BF16), "w_b": w_proj_b.astype(BF16), "w_o": w_out.astype(BF16),
        "w_b_rows": jnp.concatenate(
            [jnp.pad(w_proj_b[h * HEAD_DIM:(h + 1) * HEAD_DIM],
                     ((NKV + (h // HPG) * HEAD_DIM, ROW_W - NKV - (h // HPG + 1) * HEAD_DIM), (0, 0)))
             for h in range(N_HEADS)], axis=0).astype(BF16),
        "w_up": w_up.astype(BF16), "w_down": w_down.astype(BF16),
    }


def _rope_tables(pos):
    half = ROPE_DIM // 2
    n = pos.shape[0]
    inv = ROPE_THETA ** (-jnp.arange(half, dtype=F32) * 2.0 / ROPE_DIM)
    ang = pos.astype(F32)[:, None] * inv[None, :]
    cos, sin = jnp.cos(ang), jnp.sin(ang)
    rest = HEAD_DIM - ROPE_DIM
    zh, zr = jnp.zeros((n, half), F32), jnp.zeros((n, rest), F32)
    a = jnp.concatenate([cos, cos, jnp.ones((n, rest), F32)], axis=1)
    p = jnp.concatenate([zh, sin, zr], axis=1)
    m = jnp.concatenate([-sin, zh, zr], axis=1)
    return tuple(jnp.tile(t, (1, LANES // HEAD_DIM)) for t in (a, p, m))


def _block_expand(n_pages):
    npc = n_pages // PAGES_PER_STEP
    keys = PAGES_PER_STEP * PAGE_SIZE
    nselp = n_pages * PAGE_SIZE // SEL_BLOCK
    blk = (jnp.arange(npc)[:, None] * keys + jnp.arange(keys)[None, :]) // SEL_BLOCK
    return (jnp.arange(nselp)[None, :, None] == blk[:, None, :]).astype(BF16)


def _layer_prompt(x, mod, wts, norms, mlstm_norm_w):
    nb, s, d = x.shape
    r = nb * s
    sh1, sc1, ga1, sh2, sc2, ga2 = [m.reshape(nb, 1, d) for m in jnp.split(mod, 6, axis=-1)]
    g_pre_mix, g_post_mix, g_pre_ffn, g_post_ffn = norms
    tm = TM_PROJ
    assert s % tm == 0
    tabs = _rope_tables(jnp.arange(s, dtype=jnp.int32))
    x2 = x.reshape(r, d)
    p = _in_proj(x2, g_pre_mix, sc1, sh1, wts, tabs, tm, s)
    ya, c_new, n_new, m_new = _mlstm_prompt(p, mlstm_norm_w, nb, s)
    kc, vct = _prompt_cmp_operands(p["kcvc"], nb, s)
    yb = _nsa_prompt(p, kc, vct, nb, s)
    x1 = _merge(x2, ya, yb, p["gab"], ga1, wts["w_b"], wts, g_post_mix, tm, s)
    y = _ffn(x1, sc2, sh2, ga2, g_pre_ffn, g_post_ffn, wts, tm, s)
    rows = lambda a: a.reshape(nb, s, 2, N_KV, HEAD_DIM)
    win = rows(p["win"])[:, -min(WINDOW, s):]
    return y.reshape(nb, s, d), (rows(p["cmp"]), rows(p["slc"]), win, c_new, n_new, m_new)


def _layer_sample(x, mod, wts, norms, mlstm_norm_w, cache_cmp, cache_slc, cache_win, page_table, c0, n0, m0):
    nb, ts, d = x.shape
    assert ts == 1
    sh1, sc1, ga1, sh2, sc2, ga2 = [m.reshape(1, nb, d) for m in jnp.split(mod, 6, axis=-1)]
    g_pre_mix, g_post_mix, g_pre_ffn, g_post_ffn = norms
    n_pages = page_table.shape[1]
    past = n_pages * PAGE_SIZE
    tm = nb
    tabs = _rope_tables(jnp.full((nb,), past, jnp.int32))
    x2 = x.reshape(nb, d)
    p = _in_proj(x2, g_pre_mix, sc1, sh1, wts, tabs, tm, nb)
    ya, c_new, n_new, m_new = _mlstm_step(p, p["gt"].T, c0, n0, m0, mlstm_norm_w)
    qs = jnp.transpose(p["nq"], (1, 0, 2)).astype(F32)
    qblk = jnp.concatenate(
        [jnp.pad(qs[:, g * HPG:(g + 1) * HPG], ((0, 0), (0, 0), (g * HEAD_DIM, ROW_W - (g + 1) * HEAD_DIM)))
         for g in range(N_KV)], axis=1)
    ng = jnp.transpose(p["ngt"][:, :HPG * 3], (2, 0, 1)).reshape(nb, N_HEADS, 3)
    ng = jnp.pad(ng, ((0, 0), (0, 0), (0, LANES - 3)))
    n_pool = cache_cmp.shape[0]
    oc, sel = _nsa_dec_cmp(cache_cmp.reshape(n_pool, PAGE_SIZE, ROW_W), page_table, qblk,
                           wts["w_cmp_lo"], wts["w_cmp_hi"])
    ocomb, win_new = _nsa_dec_sel(cache_slc.reshape(n_pool, PAGE_SIZE, ROW_W), page_table, qblk, sel,
                                  _block_expand(n_pages), oc, ng, p["slc"].reshape(nb, 1, ROW_W),
                                  p["win"].reshape(nb, 1, ROW_W), cache_win.reshape(nb, -1, ROW_W))
    yb = ocomb.reshape(nb, N_HEADS * ROW_W).astype(BF16)
    x1 = _merge(x2, ya.astype(BF16), yb, p["gab"], ga1, wts["w_b_rows"], wts, g_post_mix, tm, nb)
    y = _ffn(x1, sc2, sh2, ga2, g_pre_ffn, g_post_ffn, wts, tm, nb)
    rows = lambda a: a.reshape(nb, -1, 2, N_KV, HEAD_DIM)
    return y.reshape(nb, 1, d), (rows(p["cmp"]), rows(p["slc"]), rows(win_new), c_new, n_new, m_new)


def kernel(x_prompt, x_sample, c_prompt, c_sample, cache_cmp_kv, cache_slc_kv, cache_win_kv, state_mlstm_C, state_mlstm_n, state_mlstm_m, page_table, g_pre_mix, g_post_mix, g_pre_ffn, g_post_ffn, w_ada, b_ada, w_in, b_in, mlstm_norm_w, w_cmp_k, w_cmp_v, w_proj_a, w_proj_b, w_out, w_up, w_down):
    depth = w_in.shape[0]
    nbp, nbs = x_prompt.shape[0], x_sample.shape[0]
    xp, xs = x_prompt, x_sample
    c_all = jnp.concatenate([c_prompt, c_sample], axis=0)
    pad = (-c_all.shape[0]) % 8
    c_all = jnp.pad(c_all, ((0, pad), (0, 0)))
    st_p = [[] for _ in range(6)]
    st_s = [[] for _ in range(6)]
    for l in range(depth):
        wts = _prep_weights(w_in[l], b_in[l], w_cmp_k[l], w_cmp_v[l], w_proj_a[l], w_proj_b[l], w_out[l],
                            w_up[l], w_down[l])
        norms = (g_pre_mix[l], g_post_mix[l], g_pre_ffn[l], g_post_ffn[l])
        mod = _adaln(c_all, w_ada[l], b_ada[l])
        xp, sp = _layer_prompt(xp, mod[:nbp], wts, norms, mlstm_norm_w[l])
        xs, ss = _layer_sample(xs, mod[nbp:nbp + nbs], wts, norms, mlstm_norm_w[l], cache_cmp_kv[l],
                               cache_slc_kv[l], cache_win_kv[l], page_table, state_mlstm_C[l],
                               state_mlstm_n[l], state_mlstm_m[l])
        for j in range(6):
            st_p[j].append(sp[j])
            st_s[j].append(ss[j])
    outs_p = [jnp.stack(a, axis=0) for a in st_p]
    outs_s = [jnp.stack(a, axis=0) for a in st_s]
    return (xp, xs, *outs_p, *outs_s)
```

```python
import functools

import jax
import jax.numpy as jnp
from jax import lax
from jax.experimental import pallas as pl
from jax.experimental.pallas import tpu as pltpu

F32 = jnp.float32
BF16 = jnp.bfloat16

M_HEADS, M_DQK, M_DV = 4, 64, 128
GATE_CAP = 15.0
N_HEADS, N_KV, HEAD_DIM = 8, 2, 64
HPG = N_HEADS // N_KV
CMP_BLOCK, SEL_BLOCK, TOP_K, WINDOW = 32, 64, 16, 512
ROPE_DIM = HEAD_DIM // 4
ROPE_THETA = 500000.0
PAGE_SIZE = 128
EPS = 1e-6
NEG = -1e30
LOG2E = 1.4426950408889634
MQK, MV = M_HEADS * M_DQK, M_HEADS * M_DV
NQ, NKV = N_HEADS * HEAD_DIM, N_KV * HEAD_DIM
ROW_W = 2 * NKV

LANES = 128
VMEM_LIMIT = 48 * 1024 * 1024

TM_PROJ = 512
T_CHUNK = 128
TQ = 128
TKS = 512
TKV = 128
PAGES_PER_STEP = 8
FFN_CK = 256

IMP_FORCED, IMP_FUTURE, IMP_TAKEN = 1e30, -1e30, -2e30


def _dot(a, b):
    return jnp.dot(a, b, preferred_element_type=F32)


def _dot_nt(a, b):
    return lax.dot_general(a, b, (((1,), (1,)), ((), ())), preferred_element_type=F32)


def _rms(x, g):
    return x * lax.rsqrt(jnp.mean(x * x, axis=-1, keepdims=True) + EPS) * g


def _log_sigmoid(x):
    return jnp.minimum(x, 0.0) - jnp.log1p(jnp.exp(-jnp.abs(x)))


def _cparams(*sem):
    return pltpu.CompilerParams(dimension_semantics=sem, vmem_limit_bytes=VMEM_LIMIT)


def _adaln_kernel(c_ref, w_ref, b_ref, o_ref):
    c = c_ref[...]
    a = (c * jax.nn.sigmoid(c)).astype(BF16)
    o_ref[...] = _dot(a, w_ref[...].astype(BF16)) + b_ref[...]


def _adaln(c, w, b):
    r, d = c.shape
    n = w.shape[1]
    tn = 1536
    return pl.pallas_call(
        _adaln_kernel,
        grid=(n // tn,),
        in_specs=[pl.BlockSpec((r, d), lambda j: (0, 0)),
                  pl.BlockSpec((d, tn), lambda j: (0, j)),
                  pl.BlockSpec((1, tn), lambda j: (0, j))],
        out_specs=pl.BlockSpec((r, tn), lambda j: (0, j)),
        out_shape=jax.ShapeDtypeStruct((r, n), F32),
        compiler_params=_cparams("arbitrary"),
        name="adaln",
    )(c, w, b.reshape(1, n))


_C_MQ, _C_MK, _C_MV, _C_MO, _C_NQ, _C_NKV, _C_GAB, _C_END = 0, 256, 512, 1024, 1536, 2048, 2816, 4864


def _proj_kernel(x_ref, g_ref, sc_ref, sh_ref, wm_ref, bm_ref, wtg_ref, btg_ref, wtb_ref, btb_ref,
                 ra_ref, rp_ref, rm_ref, wc_ref,
                 mq_ref, mk_ref, mv_ref, mo_ref, nq_ref, cmp_ref, slc_ref, win_ref, ks_ref, kw_ref,
                 gab_ref, kcvc_ref, gt_ref, ngt_ref, mkt_ref, vst_ref, vwt_ref, *, tks, tkw):
    tm = x_ref.shape[0]
    x = x_ref[...]
    h = _rms(x, g_ref[...]) * (1.0 + sc_ref[0]) + sh_ref[0]
    hb = h.astype(BF16)

    def mm(lo, hi):
        return _dot(hb, wm_ref[:, lo:hi]) + bm_ref[:, lo:hi]

    ra, rp, rm = ra_ref[...], rp_ref[...], rm_ref[...]

    def rope(xc):
        return xc * ra + pltpu.roll(xc, 8, 1) * rp + pltpu.roll(xc, LANES - 8, 1) * rm

    scale_m = M_DQK ** -0.5
    scale_n = HEAD_DIM ** -0.5 * LOG2E
    mq_ref[...] = (mm(_C_MQ, _C_MK) * scale_m).astype(BF16)
    mk_ref[...] = mm(_C_MK, _C_MV).astype(BF16)
    mv_ref[...] = mm(_C_MV, _C_MO).astype(BF16)
    mo_ref[...] = mm(_C_MO, _C_NQ)

    nq = mm(_C_NQ, _C_NKV)
    for c in range(NQ // LANES):
        r = (rope(nq[:, c * LANES:(c + 1) * LANES]) * scale_n).astype(BF16)
        nq_ref[2 * c] = r[:, :HEAD_DIM]
        nq_ref[2 * c + 1] = r[:, HEAD_DIM:]

    nkv = mm(_C_NKV, _C_GAB)
    kc = rope(nkv[:, 0:128])
    vc = nkv[:, 128:256]
    ksr = rope(nkv[:, 256:384])
    kwr = rope(nkv[:, 512:640])
    cmp_ref[:, 0:NKV] = kc
    cmp_ref[:, NKV:ROW_W] = vc
    slc_ref[:, 0:NKV] = ksr
    slc_ref[:, NKV:ROW_W] = nkv[:, 384:512]
    win_ref[:, 0:NKV] = kwr
    win_ref[:, NKV:ROW_W] = nkv[:, 640:768]
    for g in range(N_KV):
        ks_ref[g] = ksr[:, g * HEAD_DIM:(g + 1) * HEAD_DIM].astype(BF16)
        kw_ref[g] = kwr[:, g * HEAD_DIM:(g + 1) * HEAD_DIM].astype(BF16)

    wc = wc_ref[...]
    nb = tm // CMP_BLOCK
    kcvc_ref[:, 0:NKV] = jnp.sum(kc.reshape(nb, CMP_BLOCK, NKV) * wc[None, :, 0:NKV], axis=1)
    kcvc_ref[:, NKV:ROW_W] = jnp.sum(vc.reshape(nb, CMP_BLOCK, NKV) * wc[None, :, NKV:ROW_W], axis=1)

    gab_ref[...] = mm(_C_GAB, _C_END)

    tg = _dot_nt(wtg_ref[...], hb) + btg_ref[...]
    gt_ref[...] = tg[0:8]
    ngt_ref[0] = tg[16:32]
    ngt_ref[1] = tg[32:48]
    tb = _dot_nt(wtb_ref[...], hb) + btb_ref[...]
    mkt_ref[...] = tb[0:MQK].astype(BF16)
    for j in range(tm // tks):
        vst_ref[j] = tb[MQK:MQK + NKV, j * tks:(j + 1) * tks].astype(BF16)
    for j in range(tm // tkw):
        vwt_ref[j] = tb[MQK + NKV:MQK + 2 * NKV, j * tkw:(j + 1) * tkw].astype(BF16)


def _in_proj(x, g_pre, sc, sh, wts, rope_tabs, tm, rows_per_mod):
    r, d = x.shape
    nt = r // tm
    tks = tkw = min(TKV, tm)
    tpm = rows_per_mod // tm
    tpr = rope_tabs[0].shape[0] // tm
    mr = sc.shape[1]
    row = lambda i: (i, 0)
    const2 = lambda i: (0, 0)
    in_specs = [
        pl.BlockSpec((tm, d), row),
        pl.BlockSpec((1, d), const2),
        pl.BlockSpec((1, mr, d), lambda i: (i // tpm, 0, 0)),
        pl.BlockSpec((1, mr, d), lambda i: (i // tpm, 0, 0)),
        pl.BlockSpec((d, _C_END), const2),
        pl.BlockSpec((1, _C_END), const2),
        pl.BlockSpec((48, d), const2),
        pl.BlockSpec((48, 1), const2),
        pl.BlockSpec((512, d), const2),
        pl.BlockSpec((512, 1), const2),
        pl.BlockSpec((tm, LANES), lambda i: (i % tpr, 0)),
        pl.BlockSpec((tm, LANES), lambda i: (i % tpr, 0)),
        pl.BlockSpec((tm, LANES), lambda i: (i % tpr, 0)),
        pl.BlockSpec((CMP_BLOCK, ROW_W), const2),
    ]
    outs = [
        ("mq", (r, MQK), BF16, pl.BlockSpec((tm, MQK), row)),
        ("mk", (r, MQK), BF16, pl.BlockSpec((tm, MQK), row)),
        ("mv", (r, MV), BF16, pl.BlockSpec((tm, MV), row)),
        ("mo", (r, MV), F32, pl.BlockSpec((tm, MV), row)),
        ("nq", (N_HEADS, r, HEAD_DIM), BF16, pl.BlockSpec((N_HEADS, tm, HEAD_DIM), lambda i: (0, i, 0))),
        ("cmp", (r, ROW_W), F32, pl.BlockSpec((tm, ROW_W), row)),
        ("slc", (r, ROW_W), F32, pl.BlockSpec((tm, ROW_W), row)),
        ("win", (r, ROW_W), F32, pl.BlockSpec((tm, ROW_W), row)),
        ("ks", (N_KV, r, HEAD_DIM), BF16, pl.BlockSpec((N_KV, tm, HEAD_DIM), lambda i: (0, i, 0))),
        ("kw", (N_KV, r, HEAD_DIM), BF16, pl.BlockSpec((N_KV, tm, HEAD_DIM), lambda i: (0, i, 0))),
        ("gab", (r, 2 * d), F32, pl.BlockSpec((tm, 2 * d), row)),
        ("kcvc", (r // CMP_BLOCK, ROW_W), F32, pl.BlockSpec((tm // CMP_BLOCK, ROW_W), row)),
        ("gt", (8, r), F32, pl.BlockSpec((8, tm), lambda i: (0, i))),
        ("ngt", (N_KV, 16, r), F32, pl.BlockSpec((N_KV, 16, tm), lambda i: (0, 0, i))),
        ("mkt", (MQK, r), BF16, pl.BlockSpec((MQK, tm), lambda i: (0, i))),
        ("vst", (r // tks, NKV, tks), BF16, pl.BlockSpec((tm // tks, NKV, tks), lambda i: (i, 0, 0))),
        ("vwt", (r // tkw, NKV, tkw), BF16, pl.BlockSpec((tm // tkw, NKV, tkw), lambda i: (i, 0, 0))),
    ]
    res = pl.pallas_call(
        functools.partial(_proj_kernel, tks=tks, tkw=tkw),
        grid=(nt,),
        in_specs=in_specs,
        out_specs=[o[3] for o in outs],
        out_shape=[jax.ShapeDtypeStruct(o[1], o[2]) for o in outs],
        compiler_params=_cparams("parallel"),
        name="in_proj",
    )(x, g_pre.reshape(1, d), sc, sh, wts["w_main"], wts["b_main"], wts["w_tg"], wts["b_tg"],
      wts["w_tb"], wts["b_tb"], *rope_tabs, wts["w_cmp"])
    return {o[0]: v for o, v in zip(outs, res)}


def _mlstm_chunk_kernel(q_ref, k_ref, kt_ref, v_ref, mo_ref, gt_ref, nw_ref,
                        ya_ref, c_out_ref, m_out_ref, caug, mstate):
    t = q_ref.shape[0]
    c = pl.program_id(1)

    @pl.when(c == 0)
    def _():
        caug[...] = jnp.zeros_like(caug)
        mstate[...] = jnp.zeros_like(mstate)

    gc = GATE_CAP * jnp.tanh(gt_ref[...] / GATE_CAP)
    lane = lax.broadcasted_iota(jnp.int32, (8, t), 1)
    b = _log_sigmoid(gc)
    k = 1
    while k < t:
        b = b + jnp.where(lane >= k, pltpu.roll(b, k, 1), 0.0)
        k *= 2

    row = lax.broadcasted_iota(jnp.int32, (t, t), 0)
    col = lax.broadcasted_iota(jnp.int32, (t, t), 1)
    causal = col <= row
    eye = col == row
    lane_v = lax.broadcasted_iota(jnp.int32, (t, M_DV), 1)

    def to_col(r):
        return jnp.sum(jnp.where(eye, r, 0.0), axis=1, keepdims=True)

    for h in range(M_HEADS):
        qs, vs_ = slice(h * M_DQK, (h + 1) * M_DQK), slice(h * M_DV, (h + 1) * M_DV)
        irow = gc[h:h + 1, :]
        brow = b[M_HEADS + h:M_HEADS + h + 1, :]
        bcol = to_col(brow)
        mprev = mstate[h:h + 1, 0:1]
        d = jnp.where(causal, bcol - brow + irow, NEG)
        minter = bcol + mprev
        mt = jnp.maximum(minter, jnp.max(d, axis=1, keepdims=True))
        qh, kh, vh = q_ref[:, qs], k_ref[:, qs], v_ref[:, vs_]
        w = _dot_nt(qh, kh) * jnp.exp(d - mt)
        ainter = jnp.exp(minter - mt)
        ca = caug[h]
        qc = _dot(qh, ca.astype(BF16))
        num = _dot(w.astype(BF16), vh) + ainter * qc[:, 0:M_DV]
        den = jnp.sum(w, axis=1, keepdims=True) + ainter * qc[:, M_DV:M_DV + 1]
        hh = num / jnp.maximum(jnp.abs(den), jnp.exp(-mt))
        y = _rms(hh, nw_ref[:, vs_])
        ya_ref[:, vs_] = (jax.nn.sigmoid(mo_ref[:, vs_]) * y).astype(BF16)
        b_last = brow[:, t - 1:t]
        grow = b_last - brow + irow
        mnew = jnp.maximum(b_last + mprev, jnp.max(grow, axis=1, keepdims=True))
        ain = to_col(jnp.exp(grow - mnew))
        ast = jnp.exp(b_last + mprev - mnew)
        vsa = jnp.concatenate([ain * vh.astype(F32), jnp.where(lane_v == 0, ain, 0.0)], axis=1)
        caug[h] = ast * ca + _dot(kt_ref[qs, :], vsa.astype(BF16))
        mstate[h:h + 1, :] = jnp.broadcast_to(mnew, (1, LANES))

    @pl.when(c == pl.num_programs(1) - 1)
    def _():
        c_out_ref[0] = caug[...]
        m_out_ref[0] = mstate[...]


def _mlstm_prompt(p, norm_w, nb, s):
    t = T_CHUNK
    assert s % t == 0
    nc = s // t
    r = nb * s
    rowc = lambda b, c: (b * nc + c, 0)
    ya, caug, mst = pl.pallas_call(
        _mlstm_chunk_kernel,
        grid=(nb, nc),
        in_specs=[pl.BlockSpec((t, MQK), rowc),
                  pl.BlockSpec((t, MQK), rowc),
                  pl.BlockSpec((MQK, t), lambda b, c: (0, b * nc + c)),
                  pl.BlockSpec((t, MV), rowc),
                  pl.BlockSpec((t, MV), rowc),
                  pl.BlockSpec((8, t), lambda b, c: (0, b * nc + c)),
                  pl.BlockSpec((1, MV), lambda b, c: (0, 0))],
        out_specs=[pl.BlockSpec((t, MV), rowc),
                   pl.BlockSpec((1, M_HEADS, M_DQK, 2 * M_DV), lambda b, c: (b, 0, 0, 0)),
                   pl.BlockSpec((1, 8, LANES), lambda b, c: (b, 0, 0))],
        out_shape=[jax.ShapeDtypeStruct((r, MV), BF16),
                   jax.ShapeDtypeStruct((nb, M_HEADS, M_DQK, 2 * M_DV), F32),
                   jax.ShapeDtypeStruct((nb, 8, LANES), F32)],
        scratch_shapes=[pltpu.VMEM((M_HEADS, M_DQK, 2 * M_DV), F32), pltpu.VMEM((8, LANES), F32)],
        compiler_params=_cparams("parallel", "arbitrary"),
        name="mlstm_prompt",
    )(p["mq"], p["mk"], p["mkt"], p["mv"], p["mo"], p["gt"], norm_w.reshape(1, MV))
    return ya, caug[..., :M_DV], caug[..., M_DV], mst[:, :M_HEADS, 0]


def _mlstm_step_kernel(q_ref, k_ref, v_ref, mo_ref, g_ref, c_ref, n_ref, m_ref, nw_ref,
                       ya_ref, co_ref, no_ref, mo_out_ref):
    bb = q_ref.shape[0]
    q, k, v = q_ref[...], k_ref[...], v_ref[...]
    gc = GATE_CAP * jnp.tanh(g_ref[...] / GATE_CAP)
    lf = _log_sigmoid(gc)
    m0 = m_ref[...]
    eye = lax.broadcasted_iota(jnp.int32, (M_DQK, M_DQK), 0) == lax.broadcasted_iota(jnp.int32, (M_DQK, M_DQK), 1)

    def to_col(r):
        return jnp.sum(jnp.where(eye, r, 0.0), axis=1, keepdims=True)

    for b in range(bb):
        for h in range(M_HEADS):
            qs, vs_ = slice(h * M_DQK, (h + 1) * M_DQK), slice(h * M_DV, (h + 1) * M_DV)
            qh, kh, vh = q[b:b + 1, qs], k[b:b + 1, qs], v[b:b + 1, vs_]
            c0 = c_ref[b, h]
            n0 = n_ref[b, h:h + 1, :]
            ipre = gc[b:b + 1, h:h + 1]
            minter = lf[b:b + 1, M_HEADS + h:M_HEADS + h + 1] + m0[b:b + 1, h:h + 1]
            mt = jnp.maximum(minter, ipre)
            ain = jnp.exp(ipre - mt)
            ast = jnp.exp(minter - mt)
            w = jnp.sum(qh * kh, axis=1, keepdims=True) * ain
            qc = jnp.sum(to_col(qh) * c0, axis=0, keepdims=True)
            num = w * vh + ast * qc
            den = w + ast * jnp.sum(qh * n0, axis=1, keepdims=True)
            hh = num / jnp.maximum(jnp.abs(den), jnp.exp(-mt))
            y = _rms(hh, nw_ref[:, vs_])
            ya_ref[b:b + 1, vs_] = (jax.nn.sigmoid(mo_ref[b:b + 1, vs_]) * y).astype(ya_ref.dtype)
            co_ref[b, h] = ast * c0 + to_col(kh) * (ain * vh)
            no_ref[b, h:h + 1, :] = ast * n0 + ain * kh
            mo_out_ref[b:b + 1, h:h + 1] = mt


def _mlstm_step(p, gcol, c0, n0, m0, norm_w):
    nb = c0.shape[0]
    bb = 8
    row = lambda i: (i, 0)
    return pl.pallas_call(
        _mlstm_step_kernel,
        grid=(nb // bb,),
        in_specs=[pl.BlockSpec((bb, MQK), row), pl.BlockSpec((bb, MQK), row), pl.BlockSpec((bb, MV), row),
                  pl.BlockSpec((bb, MV), row), pl.BlockSpec((bb, 8), row),
                  pl.BlockSpec((bb, M_HEADS, M_DQK, M_DV), lambda i: (i, 0, 0, 0)),
                  pl.BlockSpec((bb, M_HEADS, M_DQK), lambda i: (i, 0, 0)),
                  pl.BlockSpec((bb, M_HEADS), row),
                  pl.BlockSpec((1, MV), lambda i: (0, 0))],
        out_specs=[pl.BlockSpec((bb, MV), row),
                   pl.BlockSpec((bb, M_HEADS, M_DQK, M_DV), lambda i: (i, 0, 0, 0)),
                   pl.BlockSpec((bb, M_HEADS, M_DQK), lambda i: (i, 0, 0)),
                   pl.BlockSpec((bb, M_HEADS), row)],
        out_shape=[jax.ShapeDtypeStruct((nb, MV), F32),
                   jax.ShapeDtypeStruct(c0.shape, F32),
                   jax.ShapeDtypeStruct(n0.shape, F32),
                   jax.ShapeDtypeStruct(m0.shape, F32)],
        compiler_params=_cparams("parallel"),
        name="mlstm_step",
    )(p["mq"].astype(F32), p["mk"].astype(F32), p["mv"].astype(F32), p["mo"], gcol, c0, n0, m0,
      norm_w.reshape(1, MV))


def _nsa_prompt_kernel(q_ref, kc_ref, vct_ref, ks_ref, vst_ref, kw_ref, vwt_ref, ng_ref, wb_ref, o_ref,
                       bias_ref, *, tks):
    tq = q_ref.shape[1]
    r4 = HPG * tq
    gw = HPG * HEAD_DIM
    tkv = vst_ref.shape[-1]
    ncmp = kc_ref.shape[1]
    nsel = ncmp // 2
    groups = range(N_KV)
    i = pl.program_id(1)
    t0 = i * tq
    qs = [q_ref[g * HPG:(g + 1) * HPG].reshape(r4, HEAD_DIM) for g in groups]
    lane_t = t0 + (lax.broadcasted_iota(jnp.int32, (1, r4), 1) & (tq - 1))

    rr = lax.broadcasted_iota(jnp.int32, (ncmp, 1), 0)
    jc = jnp.where(rr < nsel, 2 * rr, 2 * (rr - nsel) + 1)
    valid = ((jc + 1) * CMP_BLOCK - 1) <= lane_t
    jb = lax.broadcasted_iota(jnp.int32, (nsel, tq), 0)
    tt = t0 + lax.broadcasted_iota(jnp.int32, (nsel, tq), 1)
    forced = (jb == (tt >> 6)) | (jb == 0)
    future = jb * SEL_BLOCK > tt
    jf = jb.astype(F32)
    ocs, keys = [], []
    for g in groups:
        sm = jnp.where(valid, _dot_nt(kc_ref[g], qs[g]), NEG)
        e = jnp.exp2(sm - jnp.max(sm, axis=0, keepdims=True))
        pc = jnp.where(valid, e * (1.0 / jnp.sum(e, axis=0, keepdims=True)), 0.0)
        ocs.append(_dot(vct_ref[g], pc.astype(BF16)))
        pp = pc[0:nsel] + pc[nsel:ncmp]
        imp = pp[:, 0:tq]
        for h in range(1, HPG):
            imp = imp + pp[:, h * tq:(h + 1) * tq]
        keys.append(jnp.where(forced, IMP_FORCED, jnp.where(future, IMP_FUTURE, imp)))

    biases = [jnp.full((nsel, tq), NEG, F32) for _ in groups]
    for _ in range(min(TOP_K, nsel)):
        for g in groups:
            cur = jnp.max(keys[g], axis=0, keepdims=True)
            first = jnp.min(jnp.where(keys[g] == cur, jf, float(nsel)), axis=0, keepdims=True)
            pick = jf == first
            biases[g] = jnp.where(pick, 0.0, biases[g])
            keys[g] = jnp.where(pick, IMP_TAKEN, keys[g])
    for g in groups:
        bias4 = jnp.concatenate([biases[g]] * HPG, axis=1)
        for j in range(nsel):
            bias_ref[g, j] = jnp.broadcast_to(bias4[j:j + 1, :], (8, r4))

    bpv = tkv // SEL_BLOCK
    vpt = tks // tkv
    nbt = vpt * bpv
    sub = SEL_BLOCK // 8

    def sel_update(carry, kt, causal):
        k0 = pl.multiple_of(kt * tks, tks)
        ss = []
        for g in groups:
            s = _dot_nt(ks_ref[g, pl.ds(k0, tks), :], qs[g])
            s = (s.reshape(nbt, sub, 8, r4) + bias_ref[g, pl.ds(kt * nbt, nbt)][:, None]).reshape(tks, r4)
            if causal:
                s = jnp.where(k0 + lax.broadcasted_iota(jnp.int32, (tks, 1), 0) <= lane_t, s, NEG)
            ss.append(s)
        mid = []
        for g in groups:
            m, l, acc = carry[g]
            mn = jnp.maximum(m, jnp.max(ss[g], axis=0, keepdims=True))
            alpha = jnp.exp2(m - mn)
            pb = jnp.exp2(ss[g] - mn)
            l = alpha * l + jnp.sum(pb, axis=0, keepdims=True)
            mid.append((mn, l, alpha * acc, pb.astype(BF16)))
        out = []
        for g in groups:
            mn, l, acc, pb = mid[g]
            for u in range(vpt):
                acc = acc + _dot(vst_ref[kt * vpt + u, g], pb[u * tkv:(u + 1) * tkv])
            out.append((mn, l, acc))
        return tuple(out)

    init = tuple((jnp.full((1, r4), NEG, F32), jnp.zeros((1, r4), F32), jnp.zeros((HEAD_DIM, r4), F32))
                 for _ in groups)
    kd = t0 // tks
    carry = lax.fori_loop(0, kd, lambda kt, c: sel_update(c, kt, False), init)
    sel = sel_update(carry, kd, True)

    nwv = (WINDOW + tq) // tkv
    w0 = jnp.maximum(t0 - WINDOW, 0) // tkv
    ng = jax.nn.sigmoid(ng_ref[...])
    for g in groups:
        sw = _dot_nt(kw_ref[g, pl.ds(pl.multiple_of(w0 * tkv, tkv), nwv * tkv), :], qs[g]) + wb_ref[...]
        pw = jnp.exp2(sw - jnp.max(sw, axis=0, keepdims=True))
        lw = jnp.sum(pw, axis=0, keepdims=True)
        pw = pw.astype(BF16)
        accw = _dot(vwt_ref[w0, g], pw[0:tkv])
        for u in range(1, nwv):
            accw = accw + _dot(vwt_ref[w0 + u, g], pw[u * tkv:(u + 1) * tkv])

        def gate(br):
            return jnp.concatenate([ng[g, h * 3 + br:h * 3 + br + 1, :] for h in range(HPG)], axis=1)

        _, ls, accs = sel[g]
        out_t = gate(0) * ocs[g] + (gate(1) / ls) * accs + (gate(2) / lw) * accw
        stacked = jnp.concatenate([out_t[:, h * tq:(h + 1) * tq] for h in range(HPG)], axis=0)
        o_ref[:, g * gw:(g + 1) * gw] = stacked.T.astype(o_ref.dtype)


def _window_bias(tq):
    nvar = WINDOW // tq + 1
    v = jnp.arange(nvar)[:, None, None]
    u = jnp.arange(WINDOW + tq)[None, :, None]
    tt = jnp.arange(tq)[None, None, :]
    dpos = v * tq + tt - u
    ok = (dpos >= 0) & (dpos < WINDOW)
    return jnp.tile(jnp.where(ok, 0.0, NEG).astype(F32), (1, 1, HPG))


def _nsa_prompt(p, kc, vct, nb, s):
    tq = TQ
    tks = min(TKS, s)
    assert s % tks == 0 and tq == TKV and tq & (tq - 1) == 0 and tks % TKV == 0 and s >= WINDOW + tq
    nqb = s // tq
    r = nb * s
    ncmp = s // CMP_BLOCK
    gw = HPG * HEAD_DIM
    vst = p["vst"].reshape(r // TKV, N_KV, HEAD_DIM, TKV)
    vwt = p["vwt"].reshape(r // TKV, N_KV, HEAD_DIM, TKV)
    nvar = WINDOW // tq
    return pl.pallas_call(
        functools.partial(_nsa_prompt_kernel, tks=tks),
        grid=(nb, nqb),
        in_specs=[pl.BlockSpec((N_HEADS, tq, HEAD_DIM), lambda b, i: (0, b * nqb + i, 0)),
                  pl.BlockSpec((None, N_KV, ncmp, HEAD_DIM), lambda b, i: (b, 0, 0, 0)),
                  pl.BlockSpec((None, N_KV, HEAD_DIM, ncmp), lambda b, i: (b, 0, 0, 0)),
                  pl.BlockSpec((N_KV, s, HEAD_DIM), lambda b, i: (0, b, 0)),
                  pl.BlockSpec((s // TKV, N_KV, HEAD_DIM, TKV), lambda b, i: (b, 0, 0, 0)),
                  pl.BlockSpec((N_KV, s, HEAD_DIM), lambda b, i: (0, b, 0)),
                  pl.BlockSpec((s // TKV, N_KV, HEAD_DIM, TKV), lambda b, i: (b, 0, 0, 0)),
                  pl.BlockSpec((N_KV, 16, tq), lambda b, i: (0, 0, b * nqb + i)),
                  pl.BlockSpec((None, WINDOW + tq, HPG * tq), lambda b, i: (jnp.minimum(i, nvar), 0, 0))],
        out_specs=pl.BlockSpec((tq, NQ), lambda b, i: (b * nqb + i, 0)),
        out_shape=jax.ShapeDtypeStruct((r, NQ), BF16),
        scratch_shapes=[pltpu.VMEM((N_KV, s // SEL_BLOCK, 8, HPG * tq), F32)],
        compiler_params=_cparams("parallel", "arbitrary"),
        name="nsa_prompt",
    )(p["nq"], kc, vct, p["ks"], vst, p["kw"], vwt, p["ngt"], _window_bias(tq))


def _prompt_cmp_operands(kcvc, nb, s):
    ncmp = s // CMP_BLOCK
    a = kcvc.reshape(nb, ncmp // 2, 2, 2, N_KV, HEAD_DIM)
    a = jnp.transpose(a, (0, 3, 4, 2, 1, 5)).reshape(nb, 2, N_KV, ncmp, HEAD_DIM)
    kc = a[:, 0].astype(BF16)
    vct = jnp.swapaxes(a[:, 1], -1, -2).astype(BF16)
    return kc, vct


def _nsa_dec_cmp_kernel(pt_ref, *refs, pos):
    del pt_ref
    pages = refs[:PAGES_PER_STEP]
    wlo_ref, whi_ref, q_ref, oc_ref, sel_ref, kcvc = refs[PAGES_PER_STEP:]
    pc = pl.program_id(1)
    rows = jnp.concatenate([pg[...] for pg in pages], axis=0)
    npair = rows.shape[0] // SEL_BLOCK
    r3 = rows.reshape(npair, SEL_BLOCK, ROW_W)
    half = kcvc.shape[0] // 2
    kcvc[pl.ds(pl.multiple_of(pc * npair, npair), npair), :] = jnp.sum(r3 * wlo_ref[...][None], axis=1)
    kcvc[pl.ds(pl.multiple_of(half + pc * npair, npair), npair), :] = jnp.sum(r3 * whi_ref[...][None], axis=1)

    @pl.when(pc == pl.num_programs(1) - 1)
    def _():
        ncmp = kcvc.shape[0]
        q = q_ref[...]
        kv = kcvc[...].astype(BF16)
        s = _dot_nt(q.astype(BF16), kv)
        cc = lax.broadcasted_iota(jnp.int32, (1, ncmp), 1)
        jc = jnp.where(cc < half, 2 * cc, 2 * (cc - half) + 1)
        valid = ((jc + 1) * CMP_BLOCK - 1) <= pos
        sm = jnp.where(valid, s, NEG)
        e = jnp.exp2(sm - jnp.max(sm, axis=1, keepdims=True))
        p = jnp.where(valid, e / jnp.sum(e, axis=1, keepdims=True), 0.0)
        oc_ref[...] = _dot(p.astype(BF16), kv)
        pp = p[:, 0:half] + p[:, half:ncmp]
        jl = lax.broadcasted_iota(jnp.int32, (1, half), 1)
        ii = lax.broadcasted_iota(jnp.int32, (half, half), 0)
        jj = lax.broadcasted_iota(jnp.int32, (half, half), 1)
        k_past = min(TOP_K, half + 1) - 1
        for g in range(N_KV):
            imp = jnp.sum(pp[g * HPG:(g + 1) * HPG], axis=0, keepdims=True)
            forced = (jl == pos // SEL_BLOCK) | (jl == 0)
            key = jnp.where(forced, IMP_FORCED, jnp.where(jl * SEL_BLOCK > pos, IMP_FUTURE, imp))
            kcol = jnp.sum(jnp.where(ii == jj, key, 0.0), axis=1, keepdims=True)
            ahead = (kcol > key) | ((kcol == key) & (ii < jj))
            rank = jnp.sum(jnp.where(ahead, 1.0, 0.0), axis=0, keepdims=True)
            sel = jnp.where(rank < k_past, 1.0, 0.0)
            sel_ref[g * HPG:(g + 1) * HPG, :] = jnp.broadcast_to(sel, (HPG, half))


def _page_specs(n_pages):
    def spec(kk):
        return pl.BlockSpec((None, PAGE_SIZE, ROW_W), lambda b, pc, pt: (pt[b, pc * PAGES_PER_STEP + kk], 0, 0))
    del n_pages
    return [spec(kk) for kk in range(PAGES_PER_STEP)]


def _nsa_dec_cmp(cache, page_table, qblk, wlo, whi):
    nb, n_pages = page_table.shape
    assert n_pages % PAGES_PER_STEP == 0
    npc = n_pages // PAGES_PER_STEP
    past = n_pages * PAGE_SIZE
    ncmp = past // CMP_BLOCK
    nselp = ncmp // 2
    const2 = lambda b, pc, pt: (0, 0)
    grid_spec = pltpu.PrefetchScalarGridSpec(
        num_scalar_prefetch=1,
        grid=(nb, npc),
        in_specs=_page_specs(n_pages) + [
            pl.BlockSpec((SEL_BLOCK, ROW_W), const2),
            pl.BlockSpec((SEL_BLOCK, ROW_W), const2),
            pl.BlockSpec((None, N_HEADS, ROW_W), lambda b, pc, pt: (b, 0, 0))],
        out_specs=[pl.BlockSpec((None, N_HEADS, ROW_W), lambda b, pc, pt: (b, 0, 0)),
                   pl.BlockSpec((None, N_HEADS, nselp), lambda b, pc, pt: (b, 0, 0))],
        scratch_shapes=[pltpu.VMEM((ncmp, ROW_W), F32)],
    )
    return pl.pallas_call(
        functools.partial(_nsa_dec_cmp_kernel, pos=past),
        grid_spec=grid_spec,
        out_shape=[jax.ShapeDtypeStruct((nb, N_HEADS, ROW_W), F32),
                   jax.ShapeDtypeStruct((nb, N_HEADS, nselp), F32)],
        compiler_params=_cparams("parallel", "arbitrary"),
        name="nsa_decode_cmp",
    )(page_table, *([cache] * PAGES_PER_STEP), wlo, whi, qblk)


def _nsa_dec_sel_kernel(pt_ref, *refs, pos, wb):
    del pt_ref
    pages = refs[:PAGES_PER_STEP]
    (q_ref, sel_ref, exp_ref, oc_ref, ng_ref, nslc_ref, nwin_ref, win_ref,
     o_ref, wout_ref, m_s, l_s, acc_s) = refs[PAGES_PER_STEP:]
    pc = pl.program_id(1)
    q = q_ref[...]
    qb = q.astype(BF16)

    @pl.when(pc == 0)
    def _():
        m_s[...] = jnp.full_like(m_s, NEG)
        l_s[...] = jnp.zeros_like(l_s)
        acc_s[...] = jnp.zeros_like(acc_s)

    rows = jnp.concatenate([pg[...] for pg in pages], axis=0).astype(BF16)
    keep = _dot(sel_ref[...].astype(BF16), exp_ref[...])
    s = jnp.where(keep > 0.5, _dot_nt(qb, rows), NEG)
    m = m_s[:, 0:1]
    mn = jnp.maximum(m, jnp.max(s, axis=1, keepdims=True))
    alpha = jnp.exp2(m - mn)
    p = jnp.exp2(s - mn)
    l_s[...] = alpha * l_s[...] + jnp.sum(p, axis=1, keepdims=True)
    acc_s[...] = alpha * acc_s[...] + _dot(p.astype(BF16), rows)
    m_s[...] = jnp.broadcast_to(mn, m_s.shape)

    @pl.when(pc == pl.num_programs(1) - 1)
    def _():
        nslc = nslc_ref[...]
        sn = jnp.sum(q * nslc, axis=1, keepdims=True)
        m1 = m_s[:, 0:1]
        m2 = jnp.maximum(m1, sn)
        a1, pn = jnp.exp2(m1 - m2), jnp.exp2(sn - m2)
        o_sel = (a1 * acc_s[...] + pn * nslc) / (a1 * l_s[:, 0:1] + pn)
        win = win_ref[...]
        nwin = nwin_ref[...]
        wbf = win.astype(BF16)
        sw = _dot_nt(qb, wbf)
        dpos = pos - (pos - wb + lax.broadcasted_iota(jnp.int32, (1, wb), 1))
        okw = (dpos >= 0) & (dpos < WINDOW) & (pos - dpos >= 0)
        sw = jnp.where(okw, sw, NEG)
        swn = jnp.sum(q * nwin, axis=1, keepdims=True)
        mw = jnp.maximum(jnp.max(sw, axis=1, keepdims=True), swn)
        pw = jnp.exp2(sw - mw)
        pwn = jnp.exp2(swn - mw)
        o_win = (_dot(pw.astype(BF16), wbf) + pwn * nwin) / (jnp.sum(pw, axis=1, keepdims=True) + pwn)
        gates = jax.nn.sigmoid(ng_ref[...])
        o_ref[...] = gates[:, 0:1] * oc_ref[...] + gates[:, 1:2] * o_sel + gates[:, 2:3] * o_win
        rid = lax.broadcasted_iota(jnp.int32, (wb, 1), 0)
        wout_ref[...] = jnp.where(rid == wb - 1, nwin, pltpu.roll(win, wb - 1, 0))


def _nsa_dec_sel(cache, page_table, qblk, sel, expand, oc, ng, nslc, nwin, cache_win):
    nb, n_pages = page_table.shape
    npc = n_pages // PAGES_PER_STEP
    past = n_pages * PAGE_SIZE
    wb = cache_win.shape[1]
    assert wb == WINDOW and past >= WINDOW
    nselp = sel.shape[-1]
    keys = PAGES_PER_STEP * PAGE_SIZE
    perb = lambda b, pc, pt: (b, 0, 0)
    grid_spec = pltpu.PrefetchScalarGridSpec(
        num_scalar_prefetch=1,
        grid=(nb, npc),
        in_specs=_page_specs(n_pages) + [
            pl.BlockSpec((None, N_HEADS, ROW_W), perb),
            pl.BlockSpec((None, N_HEADS, nselp), perb),
            pl.BlockSpec((None, nselp, keys), lambda b, pc, pt: (pc, 0, 0)),
            pl.BlockSpec((None, N_HEADS, ROW_W), perb),
            pl.BlockSpec((None, N_HEADS, LANES), perb),
            pl.BlockSpec((None, 1, ROW_W), perb),
            pl.BlockSpec((None, 1, ROW_W), perb),
            pl.BlockSpec((None, wb, ROW_W), perb)],
        out_specs=[pl.BlockSpec((None, N_HEADS, ROW_W), perb),
                   pl.BlockSpec((None, wb, ROW_W), perb)],
        scratch_shapes=[pltpu.VMEM((N_HEADS, LANES), F32), pltpu.VMEM((N_HEADS, LANES), F32),
                        pltpu.VMEM((N_HEADS, ROW_W), F32)],
    )
    return pl.pallas_call(
        functools.partial(_nsa_dec_sel_kernel, pos=past, wb=wb),
        grid_spec=grid_spec,
        out_shape=[jax.ShapeDtypeStruct((nb, N_HEADS, ROW_W), F32),
                   jax.ShapeDtypeStruct((nb, wb, ROW_W), F32)],
        compiler_params=_cparams("parallel", "arbitrary"),
        name="nsa_decode_sel",
    )(page_table, *([cache] * PAGES_PER_STEP), qblk, sel, expand, oc, ng, nslc, nwin, cache_win)


def _merge_kernel(x_ref, ya_ref, yb_ref, gab_ref, ga_ref, wa_ref, wb_ref, wo_ref, g_ref, o_ref):
    d = x_ref.shape[1]
    pa = _dot(ya_ref[...], wa_ref[...])
    pb = _dot(yb_ref[...], wb_ref[...])
    merged = jax.nn.sigmoid(gab_ref[:, 0:d]) * pa + jax.nn.sigmoid(gab_ref[:, d:2 * d]) * pb
    z = _dot(merged.astype(BF16), wo_ref[...])
    o_ref[...] = x_ref[...] + ga_ref[0] * _rms(z, g_ref[...])


def _merge(x, ya, yb, gab, ga1, w_b, wts, g_post, tm, rows_per_mod):
    r, d = x.shape
    wyb = yb.shape[1]
    tpm = rows_per_mod // tm
    mr = ga1.shape[1]
    row = lambda i: (i, 0)
    const2 = lambda i: (0, 0)
    return pl.pallas_call(
        _merge_kernel,
        grid=(r // tm,),
        in_specs=[pl.BlockSpec((tm, d), row), pl.BlockSpec((tm, MV), row), pl.BlockSpec((tm, wyb), row),
                  pl.BlockSpec((tm, 2 * d), row),
                  pl.BlockSpec((1, mr, d), lambda i: (i // tpm, 0, 0)),
                  pl.BlockSpec((MV, d), const2), pl.BlockSpec((wyb, d), const2), pl.BlockSpec((d, d), const2),
                  pl.BlockSpec((1, d), const2)],
        out_specs=pl.BlockSpec((tm, d), row),
        out_shape=jax.ShapeDtypeStruct((r, d), F32),
        compiler_params=_cparams("parallel"),
        name="merge",
    )(x, ya, yb, gab, ga1, wts["w_a"], w_b, wts["w_o"], g_post.reshape(1, d))


def _ffn_kernel(x_ref, sc_ref, sh_ref, ga_ref, g1_ref, g2_ref, wu_ref, wd_ref, o_ref):
    dff = wd_ref.shape[0]
    x = x_ref[...]
    hb = (_rms(x, g1_ref[...]) * (1.0 + sc_ref[0]) + sh_ref[0]).astype(BF16)
    acc = jnp.zeros(x.shape, F32)
    for c in range(dff // FFN_CK):
        lo, hi = c * FFN_CK, (c + 1) * FFN_CK
        gate = _dot(hb, wu_ref[:, lo:hi])
        up = _dot(hb, wu_ref[:, dff + lo:dff + hi])
        acc = acc + _dot((gate * jax.nn.sigmoid(gate) * up).astype(BF16), wd_ref[lo:hi, :])
    o_ref[...] = x + ga_ref[0] * _rms(acc, g2_ref[...])


def _ffn(x, sc, sh, ga2, g_pre, g_post, wts, tm, rows_per_mod):
    r, d = x.shape
    dff = wts["w_down"].shape[0]
    assert dff % FFN_CK == 0
    tpm = rows_per_mod // tm
    mr = sc.shape[1]
    row = lambda i: (i, 0)
    const2 = lambda i: (0, 0)
    mod = pl.BlockSpec((1, mr, d), lambda i: (i // tpm, 0, 0))
    return pl.pallas_call(
        _ffn_kernel,
        grid=(r // tm,),
        in_specs=[pl.BlockSpec((tm, d), row), mod, mod, mod,
                  pl.BlockSpec((1, d), const2), pl.BlockSpec((1, d), const2),
                  pl.BlockSpec((d, 2 * dff), const2), pl.BlockSpec((dff, d), const2)],
        out_specs=pl.BlockSpec((tm, d), row),
        out_shape=jax.ShapeDtypeStruct((r, d), F32),
        compiler_params=_cparams("parallel"),
        name="ffn",
    )(x, sc, sh, ga2, g_pre.reshape(1, d), g_post.reshape(1, d), wts["w_up"], wts["w_down"])


def _prep_weights(w_in, b_in, w_cmp_k, w_cmp_v, w_proj_a, w_proj_b, w_out, w_up, w_down):
    o_mi, o_nq, o_ng, o_ga = 4 * 256 + 2 * 256, 1544, 2824, 2848
    o_nkv = o_nq + NQ
    main_cols = [slice(0, o_mi), slice(o_nq, o_ng), slice(o_ga, w_in.shape[1])]
    w_main = jnp.concatenate([w_in[:, s] for s in main_cols], axis=1).astype(BF16)
    b_main = jnp.concatenate([b_in[s] for s in main_cols]).reshape(1, -1)
    z = lambda n: jnp.zeros((w_in.shape[0], n), w_in.dtype)
    ng0, ng1 = slice(o_ng, o_ng + 12), slice(o_ng + 12, o_ng + 24)
    tg_cols = [w_in[:, o_mi:o_mi + 8], z(8), w_in[:, ng0], z(4), w_in[:, ng1], z(4)]
    tg_b = [b_in[o_mi:o_mi + 8], jnp.zeros(8), b_in[ng0], jnp.zeros(4), b_in[ng1], jnp.zeros(4)]
    vs_, vw_ = slice(o_nkv + 3 * NKV, o_nkv + 4 * NKV), slice(o_nkv + 5 * NKV, o_nkv + 6 * NKV)
    tb_cols = [w_in[:, 256:512], w_in[:, vs_], w_in[:, vw_]]
    tb_b = [b_in[256:512], b_in[vs_], b_in[vw_]]
    w_cmp = jnp.concatenate([w_cmp_k, w_cmp_k, w_cmp_v, w_cmp_v], axis=1)
    zc = jnp.zeros_like(w_cmp)
    return {
        "w_main": w_main, "b_main": b_main,
        "w_tg": jnp.concatenate(tg_cols, axis=1).T.astype(BF16),
        "b_tg": jnp.concatenate(tg_b).reshape(-1, 1).astype(F32),
        "w_tb": jnp.concatenate(tb_cols, axis=1).T.astype(BF16),
        "b_tb": jnp.concatenate(tb_b).reshape(-1, 1),
        "w_cmp": w_cmp,
        "w_cmp_lo": jnp.concatenate([w_cmp, zc], axis=0),
        "w_cmp_hi": jnp.concatenate([zc, w_cmp], axis=0),
        "w_a": w_proj_a.astype(BF16), "w_b": w_proj_b.astype(BF16), "w_o": w_out.astype(BF16),
        "w_b_rows": jnp.concatenate(
            [jnp.pad(w_proj_b[h * HEAD_DIM:(h + 1) * HEAD_DIM],
                     ((NKV + (h // HPG) * HEAD_DIM, ROW_W - NKV - (h // HPG + 1) * HEAD_DIM), (0, 0)))
             for h in range(N_HEADS)], axis=0).astype(BF16),
        "w_up": w_up.astype(BF16), "w_down": w_down.astype(BF16),
    }


def _rope_tables(pos):
    half = ROPE_DIM // 2
    n = pos.shape[0]
    inv = ROPE_THETA ** (-jnp.arange(half, dtype=F32) * 2.0 / ROPE_DIM)
    ang = pos.astype(F32)[:, None] * inv[None, :]
    cos, sin = jnp.cos(ang), jnp.sin(ang)
    rest = HEAD_DIM - ROPE_DIM
    zh, zr = jnp.zeros((n, half), F32), jnp.zeros((n, rest), F32)
    a = jnp.concatenate([cos, cos, jnp.ones((n, rest), F32)], axis=1)
    p = jnp.concatenate([zh, sin, zr], axis=1)
    m = jnp.concatenate([-sin, zh, zr], axis=1)
    return tuple(jnp.tile(t, (1, LANES // HEAD_DIM)) for t in (a, p, m))


def _block_expand(n_pages):
    npc = n_pages // PAGES_PER_STEP
    keys = PAGES_PER_STEP * PAGE_SIZE
    nselp = n_pages * PAGE_SIZE // SEL_BLOCK
    blk = (jnp.arange(npc)[:, None] * keys + jnp.arange(keys)[None, :]) // SEL_BLOCK
    return (jnp.arange(nselp)[None, :, None] == blk[:, None, :]).astype(BF16)


def _layer_prompt(x, mod, wts, norms, mlstm_norm_w):
    nb, s, d = x.shape
    r = nb * s
    sh1, sc1, ga1, sh2, sc2, ga2 = [m.reshape(nb, 1, d) for m in jnp.split(mod, 6, axis=-1)]
    g_pre_mix, g_post_mix, g_pre_ffn, g_post_ffn = norms
    tm = TM_PROJ
    assert s % tm == 0
    tabs = _rope_tables(jnp.arange(s, dtype=jnp.int32))
    x2 = x.reshape(r, d)
    p = _in_proj(x2, g_pre_mix, sc1, sh1, wts, tabs, tm, s)
    ya, c_new, n_new, m_new = _mlstm_prompt(p, mlstm_norm_w, nb, s)
    kc, vct = _prompt_cmp_operands(p["kcvc"], nb, s)
    yb = _nsa_prompt(p, kc, vct, nb, s)
    x1 = _merge(x2, ya, yb, p["gab"], ga1, wts["w_b"], wts, g_post_mix, tm, s)
    y = _ffn(x1, sc2, sh2, ga2, g_pre_ffn, g_post_ffn, wts, tm, s)
    rows = lambda a: a.reshape(nb, s, 2, N_KV, HEAD_DIM)
    win = rows(p["win"])[:, -min(WINDOW, s):]
    return y.reshape(nb, s, d), (rows(p["cmp"]), rows(p["slc"]), win, c_new, n_new, m_new)


def _layer_sample(x, mod, wts, norms, mlstm_norm_w, cache_cmp, cache_slc, cache_win, page_table, c0, n0, m0):
    nb, ts, d = x.shape
    assert ts == 1
    sh1, sc1, ga1, sh2, sc2, ga2 = [m.reshape(1, nb, d) for m in jnp.split(mod, 6, axis=-1)]
    g_pre_mix, g_post_mix, g_pre_ffn, g_post_ffn = norms
    n_pages = page_table.shape[1]
    past = n_pages * PAGE_SIZE
    tm = nb
    tabs = _rope_tables(jnp.full((nb,), past, jnp.int32))
    x2 = x.reshape(nb, d)
    p = _in_proj(x2, g_pre_mix, sc1, sh1, wts, tabs, tm, nb)
    ya, c_new, n_new, m_new = _mlstm_step(p, p["gt"].T, c0, n0, m0, mlstm_norm_w)
    qs = jnp.transpose(p["nq"], (1, 0, 2)).astype(F32)
    qblk = jnp.concatenate(
        [jnp.pad(qs[:, g * HPG:(g + 1) * HPG], ((0, 0), (0, 0), (g * HEAD_DIM, ROW_W - (g + 1) * HEAD_DIM)))
         for g in range(N_KV)], axis=1)
    ng = jnp.transpose(p["ngt"][:, :HPG * 3], (2, 0, 1)).reshape(nb, N_HEADS, 3)
    ng = jnp.pad(ng, ((0, 0), (0, 0), (0, LANES - 3)))
    n_pool = cache_cmp.shape[0]
    oc, sel = _nsa_dec_cmp(cache_cmp.reshape(n_pool, PAGE_SIZE, ROW_W), page_table, qblk,
                           wts["w_cmp_lo"], wts["w_cmp_hi"])
    ocomb, win_new = _nsa_dec_sel(cache_slc.reshape(n_pool, PAGE_SIZE, ROW_W), page_table, qblk, sel,
                                  _block_expand(n_pages), oc, ng, p["slc"].reshape(nb, 1, ROW_W),
                                  p["win"].reshape(nb, 1, ROW_W), cache_win.reshape(nb, -1, ROW_W))
    yb = ocomb.reshape(nb, N_HEADS * ROW_W).astype(BF16)
    x1 = _merge(x2, ya.astype(BF16), yb, p["gab"], ga1, wts["w_b_rows"], wts, g_post_mix, tm, nb)
    y = _ffn(x1, sc2, sh2, ga2, g_pre_ffn, g_post_ffn, wts, tm, nb)
    rows = lambda a: a.reshape(nb, -1, 2, N_KV, HEAD_DIM)
    return y.reshape(nb, 1, d), (rows(p["cmp"]), rows(p["slc"]), rows(win_new), c_new, n_new, m_new)


def kernel(x_prompt, x_sample, c_prompt, c_sample, cache_cmp_kv, cache_slc_kv, cache_win_kv, state_mlstm_C, state_mlstm_n, state_mlstm_m, page_table, g_pre_mix, g_post_mix, g_pre_ffn, g_post_ffn, w_ada, b_ada, w_in, b_in, mlstm_norm_w, w_cmp_k, w_cmp_v, w_proj_a, w_proj_b, w_out, w_up, w_down):
    depth = w_in.shape[0]
    nbp, nbs = x_prompt.shape[0], x_sample.shape[0]
    xp, xs = x_prompt, x_sample
    c_all = jnp.concatenate([c_prompt, c_sample], axis=0)
    pad = (-c_all.shape[0]) % 8
    c_all = jnp.pad(c_all, ((0, pad), (0, 0)))
    st_p = [[] for _ in range(6)]
    st_s = [[] for _ in range(6)]
    for l in range(depth):
        wts = _prep_weights(w_in[l], b_in[l], w_cmp_k[l], w_cmp_v[l], w_proj_a[l], w_proj_b[l], w_out[l],
                            w_up[l], w_down[l])
        norms = (g_pre_mix[l], g_post_mix[l], g_pre_ffn[l], g_post_ffn[l])
        mod = _adaln(c_all, w_ada[l], b_ada[l])
        xp, sp = _layer_prompt(xp, mod[:nbp], wts, norms, mlstm_norm_w[l])
        xs, ss = _layer_sample(xs, mod[nbp:nbp + nbs], wts, norms, mlstm_norm_w[l], cache_cmp_kv[l],
                               cache_slc_kv[l], cache_win_kv[l], page_table, state_mlstm_C[l],
                               state_mlstm_n[l], state_mlstm_m[l])
        for j in range(6):
            st_p[j].append(sp[j])
            st_s[j].append(ss[j])
    outs_p = [jnp.stack(a, axis=0) for a in st_p]
    outs_s = [jnp.stack(a, axis=0) for a in st_s]
    return (xp, xs, *outs_p, *outs_s)
```

```python
import functools

import jax
import jax.numpy as jnp
from jax import lax
from jax.experimental import pallas as pl
from jax.experimental.pallas import tpu as pltpu

F32 = jnp.float32
BF16 = jnp.bfloat16

M_HEADS, M_DQK, M_DV = 4, 64, 128
GATE_CAP = 15.0
N_HEADS, N_KV, HEAD_DIM = 8, 2, 64
HPG = N_HEADS // N_KV
CMP_BLOCK, SEL_BLOCK, TOP_K, WINDOW = 32, 64, 16, 512
ROPE_DIM = HEAD_DIM // 4
ROPE_THETA = 500000.0
PAGE_SIZE = 128
EPS = 1e-6
NEG = -1e30
LOG2E = 1.4426950408889634
MQK, MV = M_HEADS * M_DQK, M_HEADS * M_DV
NQ, NKV = N_HEADS * HEAD_DIM, N_KV * HEAD_DIM
ROW_W = 2 * NKV

LANES = 128
VMEM_LIMIT = 48 * 1024 * 1024

TM_PROJ = 512
T_CHUNK = 128
TQ = 128
TKS = 512
TKV = 128
PAGES_PER_STEP = 16
FFN_CK = 256

IMP_FORCED, IMP_FUTURE, IMP_TAKEN = 1e30, -1e30, -2e30


def _dot(a, b):
    return jnp.dot(a, b, preferred_element_type=F32)


def _dot_nt(a, b):
    return lax.dot_general(a, b, (((1,), (1,)), ((), ())), preferred_element_type=F32)


def _rms(x, g):
    return x * lax.rsqrt(jnp.mean(x * x, axis=-1, keepdims=True) + EPS) * g


def _log_sigmoid(x):
    return jnp.minimum(x, 0.0) - jnp.log1p(jnp.exp(-jnp.abs(x)))


def _cparams(*sem):
    return pltpu.CompilerParams(dimension_semantics=sem, vmem_limit_bytes=VMEM_LIMIT)


def _adaln_kernel(c_ref, w_ref, b_ref, o_ref):
    c = c_ref[...]
    a = (c * jax.nn.sigmoid(c)).astype(BF16)
    o_ref[...] = _dot(a, w_ref[...].astype(BF16)) + b_ref[...]


def _adaln(c, w, b):
    r, d = c.shape
    n = w.shape[1]
    tn = 1536
    return pl.pallas_call(
        _adaln_kernel,
        grid=(n // tn,),
        in_specs=[pl.BlockSpec((r, d), lambda j: (0, 0)),
                  pl.BlockSpec((d, tn), lambda j: (0, j)),
                  pl.BlockSpec((1, tn), lambda j: (0, j))],
        out_specs=pl.BlockSpec((r, tn), lambda j: (0, j)),
        out_shape=jax.ShapeDtypeStruct((r, n), F32),
        compiler_params=_cparams("arbitrary"),
        name="adaln",
    )(c, w, b.reshape(1, n))


_C_MQ, _C_MK, _C_MV, _C_MO, _C_NQ, _C_NKV, _C_GAB, _C_END = 0, 256, 512, 1024, 1536, 2048, 2816, 4864


def _proj_kernel(x_ref, g_ref, sc_ref, sh_ref, wm_ref, bm_ref, wtg_ref, btg_ref, wtb_ref, btb_ref,
                 ra_ref, rp_ref, rm_ref, ct_ref, st_ref, wc_ref,
                 mq_ref, mk_ref, mv_ref, mo_ref, nq_ref, cmp_ref, slc_ref, win_ref, ks_ref, kw_ref,
                 gab_ref, kcvc_ref, gt_ref, ngt_ref, mkt_ref, vst_ref, vwt_ref, *, tks, tkw, rows_t):
    tm = x_ref.shape[0]
    x = x_ref[...]
    h = _rms(x, g_ref[...]) * (1.0 + sc_ref[0]) + sh_ref[0]
    hb = h.astype(BF16)

    def mm(lo, hi):
        return _dot(hb, wm_ref[:, lo:hi]) + bm_ref[:, lo:hi]

    ra, rp, rm = ra_ref[...], rp_ref[...], rm_ref[...]

    def rope(xc):
        return xc * ra + pltpu.roll(xc, 8, 1) * rp + pltpu.roll(xc, LANES - 8, 1) * rm

    scale_m = M_DQK ** -0.5
    scale_n = HEAD_DIM ** -0.5 * LOG2E
    mq_ref[...] = (mm(_C_MQ, _C_MK) * scale_m).astype(BF16)
    mk_ref[...] = mm(_C_MK, _C_MV).astype(BF16)
    mv_ref[...] = mm(_C_MV, _C_MO).astype(BF16)
    mo_ref[...] = mm(_C_MO, _C_NQ)

    nq = mm(_C_NQ, _C_NKV)
    for c in range(NQ // LANES):
        r = (rope(nq[:, c * LANES:(c + 1) * LANES]) * scale_n).astype(BF16)
        nq_ref[2 * c] = r[:, :HEAD_DIM]
        nq_ref[2 * c + 1] = r[:, HEAD_DIM:]

    nkv = mm(_C_NKV, _C_GAB)
    kc = rope(nkv[:, 0:128])
    vc = nkv[:, 128:256]
    ksr = rope(nkv[:, 256:384])
    kwr = rope(nkv[:, 512:640])
    if not rows_t:
        cmp_ref[:, 0:NKV] = kc
        cmp_ref[:, NKV:ROW_W] = vc
        slc_ref[:, 0:NKV] = ksr
        slc_ref[:, NKV:ROW_W] = nkv[:, 384:512]
        win_ref[:, 0:NKV] = kwr
        win_ref[:, NKV:ROW_W] = nkv[:, 640:768]
    for g in range(N_KV):
        ks_ref[g] = ksr[:, g * HEAD_DIM:(g + 1) * HEAD_DIM].astype(BF16)
        kw_ref[g] = kwr[:, g * HEAD_DIM:(g + 1) * HEAD_DIM].astype(BF16)

    wc = wc_ref[...]
    nb = tm // CMP_BLOCK
    kcvc_ref[:, 0:NKV] = jnp.sum(kc.reshape(nb, CMP_BLOCK, NKV) * wc[None, :, 0:NKV], axis=1)
    kcvc_ref[:, NKV:ROW_W] = jnp.sum(vc.reshape(nb, CMP_BLOCK, NKV) * wc[None, :, NKV:ROW_W], axis=1)

    gab_ref[...] = mm(_C_GAB, _C_END)

    tg = _dot_nt(wtg_ref[...], hb) + btg_ref[...]
    gt_ref[...] = tg[0:8]
    ngt_ref[0] = tg[16:32]
    ngt_ref[1] = tg[32:48]
    tb = _dot_nt(wtb_ref[...], hb) + btb_ref[...]
    mkt_ref[...] = tb[0:MQK].astype(BF16)
    for j in range(tm // tks):
        vst_ref[j] = tb[MQK:MQK + NKV, j * tks:(j + 1) * tks].astype(BF16)
    for j in range(tm // tkw):
        vwt_ref[j] = tb[MQK + NKV:MQK + 2 * NKV, j * tkw:(j + 1) * tkw].astype(BF16)

    if rows_t:
        ct, st = ct_ref[...], st_ref[...]
        half = ROPE_DIM // 2

        def rope_t(kt):
            parts = []
            for g in range(N_KV):
                blk = kt[g * HEAD_DIM:(g + 1) * HEAD_DIM]
                x1, x2 = blk[0:half], blk[half:ROPE_DIM]
                parts += [x1 * ct - x2 * st, x2 * ct + x1 * st, blk[ROPE_DIM:HEAD_DIM]]
            return jnp.concatenate(parts, axis=0)

        o = MQK + 2 * NKV
        cmp_ref[0:NKV] = rope_t(tb[o:o + NKV])
        cmp_ref[NKV:ROW_W] = tb[o + NKV:o + 2 * NKV]
        slc_ref[0:NKV] = rope_t(tb[o + 2 * NKV:o + 3 * NKV])
        slc_ref[NKV:ROW_W] = tb[MQK:MQK + NKV]
        win_ref[0:NKV] = rope_t(tb[o + 3 * NKV:o + 4 * NKV])
        win_ref[NKV:ROW_W] = tb[MQK + NKV:MQK + 2 * NKV]


def _in_proj(x, g_pre, sc, sh, wts, rope_tabs, tm, rows_per_mod, rows_t):
    r, d = x.shape
    nt = r // tm
    tks = tkw = min(TKV, tm)
    tpm = rows_per_mod // tm
    tpr = rope_tabs[0].shape[0] // tm
    mr = sc.shape[1]
    row = lambda i: (i, 0)
    const2 = lambda i: (0, 0)
    if rows_t:
        rows_shape = (r // rows_per_mod, ROW_W, rows_per_mod)
        rows_spec = pl.BlockSpec((None, ROW_W, tm), lambda i: (i // tpm, 0, i % tpm))
    else:
        rows_shape = (r, ROW_W)
        rows_spec = pl.BlockSpec((tm, ROW_W), row)
    ntb = wts["w_tb"].shape[0]
    in_specs = [
        pl.BlockSpec((tm, d), row),
        pl.BlockSpec((1, d), const2),
        pl.BlockSpec((1, mr, d), lambda i: (i // tpm, 0, 0)),
        pl.BlockSpec((1, mr, d), lambda i: (i // tpm, 0, 0)),
        pl.BlockSpec((d, _C_END), const2),
        pl.BlockSpec((1, _C_END), const2),
        pl.BlockSpec((48, d), const2),
        pl.BlockSpec((48, 1), const2),
        pl.BlockSpec((ntb, d), const2),
        pl.BlockSpec((ntb, 1), const2),
        pl.BlockSpec((tm, LANES), lambda i: (i % tpr, 0)),
        pl.BlockSpec((tm, LANES), lambda i: (i % tpr, 0)),
        pl.BlockSpec((tm, LANES), lambda i: (i % tpr, 0)),
        pl.BlockSpec((ROPE_DIM // 2, tm), lambda i: (0, i % tpr)),
        pl.BlockSpec((ROPE_DIM // 2, tm), lambda i: (0, i % tpr)),
        pl.BlockSpec((CMP_BLOCK, ROW_W), const2),
    ]
    outs = [
        ("mq", (r, MQK), BF16, pl.BlockSpec((tm, MQK), row)),
        ("mk", (r, MQK), BF16, pl.BlockSpec((tm, MQK), row)),
        ("mv", (r, MV), BF16, pl.BlockSpec((tm, MV), row)),
        ("mo", (r, MV), F32, pl.BlockSpec((tm, MV), row)),
        ("nq", (N_HEADS, r, HEAD_DIM), BF16, pl.BlockSpec((N_HEADS, tm, HEAD_DIM), lambda i: (0, i, 0))),
        ("cmp", rows_shape, F32, rows_spec),
        ("slc", rows_shape, F32, rows_spec),
        ("win", rows_shape, F32, rows_spec),
        ("ks", (N_KV, r, HEAD_DIM), BF16, pl.BlockSpec((N_KV, tm, HEAD_DIM), lambda i: (0, i, 0))),
        ("kw", (N_KV, r, HEAD_DIM), BF16, pl.BlockSpec((N_KV, tm, HEAD_DIM), lambda i: (0, i, 0))),
        ("gab", (r, 2 * d), F32, pl.BlockSpec((tm, 2 * d), row)),
        ("kcvc", (r // CMP_BLOCK, ROW_W), F32, pl.BlockSpec((tm // CMP_BLOCK, ROW_W), row)),
        ("gt", (8, r), F32, pl.BlockSpec((8, tm), lambda i: (0, i))),
        ("ngt", (N_KV, 16, r), F32, pl.BlockSpec((N_KV, 16, tm), lambda i: (0, 0, i))),
        ("mkt", (MQK, r), BF16, pl.BlockSpec((MQK, tm), lambda i: (0, i))),
        ("vst", (r // tks, NKV, tks), BF16, pl.BlockSpec((tm // tks, NKV, tks), lambda i: (i, 0, 0))),
        ("vwt", (r // tkw, NKV, tkw), BF16, pl.BlockSpec((tm // tkw, NKV, tkw), lambda i: (i, 0, 0))),
    ]
    res = pl.pallas_call(
        functools.partial(_proj_kernel, tks=tks, tkw=tkw, rows_t=rows_t),
        grid=(nt,),
        in_specs=in_specs,
        out_specs=[o[3] for o in outs],
        out_shape=[jax.ShapeDtypeStruct(o[1], o[2]) for o in outs],
        compiler_params=_cparams("parallel"),
        name="in_proj",
    )(x, g_pre.reshape(1, d), sc, sh, wts["w_main"], wts["b_main"], wts["w_tg"], wts["b_tg"],
      wts["w_tb"], wts["b_tb"], *rope_tabs, wts["w_cmp"])
    return {o[0]: v for o, v in zip(outs, res)}


def _mlstm_chunk_kernel(q_ref, k_ref, kt_ref, v_ref, mo_ref, gt_ref, nw_ref,
                        ya_ref, c_out_ref, m_out_ref, caug, mstate):
    t = q_ref.shape[0]
    c = pl.program_id(1)

    @pl.when(c == 0)
    def _():
        caug[...] = jnp.zeros_like(caug)
        mstate[...] = jnp.zeros_like(mstate)

    gc = GATE_CAP * jnp.tanh(gt_ref[...] / GATE_CAP)
    lane = lax.broadcasted_iota(jnp.int32, (8, t), 1)
    b = _log_sigmoid(gc)
    k = 1
    while k < t:
        b = b + jnp.where(lane >= k, pltpu.roll(b, k, 1), 0.0)
        k *= 2

    row = lax.broadcasted_iota(jnp.int32, (t, t), 0)
    col = lax.broadcasted_iota(jnp.int32, (t, t), 1)
    causal = col <= row
    eye = col == row
    lane_v = lax.broadcasted_iota(jnp.int32, (t, M_DV), 1)

    def to_col(r):
        return jnp.sum(jnp.where(eye, r, 0.0), axis=1, keepdims=True)

    for h in range(M_HEADS):
        qs, vs_ = slice(h * M_DQK, (h + 1) * M_DQK), slice(h * M_DV, (h + 1) * M_DV)
        irow = gc[h:h + 1, :]
        brow = b[M_HEADS + h:M_HEADS + h + 1, :]
        bcol = to_col(brow)
        mprev = mstate[h:h + 1, 0:1]
        d = jnp.where(causal, bcol - brow + irow, NEG)
        minter = bcol + mprev
        mt = jnp.maximum(minter, jnp.max(d, axis=1, keepdims=True))
        qh, kh, vh = q_ref[:, qs], k_ref[:, qs], v_ref[:, vs_]
        w = _dot_nt(qh, kh) * jnp.exp(d - mt)
        ainter = jnp.exp(minter - mt)
        ca = caug[h]
        qc = _dot(qh, ca.astype(BF16))
        num = _dot(w.astype(BF16), vh) + ainter * qc[:, 0:M_DV]
        den = jnp.sum(w, axis=1, keepdims=True) + ainter * qc[:, M_DV:M_DV + 1]
        hh = num / jnp.maximum(jnp.abs(den), jnp.exp(-mt))
        y = _rms(hh, nw_ref[:, vs_])
        ya_ref[:, vs_] = (jax.nn.sigmoid(mo_ref[:, vs_]) * y).astype(BF16)
        b_last = brow[:, t - 1:t]
        grow = b_last - brow + irow
        mnew = jnp.maximum(b_last + mprev, jnp.max(grow, axis=1, keepdims=True))
        ain = to_col(jnp.exp(grow - mnew))
        ast = jnp.exp(b_last + mprev - mnew)
        vsa = jnp.concatenate([ain * vh.astype(F32), jnp.where(lane_v == 0, ain, 0.0)], axis=1)
        caug[h] = ast * ca + _dot(kt_ref[qs, :], vsa.astype(BF16))
        mstate[h:h + 1, :] = jnp.broadcast_to(mnew, (1, LANES))

    @pl.when(c == pl.num_programs(1) - 1)
    def _():
        c_out_ref[0] = caug[...]
        m_out_ref[0] = mstate[...]


def _mlstm_prompt(p, norm_w, nb, s):
    t = T_CHUNK
    assert s % t == 0
    nc = s // t
    r = nb * s
    rowc = lambda b, c: (b * nc + c, 0)
    ya, caug, mst = pl.pallas_call(
        _mlstm_chunk_kernel,
        grid=(nb, nc),
        in_specs=[pl.BlockSpec((t, MQK), rowc),
                  pl.BlockSpec((t, MQK), rowc),
                  pl.BlockSpec((MQK, t), lambda b, c: (0, b * nc + c)),
                  pl.BlockSpec((t, MV), rowc),
                  pl.BlockSpec((t, MV), rowc),
                  pl.BlockSpec((8, t), lambda b, c: (0, b * nc + c)),
                  pl.BlockSpec((1, MV), lambda b, c: (0, 0))],
        out_specs=[pl.BlockSpec((t, MV), rowc),
                   pl.BlockSpec((1, M_HEADS, M_DQK, 2 * M_DV), lambda b, c: (b, 0, 0, 0)),
                   pl.BlockSpec((1, 8, LANES), lambda b, c: (b, 0, 0))],
        out_shape=[jax.ShapeDtypeStruct((r, MV), BF16),
                   jax.ShapeDtypeStruct((nb, M_HEADS, M_DQK, 2 * M_DV), F32),
                   jax.ShapeDtypeStruct((nb, 8, LANES), F32)],
        scratch_shapes=[pltpu.VMEM((M_HEADS, M_DQK, 2 * M_DV), F32), pltpu.VMEM((8, LANES), F32)],
        compiler_params=_cparams("parallel", "arbitrary"),
        name="mlstm_prompt",
    )(p["mq"], p["mk"], p["mkt"], p["mv"], p["mo"], p["gt"], norm_w.reshape(1, MV))
    return ya, caug[..., :M_DV], caug[..., M_DV], mst[:, :M_HEADS, 0]


def _mlstm_step_kernel(q_ref, k_ref, v_ref, mo_ref, g_ref, c_ref, n_ref, m_ref, nw_ref,
                       ya_ref, co_ref, no_ref, mo_out_ref):
    bb = q_ref.shape[0]
    q, k, v = q_ref[...], k_ref[...], v_ref[...]
    gc = GATE_CAP * jnp.tanh(g_ref[...] / GATE_CAP)
    lf = _log_sigmoid(gc)
    m0 = m_ref[...]
    eye = lax.broadcasted_iota(jnp.int32, (M_DQK, M_DQK), 0) == lax.broadcasted_iota(jnp.int32, (M_DQK, M_DQK), 1)

    def to_col(r):
        return jnp.sum(jnp.where(eye, r, 0.0), axis=1, keepdims=True)

    for b in range(bb):
        for h in range(M_HEADS):
            qs, vs_ = slice(h * M_DQK, (h + 1) * M_DQK), slice(h * M_DV, (h + 1) * M_DV)
            qh, kh, vh = q[b:b + 1, qs], k[b:b + 1, qs], v[b:b + 1, vs_]
            c0 = c_ref[b, h]
            n0 = n_ref[b, h:h + 1, :]
            ipre = gc[b:b + 1, h:h + 1]
            minter = lf[b:b + 1, M_HEADS + h:M_HEADS + h + 1] + m0[b:b + 1, h:h + 1]
            mt = jnp.maximum(minter, ipre)
            ain = jnp.exp(ipre - mt)
            ast = jnp.exp(minter - mt)
            w = jnp.sum(qh * kh, axis=1, keepdims=True) * ain
            qc = jnp.sum(to_col(qh) * c0, axis=0, keepdims=True)
            num = w * vh + ast * qc
            den = w + ast * jnp.sum(qh * n0, axis=1, keepdims=True)
            hh = num / jnp.maximum(jnp.abs(den), jnp.exp(-mt))
            y = _rms(hh, nw_ref[:, vs_])
            ya_ref[b:b + 1, vs_] = (jax.nn.sigmoid(mo_ref[b:b + 1, vs_]) * y).astype(ya_ref.dtype)
            co_ref[b, h] = ast * c0 + to_col(kh) * (ain * vh)
            no_ref[b, h:h + 1, :] = ast * n0 + ain * kh
            mo_out_ref[b:b + 1, h:h + 1] = mt


def _mlstm_step(p, gcol, c0, n0, m0, norm_w):
    nb = c0.shape[0]
    bb = 8
    row = lambda i: (i, 0)
    return pl.pallas_call(
        _mlstm_step_kernel,
        grid=(nb // bb,),
        in_specs=[pl.BlockSpec((bb, MQK), row), pl.BlockSpec((bb, MQK), row), pl.BlockSpec((bb, MV), row),
                  pl.BlockSpec((bb, MV), row), pl.BlockSpec((bb, 8), row),
                  pl.BlockSpec((bb, M_HEADS, M_DQK, M_DV), lambda i: (i, 0, 0, 0)),
                  pl.BlockSpec((bb, M_HEADS, M_DQK), lambda i: (i, 0, 0)),
                  pl.BlockSpec((bb, M_HEADS), row),
                  pl.BlockSpec((1, MV), lambda i: (0, 0))],
        out_specs=[pl.BlockSpec((bb, MV), row),
                   pl.BlockSpec((bb, M_HEADS, M_DQK, M_DV), lambda i: (i, 0, 0, 0)),
                   pl.BlockSpec((bb, M_HEADS, M_DQK), lambda i: (i, 0, 0)),
                   pl.BlockSpec((bb, M_HEADS), row)],
        out_shape=[jax.ShapeDtypeStruct((nb, MV), F32),
                   jax.ShapeDtypeStruct(c0.shape, F32),
                   jax.ShapeDtypeStruct(n0.shape, F32),
                   jax.ShapeDtypeStruct(m0.shape, F32)],
        compiler_params=_cparams("parallel"),
        name="mlstm_step",
    )(p["mq"].astype(F32), p["mk"].astype(F32), p["mv"].astype(F32), p["mo"], gcol, c0, n0, m0,
      norm_w.reshape(1, MV))


def _nsa_prompt_kernel(q_ref, kc_ref, vct_ref, ks_ref, vst_ref, kw_ref, vwt_ref, ng_ref, wb_ref, o_ref,
                       bias_ref, *, tks):
    tq = q_ref.shape[1]
    r4 = HPG * tq
    gw = HPG * HEAD_DIM
    tkv = vst_ref.shape[-1]
    ncmp = kc_ref.shape[1]
    nsel = ncmp // 2
    groups = range(N_KV)
    i = pl.program_id(1)
    t0 = i * tq
    qs = [q_ref[g * HPG:(g + 1) * HPG].reshape(r4, HEAD_DIM) for g in groups]
    lane_t = t0 + (lax.broadcasted_iota(jnp.int32, (1, r4), 1) & (tq - 1))

    rr = lax.broadcasted_iota(jnp.int32, (ncmp, 1), 0)
    jc = jnp.where(rr < nsel, 2 * rr, 2 * (rr - nsel) + 1)
    valid = ((jc + 1) * CMP_BLOCK - 1) <= lane_t
    jb = lax.broadcasted_iota(jnp.int32, (nsel, tq), 0)
    tt = t0 + lax.broadcasted_iota(jnp.int32, (nsel, tq), 1)
    forced = (jb == (tt >> 6)) | (jb == 0)
    future = jb * SEL_BLOCK > tt
    jf = jb.astype(F32)
    ocs, keys = [], []
    for g in groups:
        sm = jnp.where(valid, _dot_nt(kc_ref[g], qs[g]), NEG)
        e = jnp.exp2(sm - jnp.max(sm, axis=0, keepdims=True))
        pc = jnp.where(valid, e * (1.0 / jnp.sum(e, axis=0, keepdims=True)), 0.0)
        ocs.append(_dot(vct_ref[g], pc.astype(BF16)))
        pp = pc[0:nsel] + pc[nsel:ncmp]
        imp = pp[:, 0:tq]
        for h in range(1, HPG):
            imp = imp + pp[:, h * tq:(h + 1) * tq]
        keys.append(jnp.where(forced, IMP_FORCED, jnp.where(future, IMP_FUTURE, imp)))

    biases = [jnp.full((nsel, tq), NEG, F32) for _ in groups]
    for _ in range(min(TOP_K, nsel)):
        for g in groups:
            cur = jnp.max(keys[g], axis=0, keepdims=True)
            first = jnp.min(jnp.where(keys[g] == cur, jf, float(nsel)), axis=0, keepdims=True)
            pick = jf == first
            biases[g] = jnp.where(pick, 0.0, biases[g])
            keys[g] = jnp.where(pick, IMP_TAKEN, keys[g])
    for g in groups:
        bias4 = jnp.concatenate([biases[g]] * HPG, axis=1)
        for j in range(nsel):
            bias_ref[g, j] = jnp.broadcast_to(bias4[j:j + 1, :], (8, r4))

    bpv = tkv // SEL_BLOCK
    vpt = tks // tkv
    nbt = vpt * bpv
    sub = SEL_BLOCK // 8

    def sel_update(carry, kt, causal):
        k0 = pl.multiple_of(kt * tks, tks)
        ss = []
        for g in groups:
            s = _dot_nt(ks_ref[g, pl.ds(k0, tks), :], qs[g])
            s = (s.reshape(nbt, sub, 8, r4) + bias_ref[g, pl.ds(kt * nbt, nbt)][:, None]).reshape(tks, r4)
            if causal:
                s = jnp.where(k0 + lax.broadcasted_iota(jnp.int32, (tks, 1), 0) <= lane_t, s, NEG)
            ss.append(s)
        mid = []
        for g in groups:
            m, l, acc = carry[g]
            mn = jnp.maximum(m, jnp.max(ss[g], axis=0, keepdims=True))
            alpha = jnp.exp2(m - mn)
            pb = jnp.exp2(ss[g] - mn)
            l = alpha * l + jnp.sum(pb, axis=0, keepdims=True)
            mid.append((mn, l, alpha * acc, pb.astype(BF16)))
        out = []
        for g in groups:
            mn, l, acc, pb = mid[g]
            for u in range(vpt):
                acc = acc + _dot(vst_ref[kt * vpt + u, g], pb[u * tkv:(u + 1) * tkv])
            out.append((mn, l, acc))
        return tuple(out)

    init = tuple((jnp.full((1, r4), NEG, F32), jnp.zeros((1, r4), F32), jnp.zeros((HEAD_DIM, r4), F32))
                 for _ in groups)
    kd = t0 // tks
    carry = lax.fori_loop(0, kd, lambda kt, c: sel_update(c, kt, False), init)
    sel = sel_update(carry, kd, True)

    nwv = (WINDOW + tq) // tkv
    w0 = jnp.maximum(t0 - WINDOW, 0) // tkv
    ng = jax.nn.sigmoid(ng_ref[...])
    for g in groups:
        sw = _dot_nt(kw_ref[g, pl.ds(pl.multiple_of(w0 * tkv, tkv), nwv * tkv), :], qs[g]) + wb_ref[...]
        pw = jnp.exp2(sw - jnp.max(sw, axis=0, keepdims=True))
        lw = jnp.sum(pw, axis=0, keepdims=True)
        pw = pw.astype(BF16)
        accw = _dot(vwt_ref[w0, g], pw[0:tkv])
        for u in range(1, nwv):
            accw = accw + _dot(vwt_ref[w0 + u, g], pw[u * tkv:(u + 1) * tkv])

        def gate(br):
            return jnp.concatenate([ng[g, h * 3 + br:h * 3 + br + 1, :] for h in range(HPG)], axis=1)

        _, ls, accs = sel[g]
        out_t = gate(0) * ocs[g] + (gate(1) / ls) * accs + (gate(2) / lw) * accw
        stacked = jnp.concatenate([out_t[:, h * tq:(h + 1) * tq] for h in range(HPG)], axis=0)
        o_ref[:, g * gw:(g + 1) * gw] = stacked.T.astype(o_ref.dtype)


def _window_bias(tq):
    nvar = WINDOW // tq + 1
    v = jnp.arange(nvar)[:, None, None]
    u = jnp.arange(WINDOW + tq)[None, :, None]
    tt = jnp.arange(tq)[None, None, :]
    dpos = v * tq + tt - u
    ok = (dpos >= 0) & (dpos < WINDOW)
    return jnp.tile(jnp.where(ok, 0.0, NEG).astype(F32), (1, 1, HPG))


def _nsa_prompt(p, kc, vct, nb, s):
    tq = TQ
    tks = min(TKS, s)
    assert s % tks == 0 and tq == TKV and tq & (tq - 1) == 0 and tks % TKV == 0 and s >= WINDOW + tq
    nqb = s // tq
    r = nb * s
    ncmp = s // CMP_BLOCK
    gw = HPG * HEAD_DIM
    vst = p["vst"].reshape(r // TKV, N_KV, HEAD_DIM, TKV)
    vwt = p["vwt"].reshape(r // TKV, N_KV, HEAD_DIM, TKV)
    nvar = WINDOW // tq
    return pl.pallas_call(
        functools.partial(_nsa_prompt_kernel, tks=tks),
        grid=(nb, nqb),
        in_specs=[pl.BlockSpec((N_HEADS, tq, HEAD_DIM), lambda b, i: (0, b * nqb + i, 0)),
                  pl.BlockSpec((None, N_KV, ncmp, HEAD_DIM), lambda b, i: (b, 0, 0, 0)),
                  pl.BlockSpec((None, N_KV, HEAD_DIM, ncmp), lambda b, i: (b, 0, 0, 0)),
                  pl.BlockSpec((N_KV, s, HEAD_DIM), lambda b, i: (0, b, 0)),
                  pl.BlockSpec((s // TKV, N_KV, HEAD_DIM, TKV), lambda b, i: (b, 0, 0, 0)),
                  pl.BlockSpec((N_KV, s, HEAD_DIM), lambda b, i: (0, b, 0)),
                  pl.BlockSpec((s // TKV, N_KV, HEAD_DIM, TKV), lambda b, i: (b, 0, 0, 0)),
                  pl.BlockSpec((N_KV, 16, tq), lambda b, i: (0, 0, b * nqb + i)),
                  pl.BlockSpec((None, WINDOW + tq, HPG * tq), lambda b, i: (jnp.minimum(i, nvar), 0, 0))],
        out_specs=pl.BlockSpec((tq, NQ), lambda b, i: (b * nqb + i, 0)),
        out_shape=jax.ShapeDtypeStruct((r, NQ), BF16),
        scratch_shapes=[pltpu.VMEM((N_KV, s // SEL_BLOCK, 8, HPG * tq), F32)],
        compiler_params=_cparams("parallel", "arbitrary"),
        name="nsa_prompt",
    )(p["nq"], kc, vct, p["ks"], vst, p["kw"], vwt, p["ngt"], _window_bias(tq))


def _prompt_cmp_operands(kcvc, nb, s):
    ncmp = s // CMP_BLOCK
    a = kcvc.reshape(nb, ncmp // 2, 2, 2, N_KV, HEAD_DIM)
    a = jnp.transpose(a, (0, 3, 4, 2, 1, 5)).reshape(nb, 2, N_KV, ncmp, HEAD_DIM)
    kc = a[:, 0].astype(BF16)
    vct = jnp.swapaxes(a[:, 1], -1, -2).astype(BF16)
    return kc, vct


def _by_group(x0, x1):
    return jnp.where(lax.broadcasted_iota(jnp.int32, x0.shape, 0) < HPG, x0, x1)


def _nsa_dec_cmp_kernel(pt_ref, *refs, pos):
    del pt_ref
    pages = refs[:PAGES_PER_STEP]
    wt_ref, seg_ref, q_ref, oc_ref, sel_ref, kcvc = refs[PAGES_PER_STEP:]
    pc = pl.program_id(1)

    @pl.when(pc == 0)
    def _():
        kcvc[...] = jnp.zeros_like(kcvc)

    x = jnp.concatenate([pg[...] for pg in pages], axis=1)
    kcvc[...] += _dot((x * wt_ref[...]).astype(BF16), seg_ref[...])

    @pl.when(pc == pl.num_programs(1) - 1)
    def _():
        ncmp = kcvc.shape[1]
        half = ncmp // 2
        qb = q_ref[...].astype(BF16)
        kv = kcvc[...].astype(BF16)
        kt = [kv[g * HEAD_DIM:(g + 1) * HEAD_DIM] for g in range(N_KV)]
        vt = [kv[NKV + g * HEAD_DIM:NKV + (g + 1) * HEAD_DIM] for g in range(N_KV)]
        s = _by_group(_dot(qb, kt[0]), _dot(qb, kt[1]))
        cc = lax.broadcasted_iota(jnp.int32, (1, ncmp), 1)
        jc = jnp.where(cc < half, 2 * cc, 2 * (cc - half) + 1)
        valid = ((jc + 1) * CMP_BLOCK - 1) <= pos
        sm = jnp.where(valid, s, NEG)
        e = jnp.exp2(sm - jnp.max(sm, axis=1, keepdims=True))
        p = jnp.where(valid, e / jnp.sum(e, axis=1, keepdims=True), 0.0)
        pb = p.astype(BF16)
        oc_ref[...] = _by_group(_dot_nt(pb, vt[0]), _dot_nt(pb, vt[1]))
        pp = p[:, 0:half] + p[:, half:ncmp]
        jl = lax.broadcasted_iota(jnp.int32, (1, half), 1)
        ii = lax.broadcasted_iota(jnp.int32, (half, half), 0)
        jj = lax.broadcasted_iota(jnp.int32, (half, half), 1)
        k_past = min(TOP_K, half + 1) - 1
        for g in range(N_KV):
            imp = jnp.sum(pp[g * HPG:(g + 1) * HPG], axis=0, keepdims=True)
            forced = (jl == pos // SEL_BLOCK) | (jl == 0)
            key = jnp.where(forced, IMP_FORCED, jnp.where(jl * SEL_BLOCK > pos, IMP_FUTURE, imp))
            kcol = jnp.sum(jnp.where(ii == jj, key, 0.0), axis=1, keepdims=True)
            ahead = (kcol > key) | ((kcol == key) & (ii < jj))
            rank = jnp.sum(jnp.where(ahead, 1.0, 0.0), axis=0, keepdims=True)
            sel = jnp.where(rank < k_past, 1.0, 0.0)
            sel_ref[g * HPG:(g + 1) * HPG, :] = jnp.broadcast_to(sel, (HPG, half))


def _page_specs():
    def spec(kk):
        return pl.BlockSpec((None, ROW_W, PAGE_SIZE), lambda b, pc, pt: (pt[b, pc * PAGES_PER_STEP + kk], 0, 0))
    return [spec(kk) for kk in range(PAGES_PER_STEP)]


def _nsa_dec_cmp(cache_t, page_table, q, wt, seg):
    nb, n_pages = page_table.shape
    assert n_pages % PAGES_PER_STEP == 0
    npc = n_pages // PAGES_PER_STEP
    past = n_pages * PAGE_SIZE
    ncmp = past // CMP_BLOCK
    nselp = ncmp // 2
    keys = PAGES_PER_STEP * PAGE_SIZE
    perb = lambda b, pc, pt: (b, 0, 0)
    grid_spec = pltpu.PrefetchScalarGridSpec(
        num_scalar_prefetch=1,
        grid=(nb, npc),
        in_specs=_page_specs() + [
            pl.BlockSpec((ROW_W, keys), lambda b, pc, pt: (0, 0)),
            pl.BlockSpec((None, keys, ncmp), lambda b, pc, pt: (pc, 0, 0)),
            pl.BlockSpec((None, N_HEADS, HEAD_DIM), perb)],
        out_specs=[pl.BlockSpec((None, N_HEADS, HEAD_DIM), perb),
                   pl.BlockSpec((None, N_HEADS, nselp), perb)],
        scratch_shapes=[pltpu.VMEM((ROW_W, ncmp), F32)],
    )
    return pl.pallas_call(
        functools.partial(_nsa_dec_cmp_kernel, pos=past),
        grid_spec=grid_spec,
        out_shape=[jax.ShapeDtypeStruct((nb, N_HEADS, HEAD_DIM), F32),
                   jax.ShapeDtypeStruct((nb, N_HEADS, nselp), F32)],
        compiler_params=_cparams("parallel", "arbitrary"),
        name="nsa_decode_cmp",
    )(page_table, *([cache_t] * PAGES_PER_STEP), wt, seg, q)


def _nsa_dec_sel_kernel(pt_ref, *refs, pos, wb):
    del pt_ref
    pages = refs[:PAGES_PER_STEP]
    (q_ref, sel_ref, exp_ref, oc_ref, ng_ref, nslc_ref, nwin_ref, nwcol_ref, win_ref,
     o_ref, wout_ref, m_s, l_s, acc_s) = refs[PAGES_PER_STEP:]
    pc = pl.program_id(1)
    q = q_ref[...]
    qb = q.astype(BF16)
    ksl = [slice(g * HEAD_DIM, (g + 1) * HEAD_DIM) for g in range(N_KV)]
    vsl = [slice(NKV + g * HEAD_DIM, NKV + (g + 1) * HEAD_DIM) for g in range(N_KV)]

    @pl.when(pc == 0)
    def _():
        m_s[...] = jnp.full_like(m_s, NEG)
        l_s[...] = jnp.zeros_like(l_s)
        acc_s[...] = jnp.zeros_like(acc_s)

    x = jnp.concatenate([pg[...] for pg in pages], axis=1).astype(BF16)
    keep = _dot(sel_ref[...].astype(BF16), exp_ref[...])
    s = jnp.where(keep > 0.5, _by_group(_dot(qb, x[ksl[0]]), _dot(qb, x[ksl[1]])), NEG)
    m = m_s[:, 0:1]
    mn = jnp.maximum(m, jnp.max(s, axis=1, keepdims=True))
    alpha = jnp.exp2(m - mn)
    p = jnp.exp2(s - mn)
    l_s[...] = alpha * l_s[...] + jnp.sum(p, axis=1, keepdims=True)
    pb = p.astype(BF16)
    acc_s[...] = alpha * acc_s[...] + _by_group(_dot_nt(pb, x[vsl[0]]), _dot_nt(pb, x[vsl[1]]))
    m_s[...] = jnp.broadcast_to(mn, m_s.shape)

    @pl.when(pc == pl.num_programs(1) - 1)
    def _():
        def new_row(row):
            sn = _by_group(jnp.sum(q * row[:, ksl[0]], axis=1, keepdims=True),
                           jnp.sum(q * row[:, ksl[1]], axis=1, keepdims=True))
            return sn, _by_group(jnp.broadcast_to(row[:, vsl[0]], q.shape), jnp.broadcast_to(row[:, vsl[1]], q.shape))

        sn, vn = new_row(nslc_ref[...])
        m1 = m_s[:, 0:1]
        m2 = jnp.maximum(m1, sn)
        a1, pn = jnp.exp2(m1 - m2), jnp.exp2(sn - m2)
        o_sel = (a1 * acc_s[...] + pn * vn) / (a1 * l_s[:, 0:1] + pn)
        win = win_ref[...]
        wbf = win.astype(BF16)
        sw = _by_group(_dot(qb, wbf[ksl[0]]), _dot(qb, wbf[ksl[1]]))
        dpos = pos - (pos - wb + lax.broadcasted_iota(jnp.int32, (1, wb), 1))
        okw = (dpos >= 0) & (dpos < WINDOW) & (pos - dpos >= 0)
        sw = jnp.where(okw, sw, NEG)
        swn, vwn = new_row(nwin_ref[...])
        mw = jnp.maximum(jnp.max(sw, axis=1, keepdims=True), swn)
        pw = jnp.exp2(sw - mw)
        pwn = jnp.exp2(swn - mw)
        pwb = pw.astype(BF16)
        o_win = ((_by_group(_dot_nt(pwb, wbf[vsl[0]]), _dot_nt(pwb, wbf[vsl[1]])) + pwn * vwn)
                 / (jnp.sum(pw, axis=1, keepdims=True) + pwn))
        gates = jax.nn.sigmoid(ng_ref[...])
        o_ref[...] = gates[:, 0:1] * oc_ref[...] + gates[:, 1:2] * o_sel + gates[:, 2:3] * o_win
        lid = lax.broadcasted_iota(jnp.int32, (1, wb), 1)
        wout_ref[...] = jnp.where(lid == wb - 1, nwcol_ref[...], pltpu.roll(win, wb - 1, 1))


def _nsa_dec_sel(cache_t, page_table, q, sel, expand, oc, ng, nslc, nwin, cache_win_t):
    nb, n_pages = page_table.shape
    npc = n_pages // PAGES_PER_STEP
    past = n_pages * PAGE_SIZE
    wb = cache_win_t.shape[-1]
    assert wb == WINDOW and past >= WINDOW
    nselp = sel.shape[-1]
    keys = PAGES_PER_STEP * PAGE_SIZE
    perb = lambda b, pc, pt: (b, 0, 0)
    grid_spec = pltpu.PrefetchScalarGridSpec(
        num_scalar_prefetch=1,
        grid=(nb, npc),
        in_specs=_page_specs() + [
            pl.BlockSpec((None, N_HEADS, HEAD_DIM), perb),
            pl.BlockSpec((None, N_HEADS, nselp), perb),
            pl.BlockSpec((None, nselp, keys), lambda b, pc, pt: (pc, 0, 0)),
            pl.BlockSpec((None, N_HEADS, HEAD_DIM), perb),
            pl.BlockSpec((None, N_HEADS, LANES), perb),
            pl.BlockSpec((None, 1, ROW_W), perb),
            pl.BlockSpec((None, 1, ROW_W), perb),
            pl.BlockSpec((None, ROW_W, 1), perb),
            pl.BlockSpec((None, ROW_W, wb), perb)],
        out_specs=[pl.BlockSpec((None, N_HEADS, HEAD_DIM), perb),
                   pl.BlockSpec((None, ROW_W, wb), perb)],
        scratch_shapes=[pltpu.VMEM((N_HEADS, LANES), F32), pltpu.VMEM((N_HEADS, LANES), F32),
                        pltpu.VMEM((N_HEADS, HEAD_DIM), F32)],
    )
    return pl.pallas_call(
        functools.partial(_nsa_dec_sel_kernel, pos=past, wb=wb),
        grid_spec=grid_spec,
        out_shape=[jax.ShapeDtypeStruct((nb, N_HEADS, HEAD_DIM), F32),
                   jax.ShapeDtypeStruct((nb, ROW_W, wb), F32)],
        compiler_params=_cparams("parallel", "arbitrary"),
        name="nsa_decode_sel",
    )(page_table, *([cache_t] * PAGES_PER_STEP), q, sel, expand, oc, ng, nslc, nwin,
      nwin.reshape(nb, ROW_W, 1), cache_win_t)


def _merge_kernel(x_ref, ya_ref, yb_ref, gab_ref, ga_ref, wa_ref, wb_ref, wo_ref, g_ref, o_ref):
    d = x_ref.shape[1]
    pa = _dot(ya_ref[...], wa_ref[...])
    pb = _dot(yb_ref[...], wb_ref[...])
    merged = jax.nn.sigmoid(gab_ref[:, 0:d]) * pa + jax.nn.sigmoid(gab_ref[:, d:2 * d]) * pb
    z = _dot(merged.astype(BF16), wo_ref[...])
    o_ref[...] = x_ref[...] + ga_ref[0] * _rms(z, g_ref[...])


def _merge(x, ya, yb, gab, ga1, w_b, wts, g_post, tm, rows_per_mod):
    r, d = x.shape
    wyb = yb.shape[1]
    tpm = rows_per_mod // tm
    mr = ga1.shape[1]
    row = lambda i: (i, 0)
    const2 = lambda i: (0, 0)
    return pl.pallas_call(
        _merge_kernel,
        grid=(r // tm,),
        in_specs=[pl.BlockSpec((tm, d), row), pl.BlockSpec((tm, MV), row), pl.BlockSpec((tm, wyb), row),
                  pl.BlockSpec((tm, 2 * d), row),
                  pl.BlockSpec((1, mr, d), lambda i: (i // tpm, 0, 0)),
                  pl.BlockSpec((MV, d), const2), pl.BlockSpec((wyb, d), const2), pl.BlockSpec((d, d), const2),
                  pl.BlockSpec((1, d), const2)],
        out_specs=pl.BlockSpec((tm, d), row),
        out_shape=jax.ShapeDtypeStruct((r, d), F32),
        compiler_params=_cparams("parallel"),
        name="merge",
    )(x, ya, yb, gab, ga1, wts["w_a"], w_b, wts["w_o"], g_post.reshape(1, d))


def _ffn_kernel(x_ref, sc_ref, sh_ref, ga_ref, g1_ref, g2_ref, wu_ref, wd_ref, o_ref):
    dff = wd_ref.shape[0]
    x = x_ref[...]
    hb = (_rms(x, g1_ref[...]) * (1.0 + sc_ref[0]) + sh_ref[0]).astype(BF16)
    acc = jnp.zeros(x.shape, F32)
    for c in range(dff // FFN_CK):
        lo, hi = c * FFN_CK, (c + 1) * FFN_CK
        gate = _dot(hb, wu_ref[:, lo:hi])
        up = _dot(hb, wu_ref[:, dff + lo:dff + hi])
        acc = acc + _dot((gate * jax.nn.sigmoid(gate) * up).astype(BF16), wd_ref[lo:hi, :])
    o_ref[...] = x + ga_ref[0] * _rms(acc, g2_ref[...])


def _ffn(x, sc, sh, ga2, g_pre, g_post, wts, tm, rows_per_mod):
    r, d = x.shape
    dff = wts["w_down"].shape[0]
    assert dff % FFN_CK == 0
    tpm = rows_per_mod // tm
    mr = sc.shape[1]
    row = lambda i: (i, 0)
    const2 = lambda i: (0, 0)
    mod = pl.BlockSpec((1, mr, d), lambda i: (i // tpm, 0, 0))
    return pl.pallas_call(
        _ffn_kernel,
        grid=(r // tm,),
        in_specs=[pl.BlockSpec((tm, d), row), mod, mod, mod,
                  pl.BlockSpec((1, d), const2), pl.BlockSpec((1, d), const2),
                  pl.BlockSpec((d, 2 * dff), const2), pl.BlockSpec((dff, d), const2)],
        out_specs=pl.BlockSpec((tm, d), row),
        out_shape=jax.ShapeDtypeStruct((r, d), F32),
        compiler_params=_cparams("parallel"),
        name="ffn",
    )(x, sc, sh, ga2, g_pre.reshape(1, d), g_post.reshape(1, d), wts["w_up"], wts["w_down"])


def _prep_weights(w_in, b_in, w_cmp_k, w_cmp_v, w_proj_a, w_proj_b, w_out, w_up, w_down):
    o_mi, o_nq, o_ng, o_ga = 4 * 256 + 2 * 256, 1544, 2824, 2848
    o_nkv = o_nq + NQ
    main_cols = [slice(0, o_mi), slice(o_nq, o_ng), slice(o_ga, w_in.shape[1])]
    w_main = jnp.concatenate([w_in[:, s] for s in main_cols], axis=1).astype(BF16)
    b_main = jnp.concatenate([b_in[s] for s in main_cols]).reshape(1, -1)
    z = lambda n: jnp.zeros((w_in.shape[0], n), w_in.dtype)
    ng0, ng1 = slice(o_ng, o_ng + 12), slice(o_ng + 12, o_ng + 24)
    tg_cols = [w_in[:, o_mi:o_mi + 8], z(8), w_in[:, ng0], z(4), w_in[:, ng1], z(4)]
    tg_b = [b_in[o_mi:o_mi + 8], jnp.zeros(8), b_in[ng0], jnp.zeros(4), b_in[ng1], jnp.zeros(4)]
    vs_, vw_ = slice(o_nkv + 3 * NKV, o_nkv + 4 * NKV), slice(o_nkv + 5 * NKV, o_nkv + 6 * NKV)
    nkv_part = lambda j: slice(o_nkv + j * NKV, o_nkv + (j + 1) * NKV)
    tb_parts = [slice(256, 512), vs_, vw_, nkv_part(0), nkv_part(1), nkv_part(2), nkv_part(4)]
    tb_cols = [w_in[:, s] for s in tb_parts]
    tb_b = [b_in[s] for s in tb_parts]
    w_cmp = jnp.concatenate([w_cmp_k, w_cmp_k, w_cmp_v, w_cmp_v], axis=1)
    return {
        "w_main": w_main, "b_main": b_main,
        "w_tg": jnp.concatenate(tg_cols, axis=1).T.astype(BF16),
        "b_tg": jnp.concatenate(tg_b).reshape(-1, 1).astype(F32),
        "w_tb": jnp.concatenate(tb_cols, axis=1).T.astype(BF16),
        "b_tb": jnp.concatenate(tb_b).reshape(-1, 1),
        "w_cmp": w_cmp,
        "w_a": w_proj_a.astype(BF16), "w_b": w_proj_b.astype(BF16), "w_o": w_out.astype(BF16),
        "w_up": w_up.astype(BF16), "w_down": w_down.astype(BF16),
    }


def _rope_tables(pos):
    half = ROPE_DIM // 2
    n = pos.shape[0]
    inv = ROPE_THETA ** (-jnp.arange(half, dtype=F32) * 2.0 / ROPE_DIM)
    ang = pos.astype(F32)[:, None] * inv[None, :]
    cos, sin = jnp.cos(ang), jnp.sin(ang)
    rest = HEAD_DIM - ROPE_DIM
    zh, zr = jnp.zeros((n, half), F32), jnp.zeros((n, rest), F32)
    a = jnp.concatenate([cos, cos, jnp.ones((n, rest), F32)], axis=1)
    p = jnp.concatenate([zh, sin, zr], axis=1)
    m = jnp.concatenate([-sin, zh, zr], axis=1)
    return tuple(jnp.tile(t, (1, LANES // HEAD_DIM)) for t in (a, p, m)) + (cos.T, sin.T)


def _rows_on_lanes(cache):
    n, rows = cache.shape[0], cache.shape[1]
    return jnp.transpose(cache, (0, 2, 3, 4, 1)).reshape(n, ROW_W, rows)


def _cmp_step_operands(w_cmp, n_pages):
    npc = n_pages // PAGES_PER_STEP
    keys = PAGES_PER_STEP * PAGE_SIZE
    ncmp = n_pages * PAGE_SIZE // CMP_BLOCK
    wt = jnp.tile(w_cmp.T, (1, keys // CMP_BLOCK))
    blk = (jnp.arange(npc)[:, None] * keys + jnp.arange(keys)[None, :]) // CMP_BLOCK
    col = (blk % 2) * (ncmp // 2) + blk // 2
    seg = (col[:, :, None] == jnp.arange(ncmp)[None, None, :]).astype(BF16)
    return wt, seg


def _block_expand(n_pages):
    npc = n_pages // PAGES_PER_STEP
    keys = PAGES_PER_STEP * PAGE_SIZE
    nselp = n_pages * PAGE_SIZE // SEL_BLOCK
    blk = (jnp.arange(npc)[:, None] * keys + jnp.arange(keys)[None, :]) // SEL_BLOCK
    return (jnp.arange(nselp)[None, :, None] == blk[:, None, :]).astype(BF16)


def _layer_prompt(x, mod, wts, norms, mlstm_norm_w):
    nb, s, d = x.shape
    r = nb * s
    sh1, sc1, ga1, sh2, sc2, ga2 = [m.reshape(nb, 1, d) for m in jnp.split(mod, 6, axis=-1)]
    g_pre_mix, g_post_mix, g_pre_ffn, g_post_ffn = norms
    tm = TM_PROJ
    assert s % tm == 0
    tabs = _rope_tables(jnp.arange(s, dtype=jnp.int32))
    x2 = x.reshape(r, d)
    p = _in_proj(x2, g_pre_mix, sc1, sh1, wts, tabs, tm, s, True)
    ya, c_new, n_new, m_new = _mlstm_prompt(p, mlstm_norm_w, nb, s)
    kc, vct = _prompt_cmp_operands(p["kcvc"], nb, s)
    yb = _nsa_prompt(p, kc, vct, nb, s)
    x1 = _merge(x2, ya, yb, p["gab"], ga1, wts["w_b"], wts, g_post_mix, tm, s)
    y = _ffn(x1, sc2, sh2, ga2, g_pre_ffn, g_post_ffn, wts, tm, s)
    rows = lambda a: jnp.transpose(a.reshape(nb, 2, N_KV, HEAD_DIM, -1), (0, 4, 1, 2, 3))
    win = rows(p["win"][:, :, s - min(WINDOW, s):])
    return y.reshape(nb, s, d), (rows(p["cmp"]), rows(p["slc"]), win, c_new, n_new, m_new)


def _layer_sample(x, mod, wts, norms, mlstm_norm_w, cache_cmp, cache_slc, cache_win, page_table, c0, n0, m0):
    nb, ts, d = x.shape
    assert ts == 1
    sh1, sc1, ga1, sh2, sc2, ga2 = [m.reshape(1, nb, d) for m in jnp.split(mod, 6, axis=-1)]
    g_pre_mix, g_post_mix, g_pre_ffn, g_post_ffn = norms
    n_pages = page_table.shape[1]
    past = n_pages * PAGE_SIZE
    tm = nb
    tabs = _rope_tables(jnp.full((nb,), past, jnp.int32))
    x2 = x.reshape(nb, d)
    p = _in_proj(x2, g_pre_mix, sc1, sh1, wts, tabs, tm, nb, False)
    ya, c_new, n_new, m_new = _mlstm_step(p, p["gt"].T, c0, n0, m0, mlstm_norm_w)
    qs = jnp.transpose(p["nq"], (1, 0, 2)).astype(F32)
    ng = jnp.transpose(p["ngt"][:, :HPG * 3], (2, 0, 1)).reshape(nb, N_HEADS, 3)
    ng = jnp.pad(ng, ((0, 0), (0, 0), (0, LANES - 3)))
    wt, seg = _cmp_step_operands(wts["w_cmp"], n_pages)
    oc, sel = _nsa_dec_cmp(_rows_on_lanes(cache_cmp), page_table, qs, wt, seg)
    o, win_new = _nsa_dec_sel(_rows_on_lanes(cache_slc), page_table, qs, sel, _block_expand(n_pages), oc, ng,
                              p["slc"].reshape(nb, 1, ROW_W), p["win"].reshape(nb, 1, ROW_W),
                              _rows_on_lanes(cache_win))
    yb = o.reshape(nb, NQ).astype(BF16)
    x1 = _merge(x2, ya.astype(BF16), yb, p["gab"], ga1, wts["w_b"], wts, g_post_mix, tm, nb)
    y = _ffn(x1, sc2, sh2, ga2, g_pre_ffn, g_post_ffn, wts, tm, nb)
    rows = lambda a: a.reshape(nb, -1, 2, N_KV, HEAD_DIM)
    win_rows = jnp.transpose(win_new.reshape(nb, 2, N_KV, HEAD_DIM, -1), (0, 4, 1, 2, 3))
    return y.reshape(nb, 1, d), (rows(p["cmp"]), rows(p["slc"]), win_rows, c_new, n_new, m_new)


def kernel(x_prompt, x_sample, c_prompt, c_sample, cache_cmp_kv, cache_slc_kv, cache_win_kv, state_mlstm_C, state_mlstm_n, state_mlstm_m, page_table, g_pre_mix, g_post_mix, g_pre_ffn, g_post_ffn, w_ada, b_ada, w_in, b_in, mlstm_norm_w, w_cmp_k, w_cmp_v, w_proj_a, w_proj_b, w_out, w_up, w_down):
    depth = w_in.shape[0]
    nbp, nbs = x_prompt.shape[0], x_sample.shape[0]
    xp, xs = x_prompt, x_sample
    c_all = jnp.concatenate([c_prompt, c_sample], axis=0)
    pad = (-c_all.shape[0]) % 8
    c_all = jnp.pad(c_all, ((0, pad), (0, 0)))
    st_p = [[] for _ in range(6)]
    st_s = [[] for _ in range(6)]
    for l in range(depth):
        wts = _prep_weights(w_in[l], b_in[l], w_cmp_k[l], w_cmp_v[l], w_proj_a[l], w_proj_b[l], w_out[l],
                            w_up[l], w_down[l])
        norms = (g_pre_mix[l], g_post_mix[l], g_pre_ffn[l], g_post_ffn[l])
        mod = _adaln(c_all, w_ada[l], b_ada[l])
        xp, sp = _layer_prompt(xp, mod[:nbp], wts, norms, mlstm_norm_w[l])
        xs, ss = _layer_sample(xs, mod[nbp:nbp + nbs], wts, norms, mlstm_norm_w[l], cache_cmp_kv[l],
                               cache_slc_kv[l], cache_win_kv[l], page_table, state_mlstm_C[l],
                               state_mlstm_n[l], state_mlstm_m[l])
        for j in range(6):
            st_p[j].append(sp[j])
            st_s[j].append(ss[j])
    outs_p = [jnp.stack(a, axis=0) for a in st_p]
    outs_s = [jnp.stack(a, axis=0) for a in st_s]
    return (xp, xs, *outs_p, *outs_s)
```

```python
import functools

import jax
import jax.numpy as jnp
from jax import lax
from jax.experimental import pallas as pl
from jax.experimental.pallas import tpu as pltpu

F32 = jnp.float32
BF16 = jnp.bfloat16

M_HEADS, M_DQK, M_DV = 4, 64, 128
GATE_CAP = 15.0
N_HEADS, N_KV, HEAD_DIM = 8, 2, 64
HPG = N_HEADS // N_KV
CMP_BLOCK, SEL_BLOCK, TOP_K, WINDOW = 32, 64, 16, 512
ROPE_DIM = HEAD_DIM // 4
ROPE_THETA = 500000.0
PAGE_SIZE = 128
EPS = 1e-6
NEG = -1e30
LOG2E = 1.4426950408889634
MQK, MV = M_HEADS * M_DQK, M_HEADS * M_DV
NQ, NKV = N_HEADS * HEAD_DIM, N_KV * HEAD_DIM
ROW_W = 2 * NKV

LANES = 128
VMEM_LIMIT = 48 * 1024 * 1024

TM_PROJ = 512
T_CHUNK = 128
TQ = 128
TKS = 512
TKV = 128
ONES_ROWS = 16
PAGES_PER_STEP = 16
FFN_CK = 256

IMP_FORCED, IMP_FUTURE, IMP_TAKEN = 1e30, -1e30, -2e30


def _dot(a, b):
    return jnp.dot(a, b, preferred_element_type=F32)


def _dot_nt(a, b):
    return lax.dot_general(a, b, (((1,), (1,)), ((), ())), preferred_element_type=F32)


def _rms(x, g):
    return x * lax.rsqrt(jnp.mean(x * x, axis=-1, keepdims=True) + EPS) * g


def _log_sigmoid(x):
    return jnp.minimum(x, 0.0) - jnp.log1p(jnp.exp(-jnp.abs(x)))


def _cparams(*sem):
    return pltpu.CompilerParams(dimension_semantics=sem, vmem_limit_bytes=VMEM_LIMIT)


def _adaln_kernel(c_ref, w_ref, b_ref, o_ref):
    c = c_ref[...]
    a = (c * jax.nn.sigmoid(c)).astype(BF16)
    o_ref[...] = _dot(a, w_ref[...].astype(BF16)) + b_ref[...]


def _adaln(c, w, b):
    r, d = c.shape
    n = w.shape[1]
    tn = 1536
    return pl.pallas_call(
        _adaln_kernel,
        grid=(n // tn,),
        in_specs=[pl.BlockSpec((r, d), lambda j: (0, 0)),
                  pl.BlockSpec((d, tn), lambda j: (0, j)),
                  pl.BlockSpec((1, tn), lambda j: (0, j))],
        out_specs=pl.BlockSpec((r, tn), lambda j: (0, j)),
        out_shape=jax.ShapeDtypeStruct((r, n), F32),
        compiler_params=_cparams("arbitrary"),
        name="adaln",
    )(c, w, b.reshape(1, n))


_C_MQ, _C_MK, _C_MV, _C_MO, _C_NQ, _C_NKV, _C_GAB, _C_END = 0, 256, 512, 1024, 1536, 2048, 2816, 4864


def _proj_kernel(x_ref, g_ref, sc_ref, sh_ref, wm_ref, bm_ref, wtg_ref, btg_ref, wtb_ref, btb_ref,
                 ra_ref, rp_ref, rm_ref, ct_ref, st_ref, wc_ref,
                 mq_ref, mk_ref, mv_ref, mo_ref, nq_ref, cmp_ref, slc_ref, win_ref, ks_ref, kw_ref,
                 gab_ref, kcvc_ref, gt_ref, ngt_ref, mkt_ref, vst_ref, vwt_ref, *, tks, tkw, rows_t):
    tm = x_ref.shape[0]
    x = x_ref[...]
    h = _rms(x, g_ref[...]) * (1.0 + sc_ref[0]) + sh_ref[0]
    hb = h.astype(BF16)

    def mm(lo, hi):
        return _dot(hb, wm_ref[:, lo:hi]) + bm_ref[:, lo:hi]

    ra, rp, rm = ra_ref[...], rp_ref[...], rm_ref[...]

    def rope(xc):
        return xc * ra + pltpu.roll(xc, 8, 1) * rp + pltpu.roll(xc, LANES - 8, 1) * rm

    scale_m = M_DQK ** -0.5
    scale_n = HEAD_DIM ** -0.5 * LOG2E
    mq_ref[...] = (mm(_C_MQ, _C_MK) * scale_m).astype(BF16)
    mk_ref[...] = mm(_C_MK, _C_MV).astype(BF16)
    mv_ref[...] = mm(_C_MV, _C_MO).astype(BF16)
    mo_ref[...] = mm(_C_MO, _C_NQ)

    nq = mm(_C_NQ, _C_NKV)
    for c in range(NQ // LANES):
        r = (rope(nq[:, c * LANES:(c + 1) * LANES]) * scale_n).astype(BF16)
        nq_ref[2 * c] = r[:, :HEAD_DIM]
        nq_ref[2 * c + 1] = r[:, HEAD_DIM:]

    nkv = mm(_C_NKV, _C_GAB)
    kc = rope(nkv[:, 0:128])
    vc = nkv[:, 128:256]
    ksr = rope(nkv[:, 256:384])
    kwr = rope(nkv[:, 512:640])
    if not rows_t:
        cmp_ref[:, 0:NKV] = kc
        cmp_ref[:, NKV:ROW_W] = vc
        slc_ref[:, 0:NKV] = ksr
        slc_ref[:, NKV:ROW_W] = nkv[:, 384:512]
        win_ref[:, 0:NKV] = kwr
        win_ref[:, NKV:ROW_W] = nkv[:, 640:768]
    for g in range(N_KV):
        ks_ref[g] = ksr[:, g * HEAD_DIM:(g + 1) * HEAD_DIM].astype(BF16)
        kw_ref[g] = kwr[:, g * HEAD_DIM:(g + 1) * HEAD_DIM].astype(BF16)

    wc = wc_ref[...]
    nb = tm // CMP_BLOCK
    kcvc_ref[:, 0:NKV] = jnp.sum(kc.reshape(nb, CMP_BLOCK, NKV) * wc[None, :, 0:NKV], axis=1)
    kcvc_ref[:, NKV:ROW_W] = jnp.sum(vc.reshape(nb, CMP_BLOCK, NKV) * wc[None, :, NKV:ROW_W], axis=1)

    gab_ref[...] = mm(_C_GAB, _C_END)

    tg = _dot_nt(wtg_ref[...], hb) + btg_ref[...]
    gt_ref[...] = tg[0:8]
    ngt_ref[0] = tg[16:32]
    ngt_ref[1] = tg[32:48]
    tb = _dot_nt(wtb_ref[...], hb) + btb_ref[...]
    mkt_ref[...] = tb[0:MQK].astype(BF16)
    for j in range(tm // tks):
        vst_ref[j] = tb[MQK:MQK + NKV, j * tks:(j + 1) * tks].astype(BF16)
    for j in range(tm // tkw):
        vwt_ref[j] = tb[MQK + NKV:MQK + 2 * NKV, j * tkw:(j + 1) * tkw].astype(BF16)

    if rows_t:
        ct, st = ct_ref[...], st_ref[...]
        half = ROPE_DIM // 2

        def rope_t(kt):
            parts = []
            for g in range(N_KV):
                blk = kt[g * HEAD_DIM:(g + 1) * HEAD_DIM]
                x1, x2 = blk[0:half], blk[half:ROPE_DIM]
                parts += [x1 * ct - x2 * st, x2 * ct + x1 * st, blk[ROPE_DIM:HEAD_DIM]]
            return jnp.concatenate(parts, axis=0)

        o = MQK + 2 * NKV
        cmp_ref[0:NKV] = rope_t(tb[o:o + NKV])
        cmp_ref[NKV:ROW_W] = tb[o + NKV:o + 2 * NKV]
        slc_ref[0:NKV] = rope_t(tb[o + 2 * NKV:o + 3 * NKV])
        slc_ref[NKV:ROW_W] = tb[MQK:MQK + NKV]
        win_ref[0:NKV] = rope_t(tb[o + 3 * NKV:o + 4 * NKV])
        win_ref[NKV:ROW_W] = tb[MQK + NKV:MQK + 2 * NKV]


def _in_proj(x, g_pre, sc, sh, wts, rope_tabs, tm, rows_per_mod, rows_t):
    r, d = x.shape
    nt = r // tm
    tks = tkw = min(TKV, tm)
    tpm = rows_per_mod // tm
    tpr = rope_tabs[0].shape[0] // tm
    mr = sc.shape[1]
    row = lambda i: (i, 0)
    const2 = lambda i: (0, 0)
    if rows_t:
        rows_shape = (r // rows_per_mod, ROW_W, rows_per_mod)
        rows_spec = pl.BlockSpec((None, ROW_W, tm), lambda i: (i // tpm, 0, i % tpm))
    else:
        rows_shape = (r, ROW_W)
        rows_spec = pl.BlockSpec((tm, ROW_W), row)
    ntb = wts["w_tb"].shape[0]
    in_specs = [
        pl.BlockSpec((tm, d), row),
        pl.BlockSpec((1, d), const2),
        pl.BlockSpec((1, mr, d), lambda i: (i // tpm, 0, 0)),
        pl.BlockSpec((1, mr, d), lambda i: (i // tpm, 0, 0)),
        pl.BlockSpec((d, _C_END), const2),
        pl.BlockSpec((1, _C_END), const2),
        pl.BlockSpec((48, d), const2),
        pl.BlockSpec((48, 1), const2),
        pl.BlockSpec((ntb, d), const2),
        pl.BlockSpec((ntb, 1), const2),
        pl.BlockSpec((tm, LANES), lambda i: (i % tpr, 0)),
        pl.BlockSpec((tm, LANES), lambda i: (i % tpr, 0)),
        pl.BlockSpec((tm, LANES), lambda i: (i % tpr, 0)),
        pl.BlockSpec((ROPE_DIM // 2, tm), lambda i: (0, i % tpr)),
        pl.BlockSpec((ROPE_DIM // 2, tm), lambda i: (0, i % tpr)),
        pl.BlockSpec((CMP_BLOCK, ROW_W), const2),
    ]
    outs = [
        ("mq", (r, MQK), BF16, pl.BlockSpec((tm, MQK), row)),
        ("mk", (r, MQK), BF16, pl.BlockSpec((tm, MQK), row)),
        ("mv", (r, MV), BF16, pl.BlockSpec((tm, MV), row)),
        ("mo", (r, MV), F32, pl.BlockSpec((tm, MV), row)),
        ("nq", (N_HEADS, r, HEAD_DIM), BF16, pl.BlockSpec((N_HEADS, tm, HEAD_DIM), lambda i: (0, i, 0))),
        ("cmp", rows_shape, F32, rows_spec),
        ("slc", rows_shape, F32, rows_spec),
        ("win", rows_shape, F32, rows_spec),
        ("ks", (N_KV, r, HEAD_DIM), BF16, pl.BlockSpec((N_KV, tm, HEAD_DIM), lambda i: (0, i, 0))),
        ("kw", (N_KV, r, HEAD_DIM), BF16, pl.BlockSpec((N_KV, tm, HEAD_DIM), lambda i: (0, i, 0))),
        ("gab", (r, 2 * d), F32, pl.BlockSpec((tm, 2 * d), row)),
        ("kcvc", (r // CMP_BLOCK, ROW_W), F32, pl.BlockSpec((tm // CMP_BLOCK, ROW_W), row)),
        ("gt", (8, r), F32, pl.BlockSpec((8, tm), lambda i: (0, i))),
        ("ngt", (N_KV, 16, r), F32, pl.BlockSpec((N_KV, 16, tm), lambda i: (0, 0, i))),
        ("mkt", (MQK, r), BF16, pl.BlockSpec((MQK, tm), lambda i: (0, i))),
        ("vst", (r // tks, NKV, tks), BF16, pl.BlockSpec((tm // tks, NKV, tks), lambda i: (i, 0, 0))),
        ("vwt", (r // tkw, NKV, tkw), BF16, pl.BlockSpec((tm // tkw, NKV, tkw), lambda i: (i, 0, 0))),
    ]
    res = pl.pallas_call(
        functools.partial(_proj_kernel, tks=tks, tkw=tkw, rows_t=rows_t),
        grid=(nt,),
        in_specs=in_specs,
        out_specs=[o[3] for o in outs],
        out_shape=[jax.ShapeDtypeStruct(o[1], o[2]) for o in outs],
        compiler_params=_cparams("parallel"),
        name="in_proj",
    )(x, g_pre.reshape(1, d), sc, sh, wts["w_main"], wts["b_main"], wts["w_tg"], wts["b_tg"],
      wts["w_tb"], wts["b_tb"], *rope_tabs, wts["w_cmp"])
    return {o[0]: v for o, v in zip(outs, res)}


def _mlstm_chunk_kernel(q_ref, k_ref, kt_ref, v_ref, mo_ref, gt_ref, nw_ref,
                        ya_ref, c_out_ref, m_out_ref, caug, mstate):
    t = q_ref.shape[0]
    c = pl.program_id(1)

    @pl.when(c == 0)
    def _():
        caug[...] = jnp.zeros_like(caug)
        mstate[...] = jnp.zeros_like(mstate)

    gc = GATE_CAP * jnp.tanh(gt_ref[...] / GATE_CAP)
    lane = lax.broadcasted_iota(jnp.int32, (8, t), 1)
    b = _log_sigmoid(gc)
    k = 1
    while k < t:
        b = b + jnp.where(lane >= k, pltpu.roll(b, k, 1), 0.0)
        k *= 2

    row = lax.broadcasted_iota(jnp.int32, (t, t), 0)
    col = lax.broadcasted_iota(jnp.int32, (t, t), 1)
    causal = col <= row
    eye = col == row
    lane_v = lax.broadcasted_iota(jnp.int32, (t, M_DV), 1)

    def to_col(r):
        return jnp.sum(jnp.where(eye, r, 0.0), axis=1, keepdims=True)

    for h in range(M_HEADS):
        qs, vs_ = slice(h * M_DQK, (h + 1) * M_DQK), slice(h * M_DV, (h + 1) * M_DV)
        irow = gc[h:h + 1, :]
        brow = b[M_HEADS + h:M_HEADS + h + 1, :]
        bcol = to_col(brow)
        mprev = mstate[h:h + 1, 0:1]
        d = jnp.where(causal, bcol - brow + irow, NEG)
        minter = bcol + mprev
        mt = jnp.maximum(minter, jnp.max(d, axis=1, keepdims=True))
        qh, kh, vh = q_ref[:, qs], k_ref[:, qs], v_ref[:, vs_]
        w = _dot_nt(qh, kh) * jnp.exp(d - mt)
        ainter = jnp.exp(minter - mt)
        ca = caug[h]
        qc = _dot(qh, ca.astype(BF16))
        num = _dot(w.astype(BF16), vh) + ainter * qc[:, 0:M_DV]
        den = jnp.sum(w, axis=1, keepdims=True) + ainter * qc[:, M_DV:M_DV + 1]
        hh = num / jnp.maximum(jnp.abs(den), jnp.exp(-mt))
        y = _rms(hh, nw_ref[:, vs_])
        ya_ref[:, vs_] = (jax.nn.sigmoid(mo_ref[:, vs_]) * y).astype(BF16)
        b_last = brow[:, t - 1:t]
        grow = b_last - brow + irow
        mnew = jnp.maximum(b_last + mprev, jnp.max(grow, axis=1, keepdims=True))
        ain = to_col(jnp.exp(grow - mnew))
        ast = jnp.exp(b_last + mprev - mnew)
        vsa = jnp.concatenate([ain * vh.astype(F32), jnp.where(lane_v == 0, ain, 0.0)], axis=1)
        caug[h] = ast * ca + _dot(kt_ref[qs, :], vsa.astype(BF16))
        mstate[h:h + 1, :] = jnp.broadcast_to(mnew, (1, LANES))

    @pl.when(c == pl.num_programs(1) - 1)
    def _():
        c_out_ref[0] = caug[...]
        m_out_ref[0] = mstate[...]


def _mlstm_prompt(p, norm_w, nb, s):
    t = T_CHUNK
    assert s % t == 0
    nc = s // t
    r = nb * s
    rowc = lambda b, c: (b * nc + c, 0)
    ya, caug, mst = pl.pallas_call(
        _mlstm_chunk_kernel,
        grid=(nb, nc),
        in_specs=[pl.BlockSpec((t, MQK), rowc),
                  pl.BlockSpec((t, MQK), rowc),
                  pl.BlockSpec((MQK, t), lambda b, c: (0, b * nc + c)),
                  pl.BlockSpec((t, MV), rowc),
                  pl.BlockSpec((t, MV), rowc),
                  pl.BlockSpec((8, t), lambda b, c: (0, b * nc + c)),
                  pl.BlockSpec((1, MV), lambda b, c: (0, 0))],
        out_specs=[pl.BlockSpec((t, MV), rowc),
                   pl.BlockSpec((1, M_HEADS, M_DQK, 2 * M_DV), lambda b, c: (b, 0, 0, 0)),
                   pl.BlockSpec((1, 8, LANES), lambda b, c: (b, 0, 0))],
        out_shape=[jax.ShapeDtypeStruct((r, MV), BF16),
                   jax.ShapeDtypeStruct((nb, M_HEADS, M_DQK, 2 * M_DV), F32),
                   jax.ShapeDtypeStruct((nb, 8, LANES), F32)],
        scratch_shapes=[pltpu.VMEM((M_HEADS, M_DQK, 2 * M_DV), F32), pltpu.VMEM((8, LANES), F32)],
        compiler_params=_cparams("parallel", "arbitrary"),
        name="mlstm_prompt",
    )(p["mq"], p["mk"], p["mkt"], p["mv"], p["mo"], p["gt"], norm_w.reshape(1, MV))
    return ya, caug[..., :M_DV], caug[..., M_DV], mst[:, :M_HEADS, 0]


def _mlstm_step_kernel(q_ref, k_ref, v_ref, mo_ref, g_ref, c_ref, n_ref, m_ref, nw_ref,
                       ya_ref, co_ref, no_ref, mo_out_ref):
    bb = q_ref.shape[0]
    q, k, v = q_ref[...], k_ref[...], v_ref[...]
    gc = GATE_CAP * jnp.tanh(g_ref[...] / GATE_CAP)
    lf = _log_sigmoid(gc)
    m0 = m_ref[...]
    eye = lax.broadcasted_iota(jnp.int32, (M_DQK, M_DQK), 0) == lax.broadcasted_iota(jnp.int32, (M_DQK, M_DQK), 1)

    def to_col(r):
        return jnp.sum(jnp.where(eye, r, 0.0), axis=1, keepdims=True)

    for b in range(bb):
        for h in range(M_HEADS):
            qs, vs_ = slice(h * M_DQK, (h + 1) * M_DQK), slice(h * M_DV, (h + 1) * M_DV)
            qh, kh, vh = q[b:b + 1, qs], k[b:b + 1, qs], v[b:b + 1, vs_]
            c0 = c_ref[b, h]
            n0 = n_ref[b, h:h + 1, :]
            ipre = gc[b:b + 1, h:h + 1]
            minter = lf[b:b + 1, M_HEADS + h:M_HEADS + h + 1] + m0[b:b + 1, h:h + 1]
            mt = jnp.maximum(minter, ipre)
            ain = jnp.exp(ipre - mt)
            ast = jnp.exp(minter - mt)
            w = jnp.sum(qh * kh, axis=1, keepdims=True) * ain
            qc = jnp.sum(to_col(qh) * c0, axis=0, keepdims=True)
            num = w * vh + ast * qc
            den = w + ast * jnp.sum(qh * n0, axis=1, keepdims=True)
            hh = num / jnp.maximum(jnp.abs(den), jnp.exp(-mt))
            y = _rms(hh, nw_ref[:, vs_])
            ya_ref[b:b + 1, vs_] = (jax.nn.sigmoid(mo_ref[b:b + 1, vs_]) * y).astype(ya_ref.dtype)
            co_ref[b, h] = ast * c0 + to_col(kh) * (ain * vh)
            no_ref[b, h:h + 1, :] = ast * n0 + ain * kh
            mo_out_ref[b:b + 1, h:h + 1] = mt


def _mlstm_step(p, gcol, c0, n0, m0, norm_w):
    nb = c0.shape[0]
    bb = 8
    row = lambda i: (i, 0)
    return pl.pallas_call(
        _mlstm_step_kernel,
        grid=(nb // bb,),
        in_specs=[pl.BlockSpec((bb, MQK), row), pl.BlockSpec((bb, MQK), row), pl.BlockSpec((bb, MV), row),
                  pl.BlockSpec((bb, MV), row), pl.BlockSpec((bb, 8), row),
                  pl.BlockSpec((bb, M_HEADS, M_DQK, M_DV), lambda i: (i, 0, 0, 0)),
                  pl.BlockSpec((bb, M_HEADS, M_DQK), lambda i: (i, 0, 0)),
                  pl.BlockSpec((bb, M_HEADS), row),
                  pl.BlockSpec((1, MV), lambda i: (0, 0))],
        out_specs=[pl.BlockSpec((bb, MV), row),
                   pl.BlockSpec((bb, M_HEADS, M_DQK, M_DV), lambda i: (i, 0, 0, 0)),
                   pl.BlockSpec((bb, M_HEADS, M_DQK), lambda i: (i, 0, 0)),
                   pl.BlockSpec((bb, M_HEADS), row)],
        out_shape=[jax.ShapeDtypeStruct((nb, MV), F32),
                   jax.ShapeDtypeStruct(c0.shape, F32),
                   jax.ShapeDtypeStruct(n0.shape, F32),
                   jax.ShapeDtypeStruct(m0.shape, F32)],
        compiler_params=_cparams("parallel"),
        name="mlstm_step",
    )(p["mq"].astype(F32), p["mk"].astype(F32), p["mv"].astype(F32), p["mo"], gcol, c0, n0, m0,
      norm_w.reshape(1, MV))


def _nsa_prompt_kernel(q_ref, kc_ref, vct_ref, ks_ref, vst_ref, kw_ref, vwt_ref, ng_ref, wb_ref, o_ref,
                       bias_ref, s_ref, *, tks):
    tq = q_ref.shape[1]
    r4 = HPG * tq
    gw = HPG * HEAD_DIM
    tkv = vst_ref.shape[-1]
    ncmp = kc_ref.shape[1]
    nsel = ncmp // 2
    groups = range(N_KV)
    i = pl.program_id(1)
    t0 = i * tq
    qs = [q_ref[g * HPG:(g + 1) * HPG].reshape(r4, HEAD_DIM) for g in groups]
    lane_t = t0 + (lax.broadcasted_iota(jnp.int32, (1, r4), 1) & (tq - 1))

    ones_rows = jnp.where(lax.broadcasted_iota(jnp.int32, (ONES_ROWS, tkv), 0) == 0, 1.0, 0.0).astype(BF16)

    def with_ones(vt):
        return jnp.concatenate([vt, ones_rows], axis=0)

    nwv = (WINDOW + tq) // tkv
    w0 = jnp.maximum(t0 - WINDOW, 0) // tkv
    sws = [_dot_nt(kw_ref[g, pl.ds(pl.multiple_of(w0 * tkv, tkv), nwv * tkv), :], qs[g]) + wb_ref[...]
           for g in groups]
    pws = [jnp.exp2(sw - jnp.max(sw, axis=0, keepdims=True)).astype(BF16) for sw in sws]
    accws = []
    for g in groups:
        accw = _dot(with_ones(vwt_ref[w0, g]), pws[g][0:tkv])
        for u in range(1, nwv):
            accw = accw + _dot(with_ones(vwt_ref[w0 + u, g]), pws[g][u * tkv:(u + 1) * tkv])
        accws.append(accw)

    rr = lax.broadcasted_iota(jnp.int32, (ncmp, 1), 0)
    jc = jnp.where(rr < nsel, 2 * rr, 2 * (rr - nsel) + 1)
    valid = ((jc + 1) * CMP_BLOCK - 1) <= lane_t
    jb = lax.broadcasted_iota(jnp.int32, (nsel, tq), 0)
    tt = t0 + lax.broadcasted_iota(jnp.int32, (nsel, tq), 1)
    forced = (jb == (tt >> 6)) | (jb == 0)
    future = jb * SEL_BLOCK > tt
    jf = jb.astype(F32)
    ocs, keys = [], []
    for g in groups:
        sm = jnp.where(valid, _dot_nt(kc_ref[g], qs[g]), NEG)
        e = jnp.exp2(sm - jnp.max(sm, axis=0, keepdims=True))
        pc = e * jnp.where(lane_t >= CMP_BLOCK - 1, 1.0 / jnp.sum(e, axis=0, keepdims=True), 0.0)
        ocs.append(_dot(vct_ref[g], pc.astype(BF16)))
        pp = pc[0:nsel] + pc[nsel:ncmp]
        imp = pp[:, 0:tq]
        for h in range(1, HPG):
            imp = imp + pp[:, h * tq:(h + 1) * tq]
        keys.append(jnp.where(forced, IMP_FORCED, jnp.where(future, IMP_FUTURE, imp)))

    biases = [jnp.full((nsel, tq), NEG, F32) for _ in groups]
    for _ in range(min(TOP_K, nsel)):
        for g in groups:
            cur = jnp.max(keys[g], axis=0, keepdims=True)
            first = jnp.min(jnp.where(keys[g] == cur, jf, float(nsel)), axis=0, keepdims=True)
            pick = jf == first
            biases[g] = jnp.where(pick, 0.0, biases[g])
            keys[g] = jnp.where(pick, IMP_TAKEN, keys[g])
    bpq = tq // SEL_BLOCK
    own = [(jb == bpq * i + u) for u in range(bpq)]
    own_bias = []
    for g in groups:
        bg = jnp.where(future, NEG, biases[g])
        own_bias.append([jnp.max(jnp.where(o, bg, NEG), axis=0, keepdims=True) for o in own])
        for o in own:
            bg = jnp.where(o, NEG, bg)
        bias4 = jnp.concatenate([bg] * HPG, axis=1)
        for j in range(nsel):
            bias_ref[g, j] = jnp.broadcast_to(bias4[j:j + 1, :], (8, r4))

    bpv = tkv // SEL_BLOCK
    vpt = tks // tkv
    nbt = vpt * bpv
    sub = SEL_BLOCK // 8

    def sel_scores(slot, kt):
        k0 = pl.multiple_of(kt * tks, tks)
        for g in groups:
            s = _dot_nt(ks_ref[g, pl.ds(k0, tks), :], qs[g])
            s_ref[slot, g] = (s.reshape(nbt, sub, 8, r4) + bias_ref[g, pl.ds(kt * nbt, nbt)][:, None]).reshape(tks, r4)

    def sel_update(state, ss, v0):
        mid = []
        for g in groups:
            m, acc = state[g]
            mn = jnp.maximum(m, jnp.max(ss[g], axis=0, keepdims=True))
            alpha = jnp.exp2(m - mn)
            mid.append((mn, alpha * acc, jnp.exp2(ss[g] - mn).astype(BF16)))
        out = []
        for g in groups:
            mn, acc, pb = mid[g]
            nv = ss[g].shape[0] // tkv
            step = 2 if nv % 2 == 0 else 1
            for u in range(0, nv, step):
                vt = jnp.concatenate([with_ones(vst_ref[v0 + u + w, g]) for w in range(step)], axis=1)
                acc = acc + _dot(vt, pb[u * tkv:(u + step) * tkv])
            out.append((mn, acc))
        return tuple(out)

    init = tuple((jnp.full((1, r4), NEG, F32), jnp.zeros((HEAD_DIM + ONES_ROWS, r4), F32)) for _ in groups)
    kd = t0 // tks

    def from_slot(slot):
        return tuple(s_ref[slot, g] for g in groups)

    def sel_pair(j, state):
        sel_scores(1, jnp.minimum(2 * j + 1, kd))
        state = sel_update(state, from_slot(0), 2 * j * vpt)
        sel_scores(0, jnp.minimum(2 * j + 2, kd))
        return sel_update(state, from_slot(1), (2 * j + 1) * vpt)

    sel_scores(0, 0)
    state = lax.fori_loop(0, (kd + 1) // 2, sel_pair, init)
    state = lax.cond(kd % 2 == 0, lambda st: sel_update(st, from_slot(0), kd * vpt), lambda st: st, state)
    kl = lax.broadcasted_iota(jnp.int32, (tq, 1), 0)
    tri = kl <= (lane_t - t0)
    s_own = []
    for g in groups:
        s = _dot_nt(ks_ref[g, pl.ds(pl.multiple_of(t0, tq), tq), :], qs[g])
        ob = jnp.concatenate([jnp.broadcast_to(jnp.concatenate([b] * HPG, axis=1), (SEL_BLOCK, r4))
                              for b in own_bias[g]], axis=0)
        s_own.append(jnp.where(tri, s + ob, NEG))
    sel = sel_update(state, tuple(s_own), i * (tq // tkv))

    ng = jax.nn.sigmoid(ng_ref[...])
    for g in groups:
        accw = accws[g]

        def gate(br):
            return jnp.concatenate([ng[g, h * 3 + br:h * 3 + br + 1, :] for h in range(HPG)], axis=1)

        accs = sel[g][1]
        ls, lw = accs[HEAD_DIM:HEAD_DIM + 1], accw[HEAD_DIM:HEAD_DIM + 1]
        out_t = (gate(0) * ocs[g] + (gate(1) / ls) * accs[0:HEAD_DIM]
                 + (gate(2) / lw) * accw[0:HEAD_DIM])
        stacked = jnp.concatenate([out_t[:, h * tq:(h + 1) * tq] for h in range(HPG)], axis=0)
        o_ref[:, g * gw:(g + 1) * gw] = stacked.T.astype(o_ref.dtype)


def _window_bias(tq):
    nvar = WINDOW // tq + 1
    v = jnp.arange(nvar)[:, None, None]
    u = jnp.arange(WINDOW + tq)[None, :, None]
    tt = jnp.arange(tq)[None, None, :]
    dpos = v * tq + tt - u
    ok = (dpos >= 0) & (dpos < WINDOW)
    return jnp.tile(jnp.where(ok, 0.0, NEG).astype(F32), (1, 1, HPG))


def _nsa_prompt(p, kc, vct, nb, s):
    tq = TQ
    tks = min(TKS, s)
    assert s % tks == 0 and tq == TKV and tq & (tq - 1) == 0 and tks % TKV == 0 and s >= WINDOW + tq
    nqb = s // tq
    r = nb * s
    ncmp = s // CMP_BLOCK
    gw = HPG * HEAD_DIM
    vst = p["vst"].reshape(r // TKV, N_KV, HEAD_DIM, TKV)
    vwt = p["vwt"].reshape(r // TKV, N_KV, HEAD_DIM, TKV)
    nvar = WINDOW // tq
    return pl.pallas_call(
        functools.partial(_nsa_prompt_kernel, tks=tks),
        grid=(nb, nqb),
        in_specs=[pl.BlockSpec((N_HEADS, tq, HEAD_DIM), lambda b, i: (0, b * nqb + i, 0)),
                  pl.BlockSpec((None, N_KV, ncmp, HEAD_DIM), lambda b, i: (b, 0, 0, 0)),
                  pl.BlockSpec((None, N_KV, HEAD_DIM, ncmp), lambda b, i: (b, 0, 0, 0)),
                  pl.BlockSpec((N_KV, s, HEAD_DIM), lambda b, i: (0, b, 0)),
                  pl.BlockSpec((s // TKV, N_KV, HEAD_DIM, TKV), lambda b, i: (b, 0, 0, 0)),
                  pl.BlockSpec((N_KV, s, HEAD_DIM), lambda b, i: (0, b, 0)),
                  pl.BlockSpec((s // TKV, N_KV, HEAD_DIM, TKV), lambda b, i: (b, 0, 0, 0)),
                  pl.BlockSpec((N_KV, 16, tq), lambda b, i: (0, 0, b * nqb + i)),
                  pl.BlockSpec((None, WINDOW + tq, HPG * tq), lambda b, i: (jnp.minimum(i, nvar), 0, 0))],
        out_specs=pl.BlockSpec((tq, NQ), lambda b, i: (b * nqb + i, 0)),
        out_shape=jax.ShapeDtypeStruct((r, NQ), BF16),
        scratch_shapes=[pltpu.VMEM((N_KV, s // SEL_BLOCK, 8, HPG * tq), F32),
                        pltpu.VMEM((2, N_KV, tks, HPG * tq), F32)],
        compiler_params=_cparams("parallel", "arbitrary"),
        name="nsa_prompt",
    )(p["nq"], kc, vct, p["ks"], vst, p["kw"], vwt, p["ngt"], _window_bias(tq))


def _prompt_cmp_operands(kcvc, nb, s):
    ncmp = s // CMP_BLOCK
    a = kcvc.reshape(nb, ncmp // 2, 2, 2, N_KV, HEAD_DIM)
    a = jnp.transpose(a, (0, 3, 4, 2, 1, 5)).reshape(nb, 2, N_KV, ncmp, HEAD_DIM)
    kc = a[:, 0].astype(BF16)
    vct = jnp.swapaxes(a[:, 1], -1, -2).astype(BF16)
    return kc, vct


def _by_group(x0, x1):
    return jnp.where(lax.broadcasted_iota(jnp.int32, x0.shape, 0) < HPG, x0, x1)


def _nsa_dec_cmp_kernel(pt_ref, *refs, pos):
    del pt_ref
    pages = refs[:PAGES_PER_STEP]
    wt_ref, seg_ref, q_ref, oc_ref, sel_ref, kcvc = refs[PAGES_PER_STEP:]
    pc = pl.program_id(1)

    @pl.when(pc == 0)
    def _():
        kcvc[...] = jnp.zeros_like(kcvc)

    x = jnp.concatenate([pg[...] for pg in pages], axis=1)
    kcvc[...] += _dot((x * wt_ref[...]).astype(BF16), seg_ref[...])

    @pl.when(pc == pl.num_programs(1) - 1)
    def _():
        ncmp = kcvc.shape[1]
        half = ncmp // 2
        qb = q_ref[...].astype(BF16)
        kv = kcvc[...].astype(BF16)
        kt = [kv[g * HEAD_DIM:(g + 1) * HEAD_DIM] for g in range(N_KV)]
        vt = [kv[NKV + g * HEAD_DIM:NKV + (g + 1) * HEAD_DIM] for g in range(N_KV)]
        s = _by_group(_dot(qb, kt[0]), _dot(qb, kt[1]))
        cc = lax.broadcasted_iota(jnp.int32, (1, ncmp), 1)
        jc = jnp.where(cc < half, 2 * cc, 2 * (cc - half) + 1)
        valid = ((jc + 1) * CMP_BLOCK - 1) <= pos
        sm = jnp.where(valid, s, NEG)
        e = jnp.exp2(sm - jnp.max(sm, axis=1, keepdims=True))
        p = jnp.where(valid, e / jnp.sum(e, axis=1, keepdims=True), 0.0)
        pb = p.astype(BF16)
        oc_ref[...] = _by_group(_dot_nt(pb, vt[0]), _dot_nt(pb, vt[1]))
        pp = p[:, 0:half] + p[:, half:ncmp]
        jl = lax.broadcasted_iota(jnp.int32, (1, half), 1)
        ii = lax.broadcasted_iota(jnp.int32, (half, half), 0)
        jj = lax.broadcasted_iota(jnp.int32, (half, half), 1)
        k_past = min(TOP_K, half + 1) - 1
        for g in range(N_KV):
            imp = jnp.sum(pp[g * HPG:(g + 1) * HPG], axis=0, keepdims=True)
            forced = (jl == pos // SEL_BLOCK) | (jl == 0)
            key = jnp.where(forced, IMP_FORCED, jnp.where(jl * SEL_BLOCK > pos, IMP_FUTURE, imp))
            kcol = jnp.sum(jnp.where(ii == jj, key, 0.0), axis=1, keepdims=True)
            ahead = (kcol > key) | ((kcol == key) & (ii < jj))
            rank = jnp.sum(jnp.where(ahead, 1.0, 0.0), axis=0, keepdims=True)
            sel = jnp.where(rank < k_past, 1.0, 0.0)
            sel_ref[g * HPG:(g + 1) * HPG, :] = jnp.broadcast_to(sel, (HPG, half))


def _page_specs():
    def spec(kk):
        return pl.BlockSpec((None, ROW_W, PAGE_SIZE), lambda b, pc, pt: (pt[b, pc * PAGES_PER_STEP + kk], 0, 0))
    return [spec(kk) for kk in range(PAGES_PER_STEP)]


def _nsa_dec_cmp(cache_t, page_table, q, wt, seg):
    nb, n_pages = page_table.shape
    assert n_pages % PAGES_PER_STEP == 0
    npc = n_pages // PAGES_PER_STEP
    past = n_pages * PAGE_SIZE
    ncmp = past // CMP_BLOCK
    nselp = ncmp // 2
    keys = PAGES_PER_STEP * PAGE_SIZE
    perb = lambda b, pc, pt: (b, 0, 0)
    grid_spec = pltpu.PrefetchScalarGridSpec(
        num_scalar_prefetch=1,
        grid=(nb, npc),
        in_specs=_page_specs() + [
            pl.BlockSpec((ROW_W, keys), lambda b, pc, pt: (0, 0)),
            pl.BlockSpec((None, keys, ncmp), lambda b, pc, pt: (pc, 0, 0)),
            pl.BlockSpec((None, N_HEADS, HEAD_DIM), perb)],
        out_specs=[pl.BlockSpec((None, N_HEADS, HEAD_DIM), perb),
                   pl.BlockSpec((None, N_HEADS, nselp), perb)],
        scratch_shapes=[pltpu.VMEM((ROW_W, ncmp), F32)],
    )
    return pl.pallas_call(
        functools.partial(_nsa_dec_cmp_kernel, pos=past),
        grid_spec=grid_spec,
        out_shape=[jax.ShapeDtypeStruct((nb, N_HEADS, HEAD_DIM), F32),
                   jax.ShapeDtypeStruct((nb, N_HEADS, nselp), F32)],
        compiler_params=_cparams("parallel", "arbitrary"),
        name="nsa_decode_cmp",
    )(page_table, *([cache_t] * PAGES_PER_STEP), wt, seg, q)


def _nsa_dec_sel_kernel(pt_ref, *refs, pos, wb):
    del pt_ref
    pages = refs[:PAGES_PER_STEP]
    (q_ref, sel_ref, exp_ref, oc_ref, ng_ref, nslc_ref, nwin_ref, nwcol_ref, win_ref,
     o_ref, wout_ref, m_s, l_s, acc_s) = refs[PAGES_PER_STEP:]
    pc = pl.program_id(1)
    q = q_ref[...]
    qb = q.astype(BF16)
    ksl = [slice(g * HEAD_DIM, (g + 1) * HEAD_DIM) for g in range(N_KV)]
    vsl = [slice(NKV + g * HEAD_DIM, NKV + (g + 1) * HEAD_DIM) for g in range(N_KV)]

    @pl.when(pc == 0)
    def _():
        m_s[...] = jnp.full_like(m_s, NEG)
        l_s[...] = jnp.zeros_like(l_s)
        acc_s[...] = jnp.zeros_like(acc_s)

    x = jnp.concatenate([pg[...] for pg in pages], axis=1).astype(BF16)
    keep = _dot(sel_ref[...].astype(BF16), exp_ref[...])
    s = jnp.where(keep > 0.5, _by_group(_dot(qb, x[ksl[0]]), _dot(qb, x[ksl[1]])), NEG)
    m = m_s[:, 0:1]
    mn = jnp.maximum(m, jnp.max(s, axis=1, keepdims=True))
    alpha = jnp.exp2(m - mn)
    p = jnp.exp2(s - mn)
    l_s[...] = alpha * l_s[...] + jnp.sum(p, axis=1, keepdims=True)
    pb = p.astype(BF16)
    acc_s[...] = alpha * acc_s[...] + _by_group(_dot_nt(pb, x[vsl[0]]), _dot_nt(pb, x[vsl[1]]))
    m_s[...] = jnp.broadcast_to(mn, m_s.shape)

    @pl.when(pc == pl.num_programs(1) - 1)
    def _():
        def new_row(row):
            sn = _by_group(jnp.sum(q * row[:, ksl[0]], axis=1, keepdims=True),
                           jnp.sum(q * row[:, ksl[1]], axis=1, keepdims=True))
            return sn, _by_group(jnp.broadcast_to(row[:, vsl[0]], q.shape), jnp.broadcast_to(row[:, vsl[1]], q.shape))

        sn, vn = new_row(nslc_ref[...])
        m1 = m_s[:, 0:1]
        m2 = jnp.maximum(m1, sn)
        a1, pn = jnp.exp2(m1 - m2), jnp.exp2(sn - m2)
        o_sel = (a1 * acc_s[...] + pn * vn) / (a1 * l_s[:, 0:1] + pn)
        win = win_ref[...]
        wbf = win.astype(BF16)
        sw = _by_group(_dot(qb, wbf[ksl[0]]), _dot(qb, wbf[ksl[1]]))
        dpos = pos - (pos - wb + lax.broadcasted_iota(jnp.int32, (1, wb), 1))
        okw = (dpos >= 0) & (dpos < WINDOW) & (pos - dpos >= 0)
        sw = jnp.where(okw, sw, NEG)
        swn, vwn = new_row(nwin_ref[...])
        mw = jnp.maximum(jnp.max(sw, axis=1, keepdims=True), swn)
        pw = jnp.exp2(sw - mw)
        pwn = jnp.exp2(swn - mw)
        pwb = pw.astype(BF16)
        o_win = ((_by_group(_dot_nt(pwb, wbf[vsl[0]]), _dot_nt(pwb, wbf[vsl[1]])) + pwn * vwn)
                 / (jnp.sum(pw, axis=1, keepdims=True) + pwn))
        gates = jax.nn.sigmoid(ng_ref[...])
        o_ref[...] = gates[:, 0:1] * oc_ref[...] + gates[:, 1:2] * o_sel + gates[:, 2:3] * o_win
        lid = lax.broadcasted_iota(jnp.int32, (1, wb), 1)
        wout_ref[...] = jnp.where(lid == wb - 1, nwcol_ref[...], pltpu.roll(win, wb - 1, 1))


def _nsa_dec_sel(cache_t, page_table, q, sel, expand, oc, ng, nslc, nwin, cache_win_t):
    nb, n_pages = page_table.shape
    npc = n_pages // PAGES_PER_STEP
    past = n_pages * PAGE_SIZE
    wb = cache_win_t.shape[-1]
    assert wb == WINDOW and past >= WINDOW
    nselp = sel.shape[-1]
    keys = PAGES_PER_STEP * PAGE_SIZE
    perb = lambda b, pc, pt: (b, 0, 0)
    grid_spec = pltpu.PrefetchScalarGridSpec(
        num_scalar_prefetch=1,
        grid=(nb, npc),
        in_specs=_page_specs() + [
            pl.BlockSpec((None, N_HEADS, HEAD_DIM), perb),
            pl.BlockSpec((None, N_HEADS, nselp), perb),
            pl.BlockSpec((None, nselp, keys), lambda b, pc, pt: (pc, 0, 0)),
            pl.BlockSpec((None, N_HEADS, HEAD_DIM), perb),
            pl.BlockSpec((None, N_HEADS, LANES), perb),
            pl.BlockSpec((None, 1, ROW_W), perb),
            pl.BlockSpec((None, 1, ROW_W), perb),
            pl.BlockSpec((None, ROW_W, 1), perb),
            pl.BlockSpec((None, ROW_W, wb), perb)],
        out_specs=[pl.BlockSpec((None, N_HEADS, HEAD_DIM), perb),
                   pl.BlockSpec((None, ROW_W, wb), perb)],
        scratch_shapes=[pltpu.VMEM((N_HEADS, LANES), F32), pltpu.VMEM((N_HEADS, LANES), F32),
                        pltpu.VMEM((N_HEADS, HEAD_DIM), F32)],
    )
    return pl.pallas_call(
        functools.partial(_nsa_dec_sel_kernel, pos=past, wb=wb),
        grid_spec=grid_spec,
        out_shape=[jax.ShapeDtypeStruct((nb, N_HEADS, HEAD_DIM), F32),
                   jax.ShapeDtypeStruct((nb, ROW_W, wb), F32)],
        compiler_params=_cparams("parallel", "arbitrary"),
        name="nsa_decode_sel",
    )(page_table, *([cache_t] * PAGES_PER_STEP), q, sel, expand, oc, ng, nslc, nwin,
      nwin.reshape(nb, ROW_W, 1), cache_win_t)


def _merge_kernel(x_ref, ya_ref, yb_ref, gab_ref, ga_ref, wa_ref, wb_ref, wo_ref, g_ref, o_ref):
    d = x_ref.shape[1]
    pa = _dot(ya_ref[...], wa_ref[...])
    pb = _dot(yb_ref[...], wb_ref[...])
    merged = jax.nn.sigmoid(gab_ref[:, 0:d]) * pa + jax.nn.sigmoid(gab_ref[:, d:2 * d]) * pb
    z = _dot(merged.astype(BF16), wo_ref[...])
    o_ref[...] = x_ref[...] + ga_ref[0] * _rms(z, g_ref[...])


def _merge(x, ya, yb, gab, ga1, w_b, wts, g_post, tm, rows_per_mod):
    r, d = x.shape
    wyb = yb.shape[1]
    tpm = rows_per_mod // tm
    mr = ga1.shape[1]
    row = lambda i: (i, 0)
    const2 = lambda i: (0, 0)
    return pl.pallas_call(
        _merge_kernel,
        grid=(r // tm,),
        in_specs=[pl.BlockSpec((tm, d), row), pl.BlockSpec((tm, MV), row), pl.BlockSpec((tm, wyb), row),
                  pl.BlockSpec((tm, 2 * d), row),
                  pl.BlockSpec((1, mr, d), lambda i: (i // tpm, 0, 0)),
                  pl.BlockSpec((MV, d), const2), pl.BlockSpec((wyb, d), const2), pl.BlockSpec((d, d), const2),
                  pl.BlockSpec((1, d), const2)],
        out_specs=pl.BlockSpec((tm, d), row),
        out_shape=jax.ShapeDtypeStruct((r, d), F32),
        compiler_params=_cparams("parallel"),
        name="merge",
    )(x, ya, yb, gab, ga1, wts["w_a"], w_b, wts["w_o"], g_post.reshape(1, d))


def _ffn_kernel(x_ref, sc_ref, sh_ref, ga_ref, g1_ref, g2_ref, wu_ref, wd_ref, o_ref):
    dff = wd_ref.shape[0]
    x = x_ref[...]
    hb = (_rms(x, g1_ref[...]) * (1.0 + sc_ref[0]) + sh_ref[0]).astype(BF16)
    acc = jnp.zeros(x.shape, F32)
    for c in range(dff // FFN_CK):
        lo, hi = c * FFN_CK, (c + 1) * FFN_CK
        gate = _dot(hb, wu_ref[:, lo:hi])
        up = _dot(hb, wu_ref[:, dff + lo:dff + hi])
        acc = acc + _dot((gate * jax.nn.sigmoid(gate) * up).astype(BF16), wd_ref[lo:hi, :])
    o_ref[...] = x + ga_ref[0] * _rms(acc, g2_ref[...])


def _ffn(x, sc, sh, ga2, g_pre, g_post, wts, tm, rows_per_mod):
    r, d = x.shape
    dff = wts["w_down"].shape[0]
    assert dff % FFN_CK == 0
    tpm = rows_per_mod // tm
    mr = sc.shape[1]
    row = lambda i: (i, 0)
    const2 = lambda i: (0, 0)
    mod = pl.BlockSpec((1, mr, d), lambda i: (i // tpm, 0, 0))
    return pl.pallas_call(
        _ffn_kernel,
        grid=(r // tm,),
        in_specs=[pl.BlockSpec((tm, d), row), mod, mod, mod,
                  pl.BlockSpec((1, d), const2), pl.BlockSpec((1, d), const2),
                  pl.BlockSpec((d, 2 * dff), const2), pl.BlockSpec((dff, d), const2)],
        out_specs=pl.BlockSpec((tm, d), row),
        out_shape=jax.ShapeDtypeStruct((r, d), F32),
        compiler_params=_cparams("parallel"),
        name="ffn",
    )(x, sc, sh, ga2, g_pre.reshape(1, d), g_post.reshape(1, d), wts["w_up"], wts["w_down"])


def _prep_weights(w_in, b_in, w_cmp_k, w_cmp_v, w_proj_a, w_proj_b, w_out, w_up, w_down):
    o_mi, o_nq, o_ng, o_ga = 4 * 256 + 2 * 256, 1544, 2824, 2848
    o_nkv = o_nq + NQ
    main_cols = [slice(0, o_mi), slice(o_nq, o_ng), slice(o_ga, w_in.shape[1])]
    w_main = jnp.concatenate([w_in[:, s] for s in main_cols], axis=1).astype(BF16)
    b_main = jnp.concatenate([b_in[s] for s in main_cols]).reshape(1, -1)
    z = lambda n: jnp.zeros((w_in.shape[0], n), w_in.dtype)
    ng0, ng1 = slice(o_ng, o_ng + 12), slice(o_ng + 12, o_ng + 24)
    tg_cols = [w_in[:, o_mi:o_mi + 8], z(8), w_in[:, ng0], z(4), w_in[:, ng1], z(4)]
    tg_b = [b_in[o_mi:o_mi + 8], jnp.zeros(8), b_in[ng0], jnp.zeros(4), b_in[ng1], jnp.zeros(4)]
    vs_, vw_ = slice(o_nkv + 3 * NKV, o_nkv + 4 * NKV), slice(o_nkv + 5 * NKV, o_nkv + 6 * NKV)
    nkv_part = lambda j: slice(o_nkv + j * NKV, o_nkv + (j + 1) * NKV)
    tb_parts = [slice(256, 512), vs_, vw_, nkv_part(0), nkv_part(1), nkv_part(2), nkv_part(4)]
    tb_cols = [w_in[:, s] for s in tb_parts]
    tb_b = [b_in[s] for s in tb_parts]
    w_cmp = jnp.concatenate([w_cmp_k, w_cmp_k, w_cmp_v, w_cmp_v], axis=1)
    return {
        "w_main": w_main, "b_main": b_main,
        "w_tg": jnp.concatenate(tg_cols, axis=1).T.astype(BF16),
        "b_tg": jnp.concatenate(tg_b).reshape(-1, 1).astype(F32),
        "w_tb": jnp.concatenate(tb_cols, axis=1).T.astype(BF16),
        "b_tb": jnp.concatenate(tb_b).reshape(-1, 1),
        "w_cmp": w_cmp,
        "w_a": w_proj_a.astype(BF16), "w_b": w_proj_b.astype(BF16), "w_o": w_out.astype(BF16),
        "w_up": w_up.astype(BF16), "w_down": w_down.astype(BF16),
    }


def _rope_tables(pos):
    half = ROPE_DIM // 2
    n = pos.shape[0]
    inv = ROPE_THETA ** (-jnp.arange(half, dtype=F32) * 2.0 / ROPE_DIM)
    ang = pos.astype(F32)[:, None] * inv[None, :]
    cos, sin = jnp.cos(ang), jnp.sin(ang)
    rest = HEAD_DIM - ROPE_DIM
    zh, zr = jnp.zeros((n, half), F32), jnp.zeros((n, rest), F32)
    a = jnp.concatenate([cos, cos, jnp.ones((n, rest), F32)], axis=1)
    p = jnp.concatenate([zh, sin, zr], axis=1)
    m = jnp.concatenate([-sin, zh, zr], axis=1)
    return tuple(jnp.tile(t, (1, LANES // HEAD_DIM)) for t in (a, p, m)) + (cos.T, sin.T)


def _rows_on_lanes(cache):
    n, rows = cache.shape[0], cache.shape[1]
    return jnp.transpose(cache, (0, 2, 3, 4, 1)).reshape(n, ROW_W, rows)


def _cmp_step_operands(w_cmp, n_pages):
    npc = n_pages // PAGES_PER_STEP
    keys = PAGES_PER_STEP * PAGE_SIZE
    ncmp = n_pages * PAGE_SIZE // CMP_BLOCK
    wt = jnp.tile(w_cmp.T, (1, keys // CMP_BLOCK))
    blk = (jnp.arange(npc)[:, None] * keys + jnp.arange(keys)[None, :]) // CMP_BLOCK
    col = (blk % 2) * (ncmp // 2) + blk // 2
    seg = (col[:, :, None] == jnp.arange(ncmp)[None, None, :]).astype(BF16)
    return wt, seg


def _block_expand(n_pages):
    npc = n_pages // PAGES_PER_STEP
    keys = PAGES_PER_STEP * PAGE_SIZE
    nselp = n_pages * PAGE_SIZE // SEL_BLOCK
    blk = (jnp.arange(npc)[:, None] * keys + jnp.arange(keys)[None, :]) // SEL_BLOCK
    return (jnp.arange(nselp)[None, :, None] == blk[:, None, :]).astype(BF16)


def _layer_prompt(x, mod, wts, norms, mlstm_norm_w):
    nb, s, d = x.shape
    r = nb * s
    sh1, sc1, ga1, sh2, sc2, ga2 = [m.reshape(nb, 1, d) for m in jnp.split(mod, 6, axis=-1)]
    g_pre_mix, g_post_mix, g_pre_ffn, g_post_ffn = norms
    tm = TM_PROJ
    assert s % tm == 0
    tabs = _rope_tables(jnp.arange(s, dtype=jnp.int32))
    x2 = x.reshape(r, d)
    p = _in_proj(x2, g_pre_mix, sc1, sh1, wts, tabs, tm, s, True)
    ya, c_new, n_new, m_new = _mlstm_prompt(p, mlstm_norm_w, nb, s)
    kc, vct = _prompt_cmp_operands(p["kcvc"], nb, s)
    yb = _nsa_prompt(p, kc, vct, nb, s)
    x1 = _merge(x2, ya, yb, p["gab"], ga1, wts["w_b"], wts, g_post_mix, tm, s)
    y = _ffn(x1, sc2, sh2, ga2, g_pre_ffn, g_post_ffn, wts, tm, s)
    rows = lambda a: jnp.transpose(a.reshape(nb, 2, N_KV, HEAD_DIM, -1), (0, 4, 1, 2, 3))
    win = rows(p["win"][:, :, s - min(WINDOW, s):])
    return y.reshape(nb, s, d), (rows(p["cmp"]), rows(p["slc"]), win, c_new, n_new, m_new)


def _layer_sample(x, mod, wts, norms, mlstm_norm_w, cache_cmp, cache_slc, cache_win, page_table, c0, n0, m0):
    nb, ts, d = x.shape
    assert ts == 1
    sh1, sc1, ga1, sh2, sc2, ga2 = [m.reshape(1, nb, d) for m in jnp.split(mod, 6, axis=-1)]
    g_pre_mix, g_post_mix, g_pre_ffn, g_post_ffn = norms
    n_pages = page_table.shape[1]
    past = n_pages * PAGE_SIZE
    tm = nb
    tabs = _rope_tables(jnp.full((nb,), past, jnp.int32))
    x2 = x.reshape(nb, d)
    p = _in_proj(x2, g_pre_mix, sc1, sh1, wts, tabs, tm, nb, False)
    ya, c_new, n_new, m_new = _mlstm_step(p, p["gt"].T, c0, n0, m0, mlstm_norm_w)
    qs = jnp.transpose(p["nq"], (1, 0, 2)).astype(F32)
    ng = jnp.transpose(p["ngt"][:, :HPG * 3], (2, 0, 1)).reshape(nb, N_HEADS, 3)
    ng = jnp.pad(ng, ((0, 0), (0, 0), (0, LANES - 3)))
    wt, seg = _cmp_step_operands(wts["w_cmp"], n_pages)
    oc, sel = _nsa_dec_cmp(_rows_on_lanes(cache_cmp), page_table, qs, wt, seg)
    o, win_new = _nsa_dec_sel(_rows_on_lanes(cache_slc), page_table, qs, sel, _block_expand(n_pages), oc, ng,
                              p["slc"].reshape(nb, 1, ROW_W), p["win"].reshape(nb, 1, ROW_W),
                              _rows_on_lanes(cache_win))
    yb = o.reshape(nb, NQ).astype(BF16)
    x1 = _merge(x2, ya.astype(BF16), yb, p["gab"], ga1, wts["w_b"], wts, g_post_mix, tm, nb)
    y = _ffn(x1, sc2, sh2, ga2, g_pre_ffn, g_post_ffn, wts, tm, nb)
    rows = lambda a: a.reshape(nb, -1, 2, N_KV, HEAD_DIM)
    win_rows = jnp.transpose(win_new.reshape(nb, 2, N_KV, HEAD_DIM, -1), (0, 4, 1, 2, 3))
    return y.reshape(nb, 1, d), (rows(p["cmp"]), rows(p["slc"]), win_rows, c_new, n_new, m_new)


def kernel(x_prompt, x_sample, c_prompt, c_sample, cache_cmp_kv, cache_slc_kv, cache_win_kv, state_mlstm_C, state_mlstm_n, state_mlstm_m, page_table, g_pre_mix, g_post_mix, g_pre_ffn, g_post_ffn, w_ada, b_ada, w_in, b_in, mlstm_norm_w, w_cmp_k, w_cmp_v, w_proj_a, w_proj_b, w_out, w_up, w_down):
    depth = w_in.shape[0]
    nbp, nbs = x_prompt.shape[0], x_sample.shape[0]
    xp, xs = x_prompt, x_sample
    c_all = jnp.concatenate([c_prompt, c_sample], axis=0)
    pad = (-c_all.shape[0]) % 8
    c_all = jnp.pad(c_all, ((0, pad), (0, 0)))
    st_p = [[] for _ in range(6)]
    st_s = [[] for _ in range(6)]
    for l in range(depth):
        wts = _prep_weights(w_in[l], b_in[l], w_cmp_k[l], w_cmp_v[l], w_proj_a[l], w_proj_b[l], w_out[l],
                            w_up[l], w_down[l])
        norms = (g_pre_mix[l], g_post_mix[l], g_pre_ffn[l], g_post_ffn[l])
        mod = _adaln(c_all, w_ada[l], b_ada[l])
        xp, sp = _layer_prompt(xp, mod[:nbp], wts, norms, mlstm_norm_w[l])
        xs, ss = _layer_sample(xs, mod[nbp:nbp + nbs], wts, norms, mlstm_norm_w[l], cache_cmp_kv[l],
                               cache_slc_kv[l], cache_win_kv[l], page_table, state_mlstm_C[l],
                               state_mlstm_n[l], state_mlstm_m[l])
        for j in range(6):
            st_p[j].append(sp[j])
            st_s[j].append(ss[j])
    outs_p = [jnp.stack(a, axis=0) for a in st_p]
    outs_s = [jnp.stack(a, axis=0) for a in st_s]
    return (xp, xs, *outs_p, *outs_s)
```

```python
import functools

import jax
import jax.numpy as jnp
from jax import lax
from jax.experimental import pallas as pl
from jax.experimental.pallas import tpu as pltpu

F32 = jnp.float32
BF16 = jnp.bfloat16

M_HEADS, M_DQK, M_DV = 4, 64, 128
GATE_CAP = 15.0
N_HEADS, N_KV, HEAD_DIM = 8, 2, 64
HPG = N_HEADS // N_KV
CMP_BLOCK, SEL_BLOCK, TOP_K, WINDOW = 32, 64, 16, 512
ROPE_DIM = HEAD_DIM // 4
ROPE_THETA = 500000.0
PAGE_SIZE = 128
EPS = 1e-6
NEG = -1e30
LOG2E = 1.4426950408889634
MQK, MV = M_HEADS * M_DQK, M_HEADS * M_DV
NQ, NKV = N_HEADS * HEAD_DIM, N_KV * HEAD_DIM
ROW_W = 2 * NKV

LANES = 128
VMEM_LIMIT = 48 * 1024 * 1024

TM_PROJ = 512
T_CHUNK = 256
TQ = 128
TKS = 512
TKV = 128
ONES_ROWS = 16
PAGES_PER_STEP = 16
FFN_CK = 256

IMP_FORCED, IMP_FUTURE, IMP_TAKEN = 1e30, -1e30, -2e30


def _dot(a, b):
    return jnp.dot(a, b, preferred_element_type=F32)


def _dot_nt(a, b):
    return lax.dot_general(a, b, (((1,), (1,)), ((), ())), preferred_element_type=F32)


def _rms(x, g):
    return x * lax.rsqrt(jnp.mean(x * x, axis=-1, keepdims=True) + EPS) * g


def _log_sigmoid(x):
    return jnp.minimum(x, 0.0) - jnp.log1p(jnp.exp(-jnp.abs(x)))


def _cparams(*sem):
    return pltpu.CompilerParams(dimension_semantics=sem, vmem_limit_bytes=VMEM_LIMIT)


def _adaln_kernel(c_ref, w_ref, b_ref, o_ref):
    c = c_ref[...]
    a = (c * jax.nn.sigmoid(c)).astype(BF16)
    o_ref[...] = _dot(a, w_ref[...].astype(BF16)) + b_ref[...]


def _adaln(c, w, b):
    r, d = c.shape
    n = w.shape[1]
    tn = 1536
    return pl.pallas_call(
        _adaln_kernel,
        grid=(n // tn,),
        in_specs=[pl.BlockSpec((r, d), lambda j: (0, 0)),
                  pl.BlockSpec((d, tn), lambda j: (0, j)),
                  pl.BlockSpec((1, tn), lambda j: (0, j))],
        out_specs=pl.BlockSpec((r, tn), lambda j: (0, j)),
        out_shape=jax.ShapeDtypeStruct((r, n), F32),
        compiler_params=_cparams("arbitrary"),
        name="adaln",
    )(c, w, b.reshape(1, n))


_C_MQ, _C_MK, _C_MV, _C_MO, _C_NQ, _C_NKV, _C_GAB, _C_END = 0, 256, 512, 1024, 1536, 2048, 2816, 4864


def _proj_kernel(x_ref, g_ref, sc_ref, sh_ref, wm_ref, bm_ref, wtg_ref, btg_ref, wtb_ref, btb_ref,
                 ra_ref, rp_ref, rm_ref, ct_ref, st_ref, wc_ref,
                 mq_ref, mk_ref, mv_ref, mo_ref, nq_ref, cmp_ref, slc_ref, win_ref, ks_ref, kw_ref,
                 gab_ref, kcvc_ref, gt_ref, ngt_ref, mkt_ref, vst_ref, vwt_ref, *, tks, tkw, rows_t):
    tm = x_ref.shape[0]
    x = x_ref[...]
    h = _rms(x, g_ref[...]) * (1.0 + sc_ref[0]) + sh_ref[0]
    hb = h.astype(BF16)

    def mm(lo, hi):
        return _dot(hb, wm_ref[:, lo:hi]) + bm_ref[:, lo:hi]

    ra, rp, rm = ra_ref[...], rp_ref[...], rm_ref[...]

    def rope(xc):
        return xc * ra + pltpu.roll(xc, 8, 1) * rp + pltpu.roll(xc, LANES - 8, 1) * rm

    scale_m = M_DQK ** -0.5
    scale_n = HEAD_DIM ** -0.5 * LOG2E
    mq_ref[...] = (mm(_C_MQ, _C_MK) * scale_m).astype(BF16)
    mk_ref[...] = mm(_C_MK, _C_MV).astype(BF16)
    mv_ref[...] = mm(_C_MV, _C_MO).astype(BF16)
    mo_ref[...] = mm(_C_MO, _C_NQ)

    nq = mm(_C_NQ, _C_NKV)
    for c in range(NQ // LANES):
        r = (rope(nq[:, c * LANES:(c + 1) * LANES]) * scale_n).astype(BF16)
        nq_ref[2 * c] = r[:, :HEAD_DIM]
        nq_ref[2 * c + 1] = r[:, HEAD_DIM:]

    nkv = mm(_C_NKV, _C_GAB)
    kc = rope(nkv[:, 0:128])
    vc = nkv[:, 128:256]
    ksr = rope(nkv[:, 256:384])
    kwr = rope(nkv[:, 512:640])
    if not rows_t:
        cmp_ref[:, 0:NKV] = kc
        cmp_ref[:, NKV:ROW_W] = vc
        slc_ref[:, 0:NKV] = ksr
        slc_ref[:, NKV:ROW_W] = nkv[:, 384:512]
        win_ref[:, 0:NKV] = kwr
        win_ref[:, NKV:ROW_W] = nkv[:, 640:768]
    for g in range(N_KV):
        ks_ref[g] = ksr[:, g * HEAD_DIM:(g + 1) * HEAD_DIM].astype(BF16)
        kw_ref[g] = kwr[:, g * HEAD_DIM:(g + 1) * HEAD_DIM].astype(BF16)

    wc = wc_ref[...]
    nb = tm // CMP_BLOCK
    kcvc_ref[:, 0:NKV] = jnp.sum(kc.reshape(nb, CMP_BLOCK, NKV) * wc[None, :, 0:NKV], axis=1)
    kcvc_ref[:, NKV:ROW_W] = jnp.sum(vc.reshape(nb, CMP_BLOCK, NKV) * wc[None, :, NKV:ROW_W], axis=1)

    gab_ref[...] = mm(_C_GAB, _C_END)

    tg = _dot_nt(wtg_ref[...], hb) + btg_ref[...]
    gt_ref[...] = tg[0:8]
    ngt_ref[0] = tg[16:32]
    ngt_ref[1] = tg[32:48]
    tb = _dot_nt(wtb_ref[...], hb) + btb_ref[...]
    mkt_ref[...] = tb[0:MQK].astype(BF16)
    for j in range(tm // tks):
        vst_ref[j] = tb[MQK:MQK + NKV, j * tks:(j + 1) * tks].astype(BF16)
    for j in range(tm // tkw):
        vwt_ref[j] = tb[MQK + NKV:MQK + 2 * NKV, j * tkw:(j + 1) * tkw].astype(BF16)

    if rows_t:
        ct, st = ct_ref[...], st_ref[...]
        half = ROPE_DIM // 2

        def rope_t(kt):
            parts = []
            for g in range(N_KV):
                blk = kt[g * HEAD_DIM:(g + 1) * HEAD_DIM]
                x1, x2 = blk[0:half], blk[half:ROPE_DIM]
                parts += [x1 * ct - x2 * st, x2 * ct + x1 * st, blk[ROPE_DIM:HEAD_DIM]]
            return jnp.concatenate(parts, axis=0)

        o = MQK + 2 * NKV
        cmp_ref[0:NKV] = rope_t(tb[o:o + NKV])
        cmp_ref[NKV:ROW_W] = tb[o + NKV:o + 2 * NKV]
        slc_ref[0:NKV] = rope_t(tb[o + 2 * NKV:o + 3 * NKV])
        slc_ref[NKV:ROW_W] = tb[MQK:MQK + NKV]
        win_ref[0:NKV] = rope_t(tb[o + 3 * NKV:o + 4 * NKV])
        win_ref[NKV:ROW_W] = tb[MQK + NKV:MQK + 2 * NKV]


def _in_proj(x, g_pre, sc, sh, wts, rope_tabs, tm, rows_per_mod, rows_t):
    r, d = x.shape
    nt = r // tm
    tks = tkw = min(TKV, tm)
    tpm = rows_per_mod // tm
    tpr = rope_tabs[0].shape[0] // tm
    mr = sc.shape[1]
    row = lambda i: (i, 0)
    const2 = lambda i: (0, 0)
    if rows_t:
        rows_shape = (r // rows_per_mod, ROW_W, rows_per_mod)
        rows_spec = pl.BlockSpec((None, ROW_W, tm), lambda i: (i // tpm, 0, i % tpm))
    else:
        rows_shape = (r, ROW_W)
        rows_spec = pl.BlockSpec((tm, ROW_W), row)
    ntb = wts["w_tb"].shape[0]
    in_specs = [
        pl.BlockSpec((tm, d), row),
        pl.BlockSpec((1, d), const2),
        pl.BlockSpec((1, mr, d), lambda i: (i // tpm, 0, 0)),
        pl.BlockSpec((1, mr, d), lambda i: (i // tpm, 0, 0)),
        pl.BlockSpec((d, _C_END), const2),
        pl.BlockSpec((1, _C_END), const2),
        pl.BlockSpec((48, d), const2),
        pl.BlockSpec((48, 1), const2),
        pl.BlockSpec((ntb, d), const2),
        pl.BlockSpec((ntb, 1), const2),
        pl.BlockSpec((tm, LANES), lambda i: (i % tpr, 0)),
        pl.BlockSpec((tm, LANES), lambda i: (i % tpr, 0)),
        pl.BlockSpec((tm, LANES), lambda i: (i % tpr, 0)),
        pl.BlockSpec((ROPE_DIM // 2, tm), lambda i: (0, i % tpr)),
        pl.BlockSpec((ROPE_DIM // 2, tm), lambda i: (0, i % tpr)),
        pl.BlockSpec((CMP_BLOCK, ROW_W), const2),
    ]
    outs = [
        ("mq", (r, MQK), BF16, pl.BlockSpec((tm, MQK), row)),
        ("mk", (r, MQK), BF16, pl.BlockSpec((tm, MQK), row)),
        ("mv", (r, MV), BF16, pl.BlockSpec((tm, MV), row)),
        ("mo", (r, MV), F32, pl.BlockSpec((tm, MV), row)),
        ("nq", (N_HEADS, r, HEAD_DIM), BF16, pl.BlockSpec((N_HEADS, tm, HEAD_DIM), lambda i: (0, i, 0))),
        ("cmp", rows_shape, F32, rows_spec),
        ("slc", rows_shape, F32, rows_spec),
        ("win", rows_shape, F32, rows_spec),
        ("ks", (N_KV, r, HEAD_DIM), BF16, pl.BlockSpec((N_KV, tm, HEAD_DIM), lambda i: (0, i, 0))),
        ("kw", (N_KV, r, HEAD_DIM), BF16, pl.BlockSpec((N_KV, tm, HEAD_DIM), lambda i: (0, i, 0))),
        ("gab", (r, 2 * d), F32, pl.BlockSpec((tm, 2 * d), row)),
        ("kcvc", (r // CMP_BLOCK, ROW_W), F32, pl.BlockSpec((tm // CMP_BLOCK, ROW_W), row)),
        ("gt", (8, r), F32, pl.BlockSpec((8, tm), lambda i: (0, i))),
        ("ngt", (N_KV, 16, r), F32, pl.BlockSpec((N_KV, 16, tm), lambda i: (0, 0, i))),
        ("mkt", (MQK, r), BF16, pl.BlockSpec((MQK, tm), lambda i: (0, i))),
        ("vst", (r // tks, NKV, tks), BF16, pl.BlockSpec((tm // tks, NKV, tks), lambda i: (i, 0, 0))),
        ("vwt", (r // tkw, NKV, tkw), BF16, pl.BlockSpec((tm // tkw, NKV, tkw), lambda i: (i, 0, 0))),
    ]
    res = pl.pallas_call(
        functools.partial(_proj_kernel, tks=tks, tkw=tkw, rows_t=rows_t),
        grid=(nt,),
        in_specs=in_specs,
        out_specs=[o[3] for o in outs],
        out_shape=[jax.ShapeDtypeStruct(o[1], o[2]) for o in outs],
        compiler_params=_cparams("parallel"),
        name="in_proj",
    )(x, g_pre.reshape(1, d), sc, sh, wts["w_main"], wts["b_main"], wts["w_tg"], wts["b_tg"],
      wts["w_tb"], wts["b_tb"], *rope_tabs, wts["w_cmp"])
    return {o[0]: v for o, v in zip(outs, res)}


def _mlstm_chunk_kernel(q_ref, k_ref, kt_ref, v_ref, mo_ref, gt_ref, nw_ref,
                        ya_ref, c_out_ref, m_out_ref, caug, mstate):
    t = q_ref.shape[0]
    c = pl.program_id(1)

    @pl.when(c == 0)
    def _():
        caug[...] = jnp.zeros_like(caug)
        mstate[...] = jnp.zeros_like(mstate)

    gc = GATE_CAP * jnp.tanh(gt_ref[...] / GATE_CAP)
    lane = lax.broadcasted_iota(jnp.int32, (8, t), 1)
    b = _log_sigmoid(gc)
    k = 1
    while k < t:
        b = b + jnp.where(lane >= k, pltpu.roll(b, k, 1), 0.0)
        k *= 2

    row = lax.broadcasted_iota(jnp.int32, (t, t), 0)
    col = lax.broadcasted_iota(jnp.int32, (t, t), 1)
    causal = col <= row
    eye = col == row
    lane_v = lax.broadcasted_iota(jnp.int32, (t, M_DV), 1)

    def to_col(r):
        return jnp.sum(jnp.where(eye, r, 0.0), axis=1, keepdims=True)

    for h in range(M_HEADS):
        qs, vs_ = slice(h * M_DQK, (h + 1) * M_DQK), slice(h * M_DV, (h + 1) * M_DV)
        irow = gc[h:h + 1, :]
        brow = b[M_HEADS + h:M_HEADS + h + 1, :]
        bcol = to_col(brow)
        mprev = mstate[h:h + 1, 0:1]
        d = jnp.where(causal, bcol - brow + irow, NEG)
        minter = bcol + mprev
        mt = jnp.maximum(minter, jnp.max(d, axis=1, keepdims=True))
        qh, kh, vh = q_ref[:, qs], k_ref[:, qs], v_ref[:, vs_]
        w = _dot_nt(qh, kh) * jnp.exp(d - mt)
        ainter = jnp.exp(minter - mt)
        ca = caug[h]
        qc = _dot(qh, ca.astype(BF16))
        num = _dot(w.astype(BF16), vh) + ainter * qc[:, 0:M_DV]
        den = jnp.sum(w, axis=1, keepdims=True) + ainter * qc[:, M_DV:M_DV + 1]
        hh = num / jnp.maximum(jnp.abs(den), jnp.exp(-mt))
        y = _rms(hh, nw_ref[:, vs_])
        ya_ref[:, vs_] = (jax.nn.sigmoid(mo_ref[:, vs_]) * y).astype(BF16)
        b_last = brow[:, t - 1:t]
        grow = b_last - brow + irow
        mnew = jnp.maximum(b_last + mprev, jnp.max(grow, axis=1, keepdims=True))
        ain = to_col(jnp.exp(grow - mnew))
        ast = jnp.exp(b_last + mprev - mnew)
        vsa = jnp.concatenate([ain * vh.astype(F32), jnp.where(lane_v == 0, ain, 0.0)], axis=1)
        caug[h] = ast * ca + _dot(kt_ref[qs, :], vsa.astype(BF16))
        mstate[h:h + 1, :] = jnp.broadcast_to(mnew, (1, LANES))

    @pl.when(c == pl.num_programs(1) - 1)
    def _():
        c_out_ref[0] = caug[...]
        m_out_ref[0] = mstate[...]


def _mlstm_prompt(p, norm_w, nb, s):
    t = T_CHUNK
    assert s % t == 0
    nc = s // t
    r = nb * s
    rowc = lambda b, c: (b * nc + c, 0)
    ya, caug, mst = pl.pallas_call(
        _mlstm_chunk_kernel,
        grid=(nb, nc),
        in_specs=[pl.BlockSpec((t, MQK), rowc),
                  pl.BlockSpec((t, MQK), rowc),
                  pl.BlockSpec((MQK, t), lambda b, c: (0, b * nc + c)),
                  pl.BlockSpec((t, MV), rowc),
                  pl.BlockSpec((t, MV), rowc),
                  pl.BlockSpec((8, t), lambda b, c: (0, b * nc + c)),
                  pl.BlockSpec((1, MV), lambda b, c: (0, 0))],
        out_specs=[pl.BlockSpec((t, MV), rowc),
                   pl.BlockSpec((1, M_HEADS, M_DQK, 2 * M_DV), lambda b, c: (b, 0, 0, 0)),
                   pl.BlockSpec((1, 8, LANES), lambda b, c: (b, 0, 0))],
        out_shape=[jax.ShapeDtypeStruct((r, MV), BF16),
                   jax.ShapeDtypeStruct((nb, M_HEADS, M_DQK, 2 * M_DV), F32),
                   jax.ShapeDtypeStruct((nb, 8, LANES), F32)],
        scratch_shapes=[pltpu.VMEM((M_HEADS, M_DQK, 2 * M_DV), F32), pltpu.VMEM((8, LANES), F32)],
        compiler_params=_cparams("parallel", "arbitrary"),
        name="mlstm_prompt",
    )(p["mq"], p["mk"], p["mkt"], p["mv"], p["mo"], p["gt"], norm_w.reshape(1, MV))
    return ya, caug[..., :M_DV], caug[..., M_DV], mst[:, :M_HEADS, 0]


def _mlstm_step_kernel(q_ref, k_ref, v_ref, mo_ref, g_ref, c_ref, n_ref, m_ref, nw_ref,
                       ya_ref, co_ref, no_ref, mo_out_ref):
    bb = q_ref.shape[0]
    q, k, v = q_ref[...], k_ref[...], v_ref[...]
    gc = GATE_CAP * jnp.tanh(g_ref[...] / GATE_CAP)
    lf = _log_sigmoid(gc)
    m0 = m_ref[...]
    eye = lax.broadcasted_iota(jnp.int32, (M_DQK, M_DQK), 0) == lax.broadcasted_iota(jnp.int32, (M_DQK, M_DQK), 1)

    def to_col(r):
        return jnp.sum(jnp.where(eye, r, 0.0), axis=1, keepdims=True)

    for b in range(bb):
        for h in range(M_HEADS):
            qs, vs_ = slice(h * M_DQK, (h + 1) * M_DQK), slice(h * M_DV, (h + 1) * M_DV)
            qh, kh, vh = q[b:b + 1, qs], k[b:b + 1, qs], v[b:b + 1, vs_]
            c0 = c_ref[b, h]
            n0 = n_ref[b, h:h + 1, :]
            ipre = gc[b:b + 1, h:h + 1]
            minter = lf[b:b + 1, M_HEADS + h:M_HEADS + h + 1] + m0[b:b + 1, h:h + 1]
            mt = jnp.maximum(minter, ipre)
            ain = jnp.exp(ipre - mt)
            ast = jnp.exp(minter - mt)
            w = jnp.sum(qh * kh, axis=1, keepdims=True) * ain
            qc = jnp.sum(to_col(qh) * c0, axis=0, keepdims=True)
            num = w * vh + ast * qc
            den = w + ast * jnp.sum(qh * n0, axis=1, keepdims=True)
            hh = num / jnp.maximum(jnp.abs(den), jnp.exp(-mt))
            y = _rms(hh, nw_ref[:, vs_])
            ya_ref[b:b + 1, vs_] = (jax.nn.sigmoid(mo_ref[b:b + 1, vs_]) * y).astype(ya_ref.dtype)
            co_ref[b, h] = ast * c0 + to_col(kh) * (ain * vh)
            no_ref[b, h:h + 1, :] = ast * n0 + ain * kh
            mo_out_ref[b:b + 1, h:h + 1] = mt


def _mlstm_step(p, gcol, c0, n0, m0, norm_w):
    nb = c0.shape[0]
    bb = 8
    row = lambda i: (i, 0)
    return pl.pallas_call(
        _mlstm_step_kernel,
        grid=(nb // bb,),
        in_specs=[pl.BlockSpec((bb, MQK), row), pl.BlockSpec((bb, MQK), row), pl.BlockSpec((bb, MV), row),
                  pl.BlockSpec((bb, MV), row), pl.BlockSpec((bb, 8), row),
                  pl.BlockSpec((bb, M_HEADS, M_DQK, M_DV), lambda i: (i, 0, 0, 0)),
                  pl.BlockSpec((bb, M_HEADS, M_DQK), lambda i: (i, 0, 0)),
                  pl.BlockSpec((bb, M_HEADS), row),
                  pl.BlockSpec((1, MV), lambda i: (0, 0))],
        out_specs=[pl.BlockSpec((bb, MV), row),
                   pl.BlockSpec((bb, M_HEADS, M_DQK, M_DV), lambda i: (i, 0, 0, 0)),
                   pl.BlockSpec((bb, M_HEADS, M_DQK), lambda i: (i, 0, 0)),
                   pl.BlockSpec((bb, M_HEADS), row)],
        out_shape=[jax.ShapeDtypeStruct((nb, MV), F32),
                   jax.ShapeDtypeStruct(c0.shape, F32),
                   jax.ShapeDtypeStruct(n0.shape, F32),
                   jax.ShapeDtypeStruct(m0.shape, F32)],
        compiler_params=_cparams("parallel"),
        name="mlstm_step",
    )(p["mq"].astype(F32), p["mk"].astype(F32), p["mv"].astype(F32), p["mo"], gcol, c0, n0, m0,
      norm_w.reshape(1, MV))


def _nsa_prompt_kernel(q_ref, kc_ref, vct_ref, ks_ref, vst_ref, kw_ref, vwt_ref, ng_ref, wb_ref, o_ref,
                       bias_ref, s_ref, *, tks):
    tq = q_ref.shape[1]
    r4 = HPG * tq
    gw = HPG * HEAD_DIM
    tkv = vst_ref.shape[-1]
    ncmp = kc_ref.shape[1]
    nsel = ncmp // 2
    groups = range(N_KV)
    i = pl.program_id(1)
    t0 = i * tq
    qs = [q_ref[g * HPG:(g + 1) * HPG].reshape(r4, HEAD_DIM) for g in groups]
    lane_t = t0 + (lax.broadcasted_iota(jnp.int32, (1, r4), 1) & (tq - 1))

    ones_rows = jnp.where(lax.broadcasted_iota(jnp.int32, (ONES_ROWS, tkv), 0) == 0, 1.0, 0.0).astype(BF16)

    def with_ones(vt):
        return jnp.concatenate([vt, ones_rows], axis=0)

    nwv = (WINDOW + tq) // tkv
    w0 = jnp.maximum(t0 - WINDOW, 0) // tkv
    sws = [_dot_nt(kw_ref[g, pl.ds(pl.multiple_of(w0 * tkv, tkv), nwv * tkv), :], qs[g]) + wb_ref[...]
           for g in groups]
    pws = [jnp.exp2(sw - jnp.max(sw, axis=0, keepdims=True)).astype(BF16) for sw in sws]
    accws = []
    for g in groups:
        accw = _dot(with_ones(vwt_ref[w0, g]), pws[g][0:tkv])
        for u in range(1, nwv):
            accw = accw + _dot(with_ones(vwt_ref[w0 + u, g]), pws[g][u * tkv:(u + 1) * tkv])
        accws.append(accw)

    rr = lax.broadcasted_iota(jnp.int32, (ncmp, 1), 0)
    jc = jnp.where(rr < nsel, 2 * rr, 2 * (rr - nsel) + 1)
    valid = ((jc + 1) * CMP_BLOCK - 1) <= lane_t
    jb = lax.broadcasted_iota(jnp.int32, (nsel, tq), 0)
    tt = t0 + lax.broadcasted_iota(jnp.int32, (nsel, tq), 1)
    forced = (jb == (tt >> 6)) | (jb == 0)
    future = jb * SEL_BLOCK > tt
    jf = jb.astype(F32)
    ocs, keys = [], []
    for g in groups:
        sm = jnp.where(valid, _dot_nt(kc_ref[g], qs[g]), NEG)
        e = jnp.exp2(sm - jnp.max(sm, axis=0, keepdims=True))
        pc = e * jnp.where(lane_t >= CMP_BLOCK - 1, 1.0 / jnp.sum(e, axis=0, keepdims=True), 0.0)
        ocs.append(_dot(vct_ref[g], pc.astype(BF16)))
        pp = pc[0:nsel] + pc[nsel:ncmp]
        imp = pp[:, 0:tq]
        for h in range(1, HPG):
            imp = imp + pp[:, h * tq:(h + 1) * tq]
        keys.append(jnp.where(forced, IMP_FORCED, jnp.where(future, IMP_FUTURE, imp)))

    biases = [jnp.full((nsel, tq), NEG, F32) for _ in groups]
    for _ in range(min(TOP_K, nsel)):
        for g in groups:
            cur = jnp.max(keys[g], axis=0, keepdims=True)
            first = jnp.min(jnp.where(keys[g] == cur, jf, float(nsel)), axis=0, keepdims=True)
            pick = jf == first
            biases[g] = jnp.where(pick, 0.0, biases[g])
            keys[g] = jnp.where(pick, IMP_TAKEN, keys[g])
    bpq = tq // SEL_BLOCK
    own = [(jb == bpq * i + u) for u in range(bpq)]
    own_bias = []
    for g in groups:
        bg = jnp.where(future, NEG, biases[g])
        own_bias.append([jnp.max(jnp.where(o, bg, NEG), axis=0, keepdims=True) for o in own])
        for o in own:
            bg = jnp.where(o, NEG, bg)
        bias4 = jnp.concatenate([bg] * HPG, axis=1)
        for j in range(nsel):
            bias_ref[g, j] = jnp.broadcast_to(bias4[j:j + 1, :], (8, r4))

    bpv = tkv // SEL_BLOCK
    vpt = tks // tkv
    nbt = vpt * bpv
    sub = SEL_BLOCK // 8

    def sel_scores(slot, kt):
        k0 = pl.multiple_of(kt * tks, tks)
        for g in groups:
            s = _dot_nt(ks_ref[g, pl.ds(k0, tks), :], qs[g])
            s_ref[slot, g] = (s.reshape(nbt, sub, 8, r4) + bias_ref[g, pl.ds(kt * nbt, nbt)][:, None]).reshape(tks, r4)

    def sel_update(state, ss, v0):
        mid = []
        for g in groups:
            m, acc = state[g]
            mn = jnp.maximum(m, jnp.max(ss[g], axis=0, keepdims=True))
            alpha = jnp.exp2(m - mn)
            mid.append((mn, alpha * acc, jnp.exp2(ss[g] - mn).astype(BF16)))
        out = []
        for g in groups:
            mn, acc, pb = mid[g]
            nv = ss[g].shape[0] // tkv
            step = 2 if nv % 2 == 0 else 1
            for u in range(0, nv, step):
                vt = jnp.concatenate([with_ones(vst_ref[v0 + u + w, g]) for w in range(step)], axis=1)
                acc = acc + _dot(vt, pb[u * tkv:(u + step) * tkv])
            out.append((mn, acc))
        return tuple(out)

    init = tuple((jnp.full((1, r4), NEG, F32), jnp.zeros((HEAD_DIM + ONES_ROWS, r4), F32)) for _ in groups)
    kd = t0 // tks

    def from_slot(slot):
        return tuple(s_ref[slot, g] for g in groups)

    def sel_pair(j, state):
        sel_scores(1, jnp.minimum(2 * j + 1, kd))
        state = sel_update(state, from_slot(0), 2 * j * vpt)
        sel_scores(0, jnp.minimum(2 * j + 2, kd))
        return sel_update(state, from_slot(1), (2 * j + 1) * vpt)

    sel_scores(0, 0)
    state = lax.fori_loop(0, (kd + 1) // 2, sel_pair, init)
    state = lax.cond(kd % 2 == 0, lambda st: sel_update(st, from_slot(0), kd * vpt), lambda st: st, state)
    kl = lax.broadcasted_iota(jnp.int32, (tq, 1), 0)
    tri = kl <= (lane_t - t0)
    s_own = []
    for g in groups:
        s = _dot_nt(ks_ref[g, pl.ds(pl.multiple_of(t0, tq), tq), :], qs[g])
        ob = jnp.concatenate([jnp.broadcast_to(jnp.concatenate([b] * HPG, axis=1), (SEL_BLOCK, r4))
                              for b in own_bias[g]], axis=0)
        s_own.append(jnp.where(tri, s + ob, NEG))
    sel = sel_update(state, tuple(s_own), i * (tq // tkv))

    ng = jax.nn.sigmoid(ng_ref[...])
    for g in groups:
        accw = accws[g]

        def gate(br):
            return jnp.concatenate([ng[g, h * 3 + br:h * 3 + br + 1, :] for h in range(HPG)], axis=1)

        accs = sel[g][1]
        ls, lw = accs[HEAD_DIM:HEAD_DIM + 1], accw[HEAD_DIM:HEAD_DIM + 1]
        out_t = (gate(0) * ocs[g] + (gate(1) / ls) * accs[0:HEAD_DIM]
                 + (gate(2) / lw) * accw[0:HEAD_DIM])
        stacked = jnp.concatenate([out_t[:, h * tq:(h + 1) * tq] for h in range(HPG)], axis=0)
        o_ref[:, g * gw:(g + 1) * gw] = stacked.T.astype(o_ref.dtype)


def _window_bias(tq):
    nvar = WINDOW // tq + 1
    v = jnp.arange(nvar)[:, None, None]
    u = jnp.arange(WINDOW + tq)[None, :, None]
    tt = jnp.arange(tq)[None, None, :]
    dpos = v * tq + tt - u
    ok = (dpos >= 0) & (dpos < WINDOW)
    return jnp.tile(jnp.where(ok, 0.0, NEG).astype(F32), (1, 1, HPG))


def _nsa_prompt(p, kc, vct, nb, s):
    tq = TQ
    tks = min(TKS, s)
    assert s % tks == 0 and tq == TKV and tq & (tq - 1) == 0 and tks % TKV == 0 and s >= WINDOW + tq
    nqb = s // tq
    r = nb * s
    ncmp = s // CMP_BLOCK
    gw = HPG * HEAD_DIM
    vst = p["vst"].reshape(r // TKV, N_KV, HEAD_DIM, TKV)
    vwt = p["vwt"].reshape(r // TKV, N_KV, HEAD_DIM, TKV)
    nvar = WINDOW // tq
    return pl.pallas_call(
        functools.partial(_nsa_prompt_kernel, tks=tks),
        grid=(nb, nqb),
        in_specs=[pl.BlockSpec((N_HEADS, tq, HEAD_DIM), lambda b, i: (0, b * nqb + i, 0)),
                  pl.BlockSpec((None, N_KV, ncmp, HEAD_DIM), lambda b, i: (b, 0, 0, 0)),
                  pl.BlockSpec((None, N_KV, HEAD_DIM, ncmp), lambda b, i: (b, 0, 0, 0)),
                  pl.BlockSpec((N_KV, s, HEAD_DIM), lambda b, i: (0, b, 0)),
                  pl.BlockSpec((s // TKV, N_KV, HEAD_DIM, TKV), lambda b, i: (b, 0, 0, 0)),
                  pl.BlockSpec((N_KV, s, HEAD_DIM), lambda b, i: (0, b, 0)),
                  pl.BlockSpec((s // TKV, N_KV, HEAD_DIM, TKV), lambda b, i: (b, 0, 0, 0)),
                  pl.BlockSpec((N_KV, 16, tq), lambda b, i: (0, 0, b * nqb + i)),
                  pl.BlockSpec((None, WINDOW + tq, HPG * tq), lambda b, i: (jnp.minimum(i, nvar), 0, 0))],
        out_specs=pl.BlockSpec((tq, NQ), lambda b, i: (b * nqb + i, 0)),
        out_shape=jax.ShapeDtypeStruct((r, NQ), BF16),
        scratch_shapes=[pltpu.VMEM((N_KV, s // SEL_BLOCK, 8, HPG * tq), F32),
                        pltpu.VMEM((2, N_KV, tks, HPG * tq), F32)],
        compiler_params=_cparams("parallel", "arbitrary"),
        name="nsa_prompt",
    )(p["nq"], kc, vct, p["ks"], vst, p["kw"], vwt, p["ngt"], _window_bias(tq))


def _prompt_cmp_operands(kcvc, nb, s):
    ncmp = s // CMP_BLOCK
    a = kcvc.reshape(nb, ncmp // 2, 2, 2, N_KV, HEAD_DIM)
    a = jnp.transpose(a, (0, 3, 4, 2, 1, 5)).reshape(nb, 2, N_KV, ncmp, HEAD_DIM)
    kc = a[:, 0].astype(BF16)
    vct = jnp.swapaxes(a[:, 1], -1, -2).astype(BF16)
    return kc, vct


def _by_group(x0, x1):
    return jnp.where(lax.broadcasted_iota(jnp.int32, x0.shape, 0) < HPG, x0, x1)


def _nsa_dec_cmp_kernel(pt_ref, *refs, pos):
    del pt_ref
    pages = refs[:PAGES_PER_STEP]
    wt_ref, seg_ref, q_ref, oc_ref, sel_ref, kcvc = refs[PAGES_PER_STEP:]
    pc = pl.program_id(1)

    @pl.when(pc == 0)
    def _():
        kcvc[...] = jnp.zeros_like(kcvc)

    x = jnp.concatenate([pg[...] for pg in pages], axis=1)
    kcvc[...] += _dot((x * wt_ref[...]).astype(BF16), seg_ref[...])

    @pl.when(pc == pl.num_programs(1) - 1)
    def _():
        ncmp = kcvc.shape[1]
        half = ncmp // 2
        qb = q_ref[...].astype(BF16)
        kv = kcvc[...].astype(BF16)
        kt = [kv[g * HEAD_DIM:(g + 1) * HEAD_DIM] for g in range(N_KV)]
        vt = [kv[NKV + g * HEAD_DIM:NKV + (g + 1) * HEAD_DIM] for g in range(N_KV)]
        s = _by_group(_dot(qb, kt[0]), _dot(qb, kt[1]))
        cc = lax.broadcasted_iota(jnp.int32, (1, ncmp), 1)
        jc = jnp.where(cc < half, 2 * cc, 2 * (cc - half) + 1)
        valid = ((jc + 1) * CMP_BLOCK - 1) <= pos
        sm = jnp.where(valid, s, NEG)
        e = jnp.exp2(sm - jnp.max(sm, axis=1, keepdims=True))
        p = jnp.where(valid, e / jnp.sum(e, axis=1, keepdims=True), 0.0)
        pb = p.astype(BF16)
        oc_ref[...] = _by_group(_dot_nt(pb, vt[0]), _dot_nt(pb, vt[1]))
        pp = p[:, 0:half] + p[:, half:ncmp]
        jl = lax.broadcasted_iota(jnp.int32, (1, half), 1)
        ii = lax.broadcasted_iota(jnp.int32, (half, half), 0)
        jj = lax.broadcasted_iota(jnp.int32, (half, half), 1)
        for g in range(N_KV):
            imp = jnp.sum(pp[g * HPG:(g + 1) * HPG], axis=0, keepdims=True)
            forced = (jl == pos // SEL_BLOCK) | (jl == 0)
            key = jnp.where(forced, IMP_FORCED, jnp.where(jl * SEL_BLOCK > pos, IMP_FUTURE, imp))
            kcol = jnp.sum(jnp.where(ii == jj, key, 0.0), axis=1, keepdims=True)
            ahead = (kcol > key) | ((kcol == key) & (ii < jj))
            rank = jnp.sum(jnp.where(ahead, 1.0, 0.0), axis=0, keepdims=True)
            rr = lax.broadcasted_iota(jnp.int32, (TOP_K, half), 0).astype(F32)
            blk = jnp.sum(jnp.where(rank == rr, jl.astype(F32), 0.0), axis=1, keepdims=True)
            sel_ref[g] = jnp.broadcast_to(blk, (TOP_K, LANES))


def _page_specs():
    def spec(kk):
        return pl.BlockSpec((None, ROW_W, PAGE_SIZE), lambda b, pc, pt: (pt[b, pc * PAGES_PER_STEP + kk], 0, 0))
    return [spec(kk) for kk in range(PAGES_PER_STEP)]


def _nsa_dec_cmp(cache_t, page_table, q, wt, seg):
    nb, n_pages = page_table.shape
    assert n_pages % PAGES_PER_STEP == 0
    npc = n_pages // PAGES_PER_STEP
    past = n_pages * PAGE_SIZE
    ncmp = past // CMP_BLOCK
    nselp = ncmp // 2
    keys = PAGES_PER_STEP * PAGE_SIZE
    perb = lambda b, pc, pt: (b, 0, 0)
    grid_spec = pltpu.PrefetchScalarGridSpec(
        num_scalar_prefetch=1,
        grid=(nb, npc),
        in_specs=_page_specs() + [
            pl.BlockSpec((ROW_W, keys), lambda b, pc, pt: (0, 0)),
            pl.BlockSpec((None, keys, ncmp), lambda b, pc, pt: (pc, 0, 0)),
            pl.BlockSpec((None, N_HEADS, HEAD_DIM), perb)],
        out_specs=[pl.BlockSpec((None, N_HEADS, HEAD_DIM), perb),
                   pl.BlockSpec((None, N_KV, TOP_K, LANES), lambda b, pc, pt: (b, 0, 0, 0))],
        scratch_shapes=[pltpu.VMEM((ROW_W, ncmp), F32)],
    )
    del nselp
    return pl.pallas_call(
        functools.partial(_nsa_dec_cmp_kernel, pos=past),
        grid_spec=grid_spec,
        out_shape=[jax.ShapeDtypeStruct((nb, N_HEADS, HEAD_DIM), F32),
                   jax.ShapeDtypeStruct((nb, N_KV, TOP_K, LANES), F32)],
        compiler_params=_cparams("parallel", "arbitrary"),
        name="nsa_decode_cmp",
    )(page_table, *([cache_t] * PAGES_PER_STEP), wt, seg, q)


def _nsa_dec_sel_kernel(pt_ref, blk_ref, *refs, pos, wb, k_past):
    del pt_ref
    nsel = N_KV * k_past
    pages = refs[:nsel]
    (q_ref, oc_ref, ng_ref, nslc_ref, nwin_ref, nwcol_ref, win_ref, o_ref, wout_ref) = refs[nsel:]
    b = pl.program_id(0)
    q = q_ref[...]
    qb = q.astype(BF16)
    ksl = [slice(g * HEAD_DIM, (g + 1) * HEAD_DIM) for g in range(N_KV)]
    vsl = [slice(NKV + g * HEAD_DIM, NKV + (g + 1) * HEAD_DIM) for g in range(N_KV)]
    half_of_lane = lax.broadcasted_iota(jnp.int32, (1, PAGE_SIZE), 1) // SEL_BLOCK
    bpp = PAGE_SIZE // SEL_BLOCK
    ss, vt = [], []
    for g in range(N_KV):
        own = range(g * k_past, (g + 1) * k_past)
        kt = jnp.concatenate([pages[r][g] for r in own], axis=1).astype(BF16)
        vt.append(jnp.concatenate([pages[r][N_KV + g] for r in own], axis=1).astype(BF16))
        keep = jnp.concatenate([jnp.where(half_of_lane == blk_ref[b, r] % bpp, 0.0, NEG) for r in own], axis=1)
        ss.append(_dot(qb, kt) + keep)
    s_sel = _by_group(ss[0], ss[1])

    def new_row(row):
        sn = _by_group(jnp.sum(q * row[:, ksl[0]], axis=1, keepdims=True),
                       jnp.sum(q * row[:, ksl[1]], axis=1, keepdims=True))
        return sn, _by_group(jnp.broadcast_to(row[:, vsl[0]], q.shape), jnp.broadcast_to(row[:, vsl[1]], q.shape))

    def attend(s, sn, vn, vts_):
        m = jnp.maximum(jnp.max(s, axis=1, keepdims=True), sn)
        p = jnp.exp2(s - m)
        pn = jnp.exp2(sn - m)
        pb = p.astype(BF16)
        num = _by_group(_dot_nt(pb, vts_[0]), _dot_nt(pb, vts_[1])) + pn * vn
        return num / (jnp.sum(p, axis=1, keepdims=True) + pn)

    o_sel = attend(s_sel, *new_row(nslc_ref[...]), vt)
    win = win_ref[...]
    wbf = win.astype(BF16)
    sw = _by_group(_dot(qb, wbf[ksl[0]]), _dot(qb, wbf[ksl[1]]))
    dpos = pos - (pos - wb + lax.broadcasted_iota(jnp.int32, (1, wb), 1))
    okw = (dpos >= 0) & (dpos < WINDOW) & (pos - dpos >= 0)
    o_win = attend(jnp.where(okw, sw, NEG), *new_row(nwin_ref[...]), [wbf[vsl[0]], wbf[vsl[1]]])
    gates = jax.nn.sigmoid(ng_ref[...])
    o_ref[...] = gates[:, 0:1] * oc_ref[...] + gates[:, 1:2] * o_sel + gates[:, 2:3] * o_win
    lid = lax.broadcasted_iota(jnp.int32, (1, wb), 1)
    wout_ref[...] = jnp.where(lid == wb - 1, nwcol_ref[...], pltpu.roll(win, wb - 1, 1))


def _nsa_dec_sel(cache_t, page_table, blocks, q, oc, ng, nslc, nwin, cache_win_t):
    nb, n_pages = page_table.shape
    past = n_pages * PAGE_SIZE
    wb = cache_win_t.shape[-1]
    assert wb == WINDOW and past >= WINDOW
    k_past = blocks.shape[1] // N_KV
    bpp = PAGE_SIZE // SEL_BLOCK
    cache4 = cache_t.reshape(cache_t.shape[0], 2 * N_KV, HEAD_DIM, PAGE_SIZE)
    perb = lambda b, pt, bl: (b, 0, 0)

    n_pool = cache_t.shape[0]

    def tile_spec(g, r):
        def index(b, pt, bl):
            col = jnp.clip(bl[b, g * k_past + r] // bpp, 0, n_pages - 1)
            return (jnp.clip(pt[b, col], 0, n_pool - 1), 0, 0, 0)
        return pl.BlockSpec((None, 2 * N_KV, HEAD_DIM, PAGE_SIZE), index)

    tiles = [tile_spec(g, r) for g in range(N_KV) for r in range(k_past)]
    grid_spec = pltpu.PrefetchScalarGridSpec(
        num_scalar_prefetch=2,
        grid=(nb,),
        in_specs=tiles + [
            pl.BlockSpec((None, N_HEADS, HEAD_DIM), perb),
            pl.BlockSpec((None, N_HEADS, HEAD_DIM), perb),
            pl.BlockSpec((None, N_HEADS, LANES), perb),
            pl.BlockSpec((None, 1, ROW_W), perb),
            pl.BlockSpec((None, 1, ROW_W), perb),
            pl.BlockSpec((None, ROW_W, 1), perb),
            pl.BlockSpec((None, ROW_W, wb), perb)],
        out_specs=[pl.BlockSpec((None, N_HEADS, HEAD_DIM), perb),
                   pl.BlockSpec((None, ROW_W, wb), perb)],
    )
    return pl.pallas_call(
        functools.partial(_nsa_dec_sel_kernel, pos=past, wb=wb, k_past=k_past),
        grid_spec=grid_spec,
        out_shape=[jax.ShapeDtypeStruct((nb, N_HEADS, HEAD_DIM), F32),
                   jax.ShapeDtypeStruct((nb, ROW_W, wb), F32)],
        compiler_params=_cparams("parallel"),
        name="nsa_decode_sel",
    )(page_table, blocks, *([cache4] * len(tiles)), q, oc, ng, nslc, nwin, nwin.reshape(nb, ROW_W, 1),
      cache_win_t)


def _merge_kernel(x_ref, ya_ref, yb_ref, gab_ref, ga_ref, wa_ref, wb_ref, wo_ref, g_ref, o_ref):
    d = x_ref.shape[1]
    pa = _dot(ya_ref[...], wa_ref[...])
    pb = _dot(yb_ref[...], wb_ref[...])
    merged = jax.nn.sigmoid(gab_ref[:, 0:d]) * pa + jax.nn.sigmoid(gab_ref[:, d:2 * d]) * pb
    z = _dot(merged.astype(BF16), wo_ref[...])
    o_ref[...] = x_ref[...] + ga_ref[0] * _rms(z, g_ref[...])


def _merge(x, ya, yb, gab, ga1, w_b, wts, g_post, tm, rows_per_mod):
    r, d = x.shape
    wyb = yb.shape[1]
    tpm = rows_per_mod // tm
    mr = ga1.shape[1]
    row = lambda i: (i, 0)
    const2 = lambda i: (0, 0)
    return pl.pallas_call(
        _merge_kernel,
        grid=(r // tm,),
        in_specs=[pl.BlockSpec((tm, d), row), pl.BlockSpec((tm, MV), row), pl.BlockSpec((tm, wyb), row),
                  pl.BlockSpec((tm, 2 * d), row),
                  pl.BlockSpec((1, mr, d), lambda i: (i // tpm, 0, 0)),
                  pl.BlockSpec((MV, d), const2), pl.BlockSpec((wyb, d), const2), pl.BlockSpec((d, d), const2),
                  pl.BlockSpec((1, d), const2)],
        out_specs=pl.BlockSpec((tm, d), row),
        out_shape=jax.ShapeDtypeStruct((r, d), F32),
        compiler_params=_cparams("parallel"),
        name="merge",
    )(x, ya, yb, gab, ga1, wts["w_a"], w_b, wts["w_o"], g_post.reshape(1, d))


def _ffn_kernel(x_ref, sc_ref, sh_ref, ga_ref, g1_ref, g2_ref, wu_ref, wd_ref, o_ref):
    dff = wd_ref.shape[0]
    x = x_ref[...]
    hb = (_rms(x, g1_ref[...]) * (1.0 + sc_ref[0]) + sh_ref[0]).astype(BF16)
    acc = jnp.zeros(x.shape, F32)
    for c in range(dff // FFN_CK):
        lo, hi = c * FFN_CK, (c + 1) * FFN_CK
        gate = _dot(hb, wu_ref[:, lo:hi])
        up = _dot(hb, wu_ref[:, dff + lo:dff + hi])
        acc = acc + _dot((gate * jax.nn.sigmoid(gate) * up).astype(BF16), wd_ref[lo:hi, :])
    o_ref[...] = x + ga_ref[0] * _rms(acc, g2_ref[...])


def _ffn(x, sc, sh, ga2, g_pre, g_post, wts, tm, rows_per_mod):
    r, d = x.shape
    dff = wts["w_down"].shape[0]
    assert dff % FFN_CK == 0
    tpm = rows_per_mod // tm
    mr = sc.shape[1]
    row = lambda i: (i, 0)
    const2 = lambda i: (0, 0)
    mod = pl.BlockSpec((1, mr, d), lambda i: (i // tpm, 0, 0))
    return pl.pallas_call(
        _ffn_kernel,
        grid=(r // tm,),
        in_specs=[pl.BlockSpec((tm, d), row), mod, mod, mod,
                  pl.BlockSpec((1, d), const2), pl.BlockSpec((1, d), const2),
                  pl.BlockSpec((d, 2 * dff), const2), pl.BlockSpec((dff, d), const2)],
        out_specs=pl.BlockSpec((tm, d), row),
        out_shape=jax.ShapeDtypeStruct((r, d), F32),
        compiler_params=_cparams("parallel"),
        name="ffn",
    )(x, sc, sh, ga2, g_pre.reshape(1, d), g_post.reshape(1, d), wts["w_up"], wts["w_down"])


def _prep_weights(w_in, b_in, w_cmp_k, w_cmp_v, w_proj_a, w_proj_b, w_out, w_up, w_down):
    o_mi, o_nq, o_ng, o_ga = 4 * 256 + 2 * 256, 1544, 2824, 2848
    o_nkv = o_nq + NQ
    main_cols = [slice(0, o_mi), slice(o_nq, o_ng), slice(o_ga, w_in.shape[1])]
    w_main = jnp.concatenate([w_in[:, s] for s in main_cols], axis=1).astype(BF16)
    b_main = jnp.concatenate([b_in[s] for s in main_cols]).reshape(1, -1)
    z = lambda n: jnp.zeros((w_in.shape[0], n), w_in.dtype)
    ng0, ng1 = slice(o_ng, o_ng + 12), slice(o_ng + 12, o_ng + 24)
    tg_cols = [w_in[:, o_mi:o_mi + 8], z(8), w_in[:, ng0], z(4), w_in[:, ng1], z(4)]
    tg_b = [b_in[o_mi:o_mi + 8], jnp.zeros(8), b_in[ng0], jnp.zeros(4), b_in[ng1], jnp.zeros(4)]
    vs_, vw_ = slice(o_nkv + 3 * NKV, o_nkv + 4 * NKV), slice(o_nkv + 5 * NKV, o_nkv + 6 * NKV)
    nkv_part = lambda j: slice(o_nkv + j * NKV, o_nkv + (j + 1) * NKV)
    tb_parts = [slice(256, 512), vs_, vw_, nkv_part(0), nkv_part(1), nkv_part(2), nkv_part(4)]
    tb_cols = [w_in[:, s] for s in tb_parts]
    tb_b = [b_in[s] for s in tb_parts]
    w_cmp = jnp.concatenate([w_cmp_k, w_cmp_k, w_cmp_v, w_cmp_v], axis=1)
    return {
        "w_main": w_main, "b_main": b_main,
        "w_tg": jnp.concatenate(tg_cols, axis=1).T.astype(BF16),
        "b_tg": jnp.concatenate(tg_b).reshape(-1, 1).astype(F32),
        "w_tb": jnp.concatenate(tb_cols, axis=1).T.astype(BF16),
        "b_tb": jnp.concatenate(tb_b).reshape(-1, 1),
        "w_cmp": w_cmp,
        "w_a": w_proj_a.astype(BF16), "w_b": w_proj_b.astype(BF16), "w_o": w_out.astype(BF16),
        "w_up": w_up.astype(BF16), "w_down": w_down.astype(BF16),
    }


def _rope_tables(pos):
    half = ROPE_DIM // 2
    n = pos.shape[0]
    inv = ROPE_THETA ** (-jnp.arange(half, dtype=F32) * 2.0 / ROPE_DIM)
    ang = pos.astype(F32)[:, None] * inv[None, :]
    cos, sin = jnp.cos(ang), jnp.sin(ang)
    rest = HEAD_DIM - ROPE_DIM
    zh, zr = jnp.zeros((n, half), F32), jnp.zeros((n, rest), F32)
    a = jnp.concatenate([cos, cos, jnp.ones((n, rest), F32)], axis=1)
    p = jnp.concatenate([zh, sin, zr], axis=1)
    m = jnp.concatenate([-sin, zh, zr], axis=1)
    return tuple(jnp.tile(t, (1, LANES // HEAD_DIM)) for t in (a, p, m)) + (cos.T, sin.T)


def _rows_on_lanes(cache):
    n, rows = cache.shape[0], cache.shape[1]
    return jnp.transpose(cache, (0, 2, 3, 4, 1)).reshape(n, ROW_W, rows)


def _cmp_step_operands(w_cmp, n_pages):
    npc = n_pages // PAGES_PER_STEP
    keys = PAGES_PER_STEP * PAGE_SIZE
    ncmp = n_pages * PAGE_SIZE // CMP_BLOCK
    wt = jnp.tile(w_cmp.T, (1, keys // CMP_BLOCK))
    blk = (jnp.arange(npc)[:, None] * keys + jnp.arange(keys)[None, :]) // CMP_BLOCK
    col = (blk % 2) * (ncmp // 2) + blk // 2
    seg = (col[:, :, None] == jnp.arange(ncmp)[None, None, :]).astype(BF16)
    return wt, seg


def _layer_prompt(x, mod, wts, norms, mlstm_norm_w):
    nb, s, d = x.shape
    r = nb * s
    sh1, sc1, ga1, sh2, sc2, ga2 = [m.reshape(nb, 1, d) for m in jnp.split(mod, 6, axis=-1)]
    g_pre_mix, g_post_mix, g_pre_ffn, g_post_ffn = norms
    tm = TM_PROJ
    assert s % tm == 0
    tabs = _rope_tables(jnp.arange(s, dtype=jnp.int32))
    x2 = x.reshape(r, d)
    p = _in_proj(x2, g_pre_mix, sc1, sh1, wts, tabs, tm, s, True)
    ya, c_new, n_new, m_new = _mlstm_prompt(p, mlstm_norm_w, nb, s)
    kc, vct = _prompt_cmp_operands(p["kcvc"], nb, s)
    yb = _nsa_prompt(p, kc, vct, nb, s)
    x1 = _merge(x2, ya, yb, p["gab"], ga1, wts["w_b"], wts, g_post_mix, tm, s)
    y = _ffn(x1, sc2, sh2, ga2, g_pre_ffn, g_post_ffn, wts, tm, s)
    rows = lambda a: jnp.transpose(a.reshape(nb, 2, N_KV, HEAD_DIM, -1), (0, 4, 1, 2, 3))
    win = rows(p["win"][:, :, s - min(WINDOW, s):])
    return y.reshape(nb, s, d), (rows(p["cmp"]), rows(p["slc"]), win, c_new, n_new, m_new)


def _layer_sample(x, mod, wts, norms, mlstm_norm_w, cache_cmp, cache_slc, cache_win, page_table, c0, n0, m0):
    nb, ts, d = x.shape
    assert ts == 1
    sh1, sc1, ga1, sh2, sc2, ga2 = [m.reshape(1, nb, d) for m in jnp.split(mod, 6, axis=-1)]
    g_pre_mix, g_post_mix, g_pre_ffn, g_post_ffn = norms
    n_pages = page_table.shape[1]
    past = n_pages * PAGE_SIZE
    tm = nb
    tabs = _rope_tables(jnp.full((nb,), past, jnp.int32))
    x2 = x.reshape(nb, d)
    p = _in_proj(x2, g_pre_mix, sc1, sh1, wts, tabs, tm, nb, False)
    ya, c_new, n_new, m_new = _mlstm_step(p, p["gt"].T, c0, n0, m0, mlstm_norm_w)
    qs = jnp.transpose(p["nq"], (1, 0, 2)).astype(F32)
    ng = jnp.transpose(p["ngt"][:, :HPG * 3], (2, 0, 1)).reshape(nb, N_HEADS, 3)
    ng = jnp.pad(ng, ((0, 0), (0, 0), (0, LANES - 3)))
    wt, seg = _cmp_step_operands(wts["w_cmp"], n_pages)
    oc, ranked = _nsa_dec_cmp(_rows_on_lanes(cache_cmp), page_table, qs, wt, seg)
    k_past = min(TOP_K, past // SEL_BLOCK + 1) - 1
    blocks = ranked[:, :, :k_past, 0].astype(jnp.int32).reshape(nb, N_KV * k_past)
    o, win_new = _nsa_dec_sel(_rows_on_lanes(cache_slc), page_table, blocks, qs, oc, ng,
                              p["slc"].reshape(nb, 1, ROW_W), p["win"].reshape(nb, 1, ROW_W),
                              _rows_on_lanes(cache_win))
    yb = o.reshape(nb, NQ).astype(BF16)
    x1 = _merge(x2, ya.astype(BF16), yb, p["gab"], ga1, wts["w_b"], wts, g_post_mix, tm, nb)
    y = _ffn(x1, sc2, sh2, ga2, g_pre_ffn, g_post_ffn, wts, tm, nb)
    rows = lambda a: a.reshape(nb, -1, 2, N_KV, HEAD_DIM)
    win_rows = jnp.transpose(win_new.reshape(nb, 2, N_KV, HEAD_DIM, -1), (0, 4, 1, 2, 3))
    return y.reshape(nb, 1, d), (rows(p["cmp"]), rows(p["slc"]), win_rows, c_new, n_new, m_new)


def kernel(x_prompt, x_sample, c_prompt, c_sample, cache_cmp_kv, cache_slc_kv, cache_win_kv, state_mlstm_C, state_mlstm_n, state_mlstm_m, page_table, g_pre_mix, g_post_mix, g_pre_ffn, g_post_ffn, w_ada, b_ada, w_in, b_in, mlstm_norm_w, w_cmp_k, w_cmp_v, w_proj_a, w_proj_b, w_out, w_up, w_down):
    depth = w_in.shape[0]
    nbp, nbs = x_prompt.shape[0], x_sample.shape[0]
    xp, xs = x_prompt, x_sample
    c_all = jnp.concatenate([c_prompt, c_sample], axis=0)
    pad = (-c_all.shape[0]) % 8
    c_all = jnp.pad(c_all, ((0, pad), (0, 0)))
    st_p = [[] for _ in range(6)]
    st_s = [[] for _ in range(6)]
    for l in range(depth):
        wts = _prep_weights(w_in[l], b_in[l], w_cmp_k[l], w_cmp_v[l], w_proj_a[l], w_proj_b[l], w_out[l],
                            w_up[l], w_down[l])
        norms = (g_pre_mix[l], g_post_mix[l], g_pre_ffn[l], g_post_ffn[l])
        mod = _adaln(c_all, w_ada[l], b_ada[l])
        xp, sp = _layer_prompt(xp, mod[:nbp], wts, norms, mlstm_norm_w[l])
        xs, ss = _layer_sample(xs, mod[nbp:nbp + nbs], wts, norms, mlstm_norm_w[l], cache_cmp_kv[l],
                               cache_slc_kv[l], cache_win_kv[l], page_table, state_mlstm_C[l],
                               state_mlstm_n[l], state_mlstm_m[l])
        for j in range(6):
            st_p[j].append(sp[j])
            st_s[j].append(ss[j])
    outs_p = [jnp.stack(a, axis=0) for a in st_p]
    outs_s = [jnp.stack(a, axis=0) for a in st_s]
    return (xp, xs, *outs_p, *outs_s)
```

```python
import functools

import jax
import jax.numpy as jnp
from jax import lax
from jax.experimental import pallas as pl
from jax.experimental.pallas import tpu as pltpu

F32 = jnp.float32
BF16 = jnp.bfloat16

M_HEADS, M_DQK, M_DV = 4, 64, 128
GATE_CAP = 15.0
N_HEADS, N_KV, HEAD_DIM = 8, 2, 64
HPG = N_HEADS // N_KV
CMP_BLOCK, SEL_BLOCK, TOP_K, WINDOW = 32, 64, 16, 512
ROPE_DIM = HEAD_DIM // 4
ROPE_THETA = 500000.0
PAGE_SIZE = 128
EPS = 1e-6
NEG = -1e30
LOG2E = 1.4426950408889634
MQK, MV = M_HEADS * M_DQK, M_HEADS * M_DV
NQ, NKV = N_HEADS * HEAD_DIM, N_KV * HEAD_DIM
ROW_W = 2 * NKV

LANES = 128
VMEM_LIMIT = 48 * 1024 * 1024

TM_PROJ = 512
T_CHUNK = 256
TQ = 128
TKS = 512
TKV = 128
ONES_ROWS = 16
PAGES_PER_STEP = 16
FFN_CK = 256

IMP_FORCED, IMP_FUTURE, IMP_TAKEN = 1e30, -1e30, -2e30


def _dot(a, b):
    return jnp.dot(a, b, preferred_element_type=F32)


def _dot_nt(a, b):
    return lax.dot_general(a, b, (((1,), (1,)), ((), ())), preferred_element_type=F32)


def _rms(x, g):
    return x * lax.rsqrt(jnp.mean(x * x, axis=-1, keepdims=True) + EPS) * g


def _log_sigmoid(x):
    return jnp.minimum(x, 0.0) - jnp.log1p(jnp.exp(-jnp.abs(x)))


def _cparams(*sem):
    return pltpu.CompilerParams(dimension_semantics=sem, vmem_limit_bytes=VMEM_LIMIT)


def _adaln_kernel(c_ref, w_ref, b_ref, o_ref):
    c = c_ref[...]
    a = (c * jax.nn.sigmoid(c)).astype(BF16)
    o_ref[...] = _dot(a, w_ref[...].astype(BF16)) + b_ref[...]


def _adaln(c, w, b):
    r, d = c.shape
    n = w.shape[1]
    tn = 1536
    return pl.pallas_call(
        _adaln_kernel,
        grid=(n // tn,),
        in_specs=[pl.BlockSpec((r, d), lambda j: (0, 0)),
                  pl.BlockSpec((d, tn), lambda j: (0, j)),
                  pl.BlockSpec((1, tn), lambda j: (0, j))],
        out_specs=pl.BlockSpec((r, tn), lambda j: (0, j)),
        out_shape=jax.ShapeDtypeStruct((r, n), F32),
        compiler_params=_cparams("arbitrary"),
        name="adaln",
    )(c, w, b.reshape(1, n))


_C_MQ, _C_MK, _C_MV, _C_MO, _C_NQ, _C_NKV, _C_GAB, _C_END = 0, 256, 512, 1024, 1536, 2048, 2816, 4864


def _proj_kernel(x_ref, g_ref, sc_ref, sh_ref, wm_ref, bm_ref, wtg_ref, btg_ref, wtb_ref, btb_ref,
                 ra_ref, rp_ref, rm_ref, ct_ref, st_ref, wc_ref,
                 mq_ref, mk_ref, mv_ref, mo_ref, nq_ref, cmp_ref, slc_ref, win_ref, ks_ref, kw_ref,
                 gab_ref, kcvc_ref, gt_ref, ngt_ref, mkt_ref, vst_ref, vwt_ref, *, tks, tkw, rows_t):
    tm = x_ref.shape[0]
    x = x_ref[...]
    h = _rms(x, g_ref[...]) * (1.0 + sc_ref[0]) + sh_ref[0]
    hb = h.astype(BF16)

    def mm(lo, hi):
        return _dot(hb, wm_ref[:, lo:hi]) + bm_ref[:, lo:hi]

    ra, rp, rm = ra_ref[...], rp_ref[...], rm_ref[...]

    def rope(xc):
        return xc * ra + pltpu.roll(xc, 8, 1) * rp + pltpu.roll(xc, LANES - 8, 1) * rm

    scale_m = M_DQK ** -0.5
    scale_n = HEAD_DIM ** -0.5 * LOG2E
    mq_ref[...] = (mm(_C_MQ, _C_MK) * scale_m).astype(BF16)
    mk_ref[...] = mm(_C_MK, _C_MV).astype(BF16)
    mv_ref[...] = mm(_C_MV, _C_MO).astype(BF16)
    mo_ref[...] = mm(_C_MO, _C_NQ)

    nq = mm(_C_NQ, _C_NKV)
    for c in range(NQ // LANES):
        r = (rope(nq[:, c * LANES:(c + 1) * LANES]) * scale_n).astype(BF16)
        nq_ref[2 * c] = r[:, :HEAD_DIM]
        nq_ref[2 * c + 1] = r[:, HEAD_DIM:]

    nkv = mm(_C_NKV, _C_GAB)
    kc = rope(nkv[:, 0:128])
    vc = nkv[:, 128:256]
    ksr = rope(nkv[:, 256:384])
    kwr = rope(nkv[:, 512:640])
    if not rows_t:
        cmp_ref[:, 0:NKV] = kc
        cmp_ref[:, NKV:ROW_W] = vc
        slc_ref[:, 0:NKV] = ksr
        slc_ref[:, NKV:ROW_W] = nkv[:, 384:512]
        win_ref[:, 0:NKV] = kwr
        win_ref[:, NKV:ROW_W] = nkv[:, 640:768]
    for g in range(N_KV):
        ks_ref[g] = ksr[:, g * HEAD_DIM:(g + 1) * HEAD_DIM].astype(BF16)
        kw_ref[g] = kwr[:, g * HEAD_DIM:(g + 1) * HEAD_DIM].astype(BF16)

    wc = wc_ref[...]
    nb = tm // CMP_BLOCK
    kcvc_ref[:, 0:NKV] = jnp.sum(kc.reshape(nb, CMP_BLOCK, NKV) * wc[None, :, 0:NKV], axis=1)
    kcvc_ref[:, NKV:ROW_W] = jnp.sum(vc.reshape(nb, CMP_BLOCK, NKV) * wc[None, :, NKV:ROW_W], axis=1)

    gab_ref[...] = mm(_C_GAB, _C_END)

    tg = _dot_nt(wtg_ref[...], hb) + btg_ref[...]
    gt_ref[...] = tg[0:8]
    ngt_ref[0] = tg[16:32]
    ngt_ref[1] = tg[32:48]
    tb = _dot_nt(wtb_ref[...], hb) + btb_ref[...]
    mkt_ref[...] = tb[0:MQK].astype(BF16)
    for j in range(tm // tks):
        vst_ref[j] = tb[MQK:MQK + NKV, j * tks:(j + 1) * tks].astype(BF16)
    for j in range(tm // tkw):
        vwt_ref[j] = tb[MQK + NKV:MQK + 2 * NKV, j * tkw:(j + 1) * tkw].astype(BF16)

    if rows_t:
        ct, st = ct_ref[...], st_ref[...]
        half = ROPE_DIM // 2

        def rope_t(kt):
            parts = []
            for g in range(N_KV):
                blk = kt[g * HEAD_DIM:(g + 1) * HEAD_DIM]
                x1, x2 = blk[0:half], blk[half:ROPE_DIM]
                parts += [x1 * ct - x2 * st, x2 * ct + x1 * st, blk[ROPE_DIM:HEAD_DIM]]
            return jnp.concatenate(parts, axis=0)

        o = MQK + 2 * NKV
        cmp_ref[0:NKV] = rope_t(tb[o:o + NKV])
        cmp_ref[NKV:ROW_W] = tb[o + NKV:o + 2 * NKV]
        slc_ref[0:NKV] = rope_t(tb[o + 2 * NKV:o + 3 * NKV])
        slc_ref[NKV:ROW_W] = tb[MQK:MQK + NKV]
        win_ref[0:NKV] = rope_t(tb[o + 3 * NKV:o + 4 * NKV])
        win_ref[NKV:ROW_W] = tb[MQK + NKV:MQK + 2 * NKV]


def _in_proj(x, g_pre, sc, sh, wts, rope_tabs, tm, rows_per_mod, rows_t):
    r, d = x.shape
    nt = r // tm
    tks = tkw = min(TKV, tm)
    tpm = rows_per_mod // tm
    tpr = rope_tabs[0].shape[0] // tm
    mr = sc.shape[1]
    row = lambda i: (i, 0)
    const2 = lambda i: (0, 0)
    if rows_t:
        rows_shape = (r // rows_per_mod, ROW_W, rows_per_mod)
        rows_spec = pl.BlockSpec((None, ROW_W, tm), lambda i: (i // tpm, 0, i % tpm))
    else:
        rows_shape = (r, ROW_W)
        rows_spec = pl.BlockSpec((tm, ROW_W), row)
    ntb = wts["w_tb"].shape[0]
    in_specs = [
        pl.BlockSpec((tm, d), row),
        pl.BlockSpec((1, d), const2),
        pl.BlockSpec((1, mr, d), lambda i: (i // tpm, 0, 0)),
        pl.BlockSpec((1, mr, d), lambda i: (i // tpm, 0, 0)),
        pl.BlockSpec((d, _C_END), const2),
        pl.BlockSpec((1, _C_END), const2),
        pl.BlockSpec((48, d), const2),
        pl.BlockSpec((48, 1), const2),
        pl.BlockSpec((ntb, d), const2),
        pl.BlockSpec((ntb, 1), const2),
        pl.BlockSpec((tm, LANES), lambda i: (i % tpr, 0)),
        pl.BlockSpec((tm, LANES), lambda i: (i % tpr, 0)),
        pl.BlockSpec((tm, LANES), lambda i: (i % tpr, 0)),
        pl.BlockSpec((ROPE_DIM // 2, tm), lambda i: (0, i % tpr)),
        pl.BlockSpec((ROPE_DIM // 2, tm), lambda i: (0, i % tpr)),
        pl.BlockSpec((CMP_BLOCK, ROW_W), const2),
    ]
    outs = [
        ("mq", (r, MQK), BF16, pl.BlockSpec((tm, MQK), row)),
        ("mk", (r, MQK), BF16, pl.BlockSpec((tm, MQK), row)),
        ("mv", (r, MV), BF16, pl.BlockSpec((tm, MV), row)),
        ("mo", (r, MV), F32, pl.BlockSpec((tm, MV), row)),
        ("nq", (N_HEADS, r, HEAD_DIM), BF16, pl.BlockSpec((N_HEADS, tm, HEAD_DIM), lambda i: (0, i, 0))),
        ("cmp", rows_shape, F32, rows_spec),
        ("slc", rows_shape, F32, rows_spec),
        ("win", rows_shape, F32, rows_spec),
        ("ks", (N_KV, r, HEAD_DIM), BF16, pl.BlockSpec((N_KV, tm, HEAD_DIM), lambda i: (0, i, 0))),
        ("kw", (N_KV, r, HEAD_DIM), BF16, pl.BlockSpec((N_KV, tm, HEAD_DIM), lambda i: (0, i, 0))),
        ("gab", (r, 2 * d), F32, pl.BlockSpec((tm, 2 * d), row)),
        ("kcvc", (r // CMP_BLOCK, ROW_W), F32, pl.BlockSpec((tm // CMP_BLOCK, ROW_W), row)),
        ("gt", (8, r), F32, pl.BlockSpec((8, tm), lambda i: (0, i))),
        ("ngt", (N_KV, 16, r), F32, pl.BlockSpec((N_KV, 16, tm), lambda i: (0, 0, i))),
        ("mkt", (MQK, r), BF16, pl.BlockSpec((MQK, tm), lambda i: (0, i))),
        ("vst", (r // tks, NKV, tks), BF16, pl.BlockSpec((tm // tks, NKV, tks), lambda i: (i, 0, 0))),
        ("vwt", (r // tkw, NKV, tkw), BF16, pl.BlockSpec((tm // tkw, NKV, tkw), lambda i: (i, 0, 0))),
    ]
    res = pl.pallas_call(
        functools.partial(_proj_kernel, tks=tks, tkw=tkw, rows_t=rows_t),
        grid=(nt,),
        in_specs=in_specs,
        out_specs=[o[3] for o in outs],
        out_shape=[jax.ShapeDtypeStruct(o[1], o[2]) for o in outs],
        compiler_params=_cparams("parallel"),
        name="in_proj",
    )(x, g_pre.reshape(1, d), sc, sh, wts["w_main"], wts["b_main"], wts["w_tg"], wts["b_tg"],
      wts["w_tb"], wts["b_tb"], *rope_tabs, wts["w_cmp"])
    return {o[0]: v for o, v in zip(outs, res)}


def _mlstm_chunk_kernel(q_ref, k_ref, kt_ref, v_ref, mo_ref, gt_ref, nw_ref,
                        ya_ref, c_out_ref, m_out_ref, caug, mstate):
    t = q_ref.shape[0]
    c = pl.program_id(1)

    @pl.when(c == 0)
    def _():
        caug[...] = jnp.zeros_like(caug)
        mstate[...] = jnp.zeros_like(mstate)

    gc = GATE_CAP * jnp.tanh(gt_ref[...] / GATE_CAP)
    lane = lax.broadcasted_iota(jnp.int32, (8, t), 1)
    b = _log_sigmoid(gc)
    k = 1
    while k < t:
        b = b + jnp.where(lane >= k, pltpu.roll(b, k, 1), 0.0)
        k *= 2

    row = lax.broadcasted_iota(jnp.int32, (t, t), 0)
    col = lax.broadcasted_iota(jnp.int32, (t, t), 1)
    causal = col <= row
    eye = col == row
    lane_v = lax.broadcasted_iota(jnp.int32, (t, M_DV), 1)

    def to_col(r):
        return jnp.sum(jnp.where(eye, r, 0.0), axis=1, keepdims=True)

    for h in range(M_HEADS):
        qs, vs_ = slice(h * M_DQK, (h + 1) * M_DQK), slice(h * M_DV, (h + 1) * M_DV)
        irow = gc[h:h + 1, :]
        brow = b[M_HEADS + h:M_HEADS + h + 1, :]
        bcol = to_col(brow)
        mprev = mstate[h:h + 1, 0:1]
        d = jnp.where(causal, bcol - brow + irow, NEG)
        minter = bcol + mprev
        mt = jnp.maximum(minter, jnp.max(d, axis=1, keepdims=True))
        qh, kh, vh = q_ref[:, qs], k_ref[:, qs], v_ref[:, vs_]
        w = _dot_nt(qh, kh) * jnp.exp(d - mt)
        ainter = jnp.exp(minter - mt)
        ca = caug[h]
        qc = _dot(qh, ca.astype(BF16))
        num = _dot(w.astype(BF16), vh) + ainter * qc[:, 0:M_DV]
        den = jnp.sum(w, axis=1, keepdims=True) + ainter * qc[:, M_DV:M_DV + 1]
        hh = num / jnp.maximum(jnp.abs(den), jnp.exp(-mt))
        y = _rms(hh, nw_ref[:, vs_])
        ya_ref[:, vs_] = (jax.nn.sigmoid(mo_ref[:, vs_]) * y).astype(BF16)
        b_last = brow[:, t - 1:t]
        grow = b_last - brow + irow
        mnew = jnp.maximum(b_last + mprev, jnp.max(grow, axis=1, keepdims=True))
        ain = to_col(jnp.exp(grow - mnew))
        ast = jnp.exp(b_last + mprev - mnew)
        vsa = jnp.concatenate([ain * vh.astype(F32), jnp.where(lane_v == 0, ain, 0.0)], axis=1)
        caug[h] = ast * ca + _dot(kt_ref[qs, :], vsa.astype(BF16))
        mstate[h:h + 1, :] = jnp.broadcast_to(mnew, (1, LANES))

    @pl.when(c == pl.num_programs(1) - 1)
    def _():
        c_out_ref[0] = caug[...]
        m_out_ref[0] = mstate[...]


def _mlstm_prompt(p, norm_w, nb, s):
    t = T_CHUNK
    assert s % t == 0
    nc = s // t
    r = nb * s
    rowc = lambda b, c: (b * nc + c, 0)
    ya, caug, mst = pl.pallas_call(
        _mlstm_chunk_kernel,
        grid=(nb, nc),
        in_specs=[pl.BlockSpec((t, MQK), rowc),
                  pl.BlockSpec((t, MQK), rowc),
                  pl.BlockSpec((MQK, t), lambda b, c: (0, b * nc + c)),
                  pl.BlockSpec((t, MV), rowc),
                  pl.BlockSpec((t, MV), rowc),
                  pl.BlockSpec((8, t), lambda b, c: (0, b * nc + c)),
                  pl.BlockSpec((1, MV), lambda b, c: (0, 0))],
        out_specs=[pl.BlockSpec((t, MV), rowc),
                   pl.BlockSpec((1, M_HEADS, M_DQK, 2 * M_DV), lambda b, c: (b, 0, 0, 0)),
                   pl.BlockSpec((1, 8, LANES), lambda b, c: (b, 0, 0))],
        out_shape=[jax.ShapeDtypeStruct((r, MV), BF16),
                   jax.ShapeDtypeStruct((nb, M_HEADS, M_DQK, 2 * M_DV), F32),
                   jax.ShapeDtypeStruct((nb, 8, LANES), F32)],
        scratch_shapes=[pltpu.VMEM((M_HEADS, M_DQK, 2 * M_DV), F32), pltpu.VMEM((8, LANES), F32)],
        compiler_params=_cparams("parallel", "arbitrary"),
        name="mlstm_prompt",
    )(p["mq"], p["mk"], p["mkt"], p["mv"], p["mo"], p["gt"], norm_w.reshape(1, MV))
    return ya, caug[..., :M_DV], caug[..., M_DV], mst[:, :M_HEADS, 0]


def _mlstm_step_kernel(q_ref, k_ref, v_ref, mo_ref, g_ref, c_ref, n_ref, m_ref, nw_ref,
                       ya_ref, co_ref, no_ref, mo_out_ref):
    bb = q_ref.shape[0]
    q, k, v = q_ref[...], k_ref[...], v_ref[...]
    gc = GATE_CAP * jnp.tanh(g_ref[...] / GATE_CAP)
    lf = _log_sigmoid(gc)
    m0 = m_ref[...]
    eye = lax.broadcasted_iota(jnp.int32, (M_DQK, M_DQK), 0) == lax.broadcasted_iota(jnp.int32, (M_DQK, M_DQK), 1)

    def to_col(r):
        return jnp.sum(jnp.where(eye, r, 0.0), axis=1, keepdims=True)

    ipre = gc[:, 0:M_HEADS]
    minter = lf[:, M_HEADS:2 * M_HEADS] + m0
    mt = jnp.maximum(minter, ipre)
    ain_all, ast_all, emt_all = jnp.exp(ipre - mt), jnp.exp(minter - mt), jnp.exp(-mt)
    mo_out_ref[...] = mt
    pairs = [(b, h) for b in range(bb) for h in range(M_HEADS)]
    qsl = lambda h: slice(h * M_DQK, (h + 1) * M_DQK)
    vsl = lambda h: slice(h * M_DV, (h + 1) * M_DV)
    qh = [q[b:b + 1, qsl(h)] for b, h in pairs]
    kh = [k[b:b + 1, qsl(h)] for b, h in pairs]
    vh = [v[b:b + 1, vsl(h)] for b, h in pairs]
    n0 = [n_ref[b, h:h + 1, :] for b, h in pairs]
    qcol = [to_col(x) for x in qh]
    kcol = [to_col(x) for x in kh]
    qk = [jnp.sum(a * b_, axis=1, keepdims=True) for a, b_ in zip(qh, kh)]
    qn = [jnp.sum(a * b_, axis=1, keepdims=True) for a, b_ in zip(qh, n0)]
    for i, (b, h) in enumerate(pairs):
        c0 = c_ref[b, h]
        ain, ast = ain_all[b:b + 1, h:h + 1], ast_all[b:b + 1, h:h + 1]
        w = qk[i] * ain
        num = w * vh[i] + ast * jnp.sum(qcol[i] * c0, axis=0, keepdims=True)
        den = w + ast * qn[i]
        hh = num / jnp.maximum(jnp.abs(den), emt_all[b:b + 1, h:h + 1])
        y = _rms(hh, nw_ref[:, vsl(h)])
        ya_ref[b:b + 1, vsl(h)] = (jax.nn.sigmoid(mo_ref[b:b + 1, vsl(h)]) * y).astype(ya_ref.dtype)
        co_ref[b, h] = ast * c0 + kcol[i] * (ain * vh[i])
        no_ref[b, h:h + 1, :] = ast * n0[i] + ain * kh[i]


def _mlstm_step(p, gcol, c0, n0, m0, norm_w):
    nb = c0.shape[0]
    bb = 8
    row = lambda i: (i, 0)
    return pl.pallas_call(
        _mlstm_step_kernel,
        grid=(nb // bb,),
        in_specs=[pl.BlockSpec((bb, MQK), row), pl.BlockSpec((bb, MQK), row), pl.BlockSpec((bb, MV), row),
                  pl.BlockSpec((bb, MV), row), pl.BlockSpec((bb, 8), row),
                  pl.BlockSpec((bb, M_HEADS, M_DQK, M_DV), lambda i: (i, 0, 0, 0)),
                  pl.BlockSpec((bb, M_HEADS, M_DQK), lambda i: (i, 0, 0)),
                  pl.BlockSpec((bb, M_HEADS), row),
                  pl.BlockSpec((1, MV), lambda i: (0, 0))],
        out_specs=[pl.BlockSpec((bb, MV), row),
                   pl.BlockSpec((bb, M_HEADS, M_DQK, M_DV), lambda i: (i, 0, 0, 0)),
                   pl.BlockSpec((bb, M_HEADS, M_DQK), lambda i: (i, 0, 0)),
                   pl.BlockSpec((bb, M_HEADS), row)],
        out_shape=[jax.ShapeDtypeStruct((nb, MV), F32),
                   jax.ShapeDtypeStruct(c0.shape, F32),
                   jax.ShapeDtypeStruct(n0.shape, F32),
                   jax.ShapeDtypeStruct(m0.shape, F32)],
        compiler_params=_cparams("parallel"),
        name="mlstm_step",
    )(p["mq"].astype(F32), p["mk"].astype(F32), p["mv"].astype(F32), p["mo"], gcol, c0, n0, m0,
      norm_w.reshape(1, MV))


def _nsa_prompt_kernel(q_ref, kc_ref, vct_ref, ks_ref, vst_ref, kw_ref, vwt_ref, ng_ref, wb_ref, o_ref,
                       bias_ref, s_ref, *, tks):
    tq = q_ref.shape[1]
    r4 = HPG * tq
    gw = HPG * HEAD_DIM
    tkv = vst_ref.shape[-1]
    ncmp = kc_ref.shape[1]
    nsel = ncmp // 2
    groups = range(N_KV)
    i = pl.program_id(1)
    t0 = i * tq
    qs = [q_ref[g * HPG:(g + 1) * HPG].reshape(r4, HEAD_DIM) for g in groups]
    lane_t = t0 + (lax.broadcasted_iota(jnp.int32, (1, r4), 1) & (tq - 1))

    ones_rows = jnp.where(lax.broadcasted_iota(jnp.int32, (ONES_ROWS, tkv), 0) == 0, 1.0, 0.0).astype(BF16)

    def with_ones(vt):
        return jnp.concatenate([vt, ones_rows], axis=0)

    nwv = (WINDOW + tq) // tkv
    w0 = jnp.maximum(t0 - WINDOW, 0) // tkv
    sws = [_dot_nt(kw_ref[g, pl.ds(pl.multiple_of(w0 * tkv, tkv), nwv * tkv), :], qs[g]) + wb_ref[...]
           for g in groups]
    pws = [jnp.exp2(sw - jnp.max(sw, axis=0, keepdims=True)).astype(BF16) for sw in sws]
    accws = []
    for g in groups:
        accw = _dot(with_ones(vwt_ref[w0, g]), pws[g][0:tkv])
        for u in range(1, nwv):
            accw = accw + _dot(with_ones(vwt_ref[w0 + u, g]), pws[g][u * tkv:(u + 1) * tkv])
        accws.append(accw)

    rr = lax.broadcasted_iota(jnp.int32, (ncmp, 1), 0)
    jc = jnp.where(rr < nsel, 2 * rr, 2 * (rr - nsel) + 1)
    valid = ((jc + 1) * CMP_BLOCK - 1) <= lane_t
    jb = lax.broadcasted_iota(jnp.int32, (nsel, tq), 0)
    tt = t0 + lax.broadcasted_iota(jnp.int32, (nsel, tq), 1)
    forced = (jb == (tt >> 6)) | (jb == 0)
    future = jb * SEL_BLOCK > tt
    jf = jb.astype(F32)
    ocs, keys = [], []
    for g in groups:
        sm = jnp.where(valid, _dot_nt(kc_ref[g], qs[g]), NEG)
        e = jnp.exp2(sm - jnp.max(sm, axis=0, keepdims=True))
        pc = e * jnp.where(lane_t >= CMP_BLOCK - 1, 1.0 / jnp.sum(e, axis=0, keepdims=True), 0.0)
        ocs.append(_dot(vct_ref[g], pc.astype(BF16)))
        pp = pc[0:nsel] + pc[nsel:ncmp]
        imp = pp[:, 0:tq]
        for h in range(1, HPG):
            imp = imp + pp[:, h * tq:(h + 1) * tq]
        keys.append(jnp.where(forced, IMP_FORCED, jnp.where(future, IMP_FUTURE, imp)))

    biases = [jnp.full((nsel, tq), NEG, F32) for _ in groups]
    for _ in range(min(TOP_K, nsel)):
        for g in groups:
            cur = jnp.max(keys[g], axis=0, keepdims=True)
            first = jnp.min(jnp.where(keys[g] == cur, jf, float(nsel)), axis=0, keepdims=True)
            pick = jf == first
            biases[g] = jnp.where(pick, 0.0, biases[g])
            keys[g] = jnp.where(pick, IMP_TAKEN, keys[g])
    bpq = tq // SEL_BLOCK
    own = [(jb == bpq * i + u) for u in range(bpq)]
    own_bias = []
    for g in groups:
        bg = jnp.where(future, NEG, biases[g])
        own_bias.append([jnp.max(jnp.where(o, bg, NEG), axis=0, keepdims=True) for o in own])
        for o in own:
            bg = jnp.where(o, NEG, bg)
        bias4 = jnp.concatenate([bg] * HPG, axis=1)
        for j in range(nsel):
            bias_ref[g, j] = jnp.broadcast_to(bias4[j:j + 1, :], (8, r4))

    bpv = tkv // SEL_BLOCK
    vpt = tks // tkv
    nbt = vpt * bpv
    sub = SEL_BLOCK // 8

    def sel_scores(slot, kt):
        k0 = pl.multiple_of(kt * tks, tks)
        for g in groups:
            s = _dot_nt(ks_ref[g, pl.ds(k0, tks), :], qs[g])
            s_ref[slot, g] = (s.reshape(nbt, sub, 8, r4) + bias_ref[g, pl.ds(kt * nbt, nbt)][:, None]).reshape(tks, r4)

    def sel_update(state, ss, v0):
        mid = []
        for g in groups:
            m, acc = state[g]
            mn = jnp.maximum(m, jnp.max(ss[g], axis=0, keepdims=True))
            alpha = jnp.exp2(m - mn)
            mid.append((mn, alpha * acc, jnp.exp2(ss[g] - mn).astype(BF16)))
        out = []
        for g in groups:
            mn, acc, pb = mid[g]
            nv = ss[g].shape[0] // tkv
            step = 2 if nv % 2 == 0 else 1
            for u in range(0, nv, step):
                vt = jnp.concatenate([with_ones(vst_ref[v0 + u + w, g]) for w in range(step)], axis=1)
                acc = acc + _dot(vt, pb[u * tkv:(u + step) * tkv])
            out.append((mn, acc))
        return tuple(out)

    init = tuple((jnp.full((1, r4), NEG, F32), jnp.zeros((HEAD_DIM + ONES_ROWS, r4), F32)) for _ in groups)
    kd = t0 // tks

    def from_slot(slot):
        return tuple(s_ref[slot, g] for g in groups)

    def sel_pair(j, state):
        sel_scores(1, jnp.minimum(2 * j + 1, kd))
        state = sel_update(state, from_slot(0), 2 * j * vpt)
        sel_scores(0, jnp.minimum(2 * j + 2, kd))
        return sel_update(state, from_slot(1), (2 * j + 1) * vpt)

    sel_scores(0, 0)
    state = lax.fori_loop(0, (kd + 1) // 2, sel_pair, init)
    state = lax.cond(kd % 2 == 0, lambda st: sel_update(st, from_slot(0), kd * vpt), lambda st: st, state)
    kl = lax.broadcasted_iota(jnp.int32, (tq, 1), 0)
    tri = kl <= (lane_t - t0)
    s_own = []
    for g in groups:
        s = _dot_nt(ks_ref[g, pl.ds(pl.multiple_of(t0, tq), tq), :], qs[g])
        ob = jnp.concatenate([jnp.broadcast_to(jnp.concatenate([b] * HPG, axis=1), (SEL_BLOCK, r4))
                              for b in own_bias[g]], axis=0)
        s_own.append(jnp.where(tri, s + ob, NEG))
    sel = sel_update(state, tuple(s_own), i * (tq // tkv))

    ng = jax.nn.sigmoid(ng_ref[...])
    for g in groups:
        accw = accws[g]

        def gate(br):
            return jnp.concatenate([ng[g, h * 3 + br:h * 3 + br + 1, :] for h in range(HPG)], axis=1)

        accs = sel[g][1]
        ls, lw = accs[HEAD_DIM:HEAD_DIM + 1], accw[HEAD_DIM:HEAD_DIM + 1]
        out_t = (gate(0) * ocs[g] + (gate(1) / ls) * accs[0:HEAD_DIM]
                 + (gate(2) / lw) * accw[0:HEAD_DIM])
        stacked = jnp.concatenate([out_t[:, h * tq:(h + 1) * tq] for h in range(HPG)], axis=0)
        o_ref[:, g * gw:(g + 1) * gw] = stacked.T.astype(o_ref.dtype)


def _window_bias(tq):
    nvar = WINDOW // tq + 1
    v = jnp.arange(nvar)[:, None, None]
    u = jnp.arange(WINDOW + tq)[None, :, None]
    tt = jnp.arange(tq)[None, None, :]
    dpos = v * tq + tt - u
    ok = (dpos >= 0) & (dpos < WINDOW)
    return jnp.tile(jnp.where(ok, 0.0, NEG).astype(F32), (1, 1, HPG))


def _nsa_prompt(p, kc, vct, nb, s):
    tq = TQ
    tks = min(TKS, s)
    assert s % tks == 0 and tq == TKV and tq & (tq - 1) == 0 and tks % TKV == 0 and s >= WINDOW + tq
    nqb = s // tq
    r = nb * s
    ncmp = s // CMP_BLOCK
    gw = HPG * HEAD_DIM
    vst = p["vst"].reshape(r // TKV, N_KV, HEAD_DIM, TKV)
    vwt = p["vwt"].reshape(r // TKV, N_KV, HEAD_DIM, TKV)
    nvar = WINDOW // tq
    return pl.pallas_call(
        functools.partial(_nsa_prompt_kernel, tks=tks),
        grid=(nb, nqb),
        in_specs=[pl.BlockSpec((N_HEADS, tq, HEAD_DIM), lambda b, i: (0, b * nqb + i, 0)),
                  pl.BlockSpec((None, N_KV, ncmp, HEAD_DIM), lambda b, i: (b, 0, 0, 0)),
                  pl.BlockSpec((None, N_KV, HEAD_DIM, ncmp), lambda b, i: (b, 0, 0, 0)),
                  pl.BlockSpec((N_KV, s, HEAD_DIM), lambda b, i: (0, b, 0)),
                  pl.BlockSpec((s // TKV, N_KV, HEAD_DIM, TKV), lambda b, i: (b, 0, 0, 0)),
                  pl.BlockSpec((N_KV, s, HEAD_DIM), lambda b, i: (0, b, 0)),
                  pl.BlockSpec((s // TKV, N_KV, HEAD_DIM, TKV), lambda b, i: (b, 0, 0, 0)),
                  pl.BlockSpec((N_KV, 16, tq), lambda b, i: (0, 0, b * nqb + i)),
                  pl.BlockSpec((None, WINDOW + tq, HPG * tq), lambda b, i: (jnp.minimum(i, nvar), 0, 0))],
        out_specs=pl.BlockSpec((tq, NQ), lambda b, i: (b * nqb + i, 0)),
        out_shape=jax.ShapeDtypeStruct((r, NQ), BF16),
        scratch_shapes=[pltpu.VMEM((N_KV, s // SEL_BLOCK, 8, HPG * tq), F32),
                        pltpu.VMEM((2, N_KV, tks, HPG * tq), F32)],
        compiler_params=_cparams("parallel", "arbitrary"),
        name="nsa_prompt",
    )(p["nq"], kc, vct, p["ks"], vst, p["kw"], vwt, p["ngt"], _window_bias(tq))


def _prompt_cmp_operands(kcvc, nb, s):
    ncmp = s // CMP_BLOCK
    a = kcvc.reshape(nb, ncmp // 2, 2, 2, N_KV, HEAD_DIM)
    a = jnp.transpose(a, (0, 3, 4, 2, 1, 5)).reshape(nb, 2, N_KV, ncmp, HEAD_DIM)
    kc = a[:, 0].astype(BF16)
    vct = jnp.swapaxes(a[:, 1], -1, -2).astype(BF16)
    return kc, vct


def _by_group(x0, x1):
    return jnp.where(lax.broadcasted_iota(jnp.int32, x0.shape, 0) < HPG, x0, x1)


def _nsa_dec_cmp_kernel(pt_ref, *refs, pos):
    del pt_ref
    pages = refs[:PAGES_PER_STEP]
    wt_ref, seg_ref, q_ref, oc_ref, sel_ref, kcvc = refs[PAGES_PER_STEP:]
    pc = pl.program_id(1)

    @pl.when(pc == 0)
    def _():
        kcvc[...] = jnp.zeros_like(kcvc)

    x = jnp.concatenate([pg[...] for pg in pages], axis=1)
    kcvc[...] += _dot((x * wt_ref[...]).astype(BF16), seg_ref[...])

    @pl.when(pc == pl.num_programs(1) - 1)
    def _():
        ncmp = kcvc.shape[1]
        half = ncmp // 2
        qb = q_ref[...].astype(BF16)
        kv = kcvc[...].astype(BF16)
        kt = [kv[g * HEAD_DIM:(g + 1) * HEAD_DIM] for g in range(N_KV)]
        vt = [kv[NKV + g * HEAD_DIM:NKV + (g + 1) * HEAD_DIM] for g in range(N_KV)]
        s = _by_group(_dot(qb, kt[0]), _dot(qb, kt[1]))
        cc = lax.broadcasted_iota(jnp.int32, (1, ncmp), 1)
        jc = jnp.where(cc < half, 2 * cc, 2 * (cc - half) + 1)
        valid = ((jc + 1) * CMP_BLOCK - 1) <= pos
        sm = jnp.where(valid, s, NEG)
        e = jnp.exp2(sm - jnp.max(sm, axis=1, keepdims=True))
        p = jnp.where(valid, e / jnp.sum(e, axis=1, keepdims=True), 0.0)
        pb = p.astype(BF16)
        oc_ref[...] = _by_group(_dot_nt(pb, vt[0]), _dot_nt(pb, vt[1]))
        pp = p[:, 0:half] + p[:, half:ncmp]
        jl = lax.broadcasted_iota(jnp.int32, (1, half), 1)
        ii = lax.broadcasted_iota(jnp.int32, (half, half), 0)
        jj = lax.broadcasted_iota(jnp.int32, (half, half), 1)
        for g in range(N_KV):
            imp = jnp.sum(pp[g * HPG:(g + 1) * HPG], axis=0, keepdims=True)
            forced = (jl == pos // SEL_BLOCK) | (jl == 0)
            key = jnp.where(forced, IMP_FORCED, jnp.where(jl * SEL_BLOCK > pos, IMP_FUTURE, imp))
            kcol = jnp.sum(jnp.where(ii == jj, key, 0.0), axis=1, keepdims=True)
            ahead = (kcol > key) | ((kcol == key) & (ii < jj))
            rank = jnp.sum(jnp.where(ahead, 1.0, 0.0), axis=0, keepdims=True)
            rr = lax.broadcasted_iota(jnp.int32, (TOP_K, half), 0).astype(F32)
            blk = jnp.sum(jnp.where(rank == rr, jl.astype(F32), 0.0), axis=1, keepdims=True)
            sel_ref[g] = jnp.broadcast_to(blk, (TOP_K, LANES))


def _page_specs():
    def spec(kk):
        return pl.BlockSpec((None, ROW_W, PAGE_SIZE), lambda b, pc, pt: (pt[b, pc * PAGES_PER_STEP + kk], 0, 0))
    return [spec(kk) for kk in range(PAGES_PER_STEP)]


def _nsa_dec_cmp(cache_t, page_table, q, wt, seg):
    nb, n_pages = page_table.shape
    assert n_pages % PAGES_PER_STEP == 0
    npc = n_pages // PAGES_PER_STEP
    past = n_pages * PAGE_SIZE
    ncmp = past // CMP_BLOCK
    nselp = ncmp // 2
    keys = PAGES_PER_STEP * PAGE_SIZE
    perb = lambda b, pc, pt: (b, 0, 0)
    grid_spec = pltpu.PrefetchScalarGridSpec(
        num_scalar_prefetch=1,
        grid=(nb, npc),
        in_specs=_page_specs() + [
            pl.BlockSpec((ROW_W, keys), lambda b, pc, pt: (0, 0)),
            pl.BlockSpec((None, keys, ncmp), lambda b, pc, pt: (pc, 0, 0)),
            pl.BlockSpec((None, N_HEADS, HEAD_DIM), perb)],
        out_specs=[pl.BlockSpec((None, N_HEADS, HEAD_DIM), perb),
                   pl.BlockSpec((None, N_KV, TOP_K, LANES), lambda b, pc, pt: (b, 0, 0, 0))],
        scratch_shapes=[pltpu.VMEM((ROW_W, ncmp), F32)],
    )
    del nselp
    return pl.pallas_call(
        functools.partial(_nsa_dec_cmp_kernel, pos=past),
        grid_spec=grid_spec,
        out_shape=[jax.ShapeDtypeStruct((nb, N_HEADS, HEAD_DIM), F32),
                   jax.ShapeDtypeStruct((nb, N_KV, TOP_K, LANES), F32)],
        compiler_params=_cparams("parallel", "arbitrary"),
        name="nsa_decode_cmp",
    )(page_table, *([cache_t] * PAGES_PER_STEP), wt, seg, q)


def _nsa_dec_sel_kernel(pt_ref, blk_ref, *refs, pos, wb, k_past):
    del pt_ref
    nsel = N_KV * k_past
    pages = refs[:nsel]
    (q_ref, oc_ref, ng_ref, nslc_ref, nwin_ref, nwcol_ref, win_ref, o_ref, wout_ref) = refs[nsel:]
    b = pl.program_id(0)
    q = q_ref[...]
    qb = q.astype(BF16)
    ksl = [slice(g * HEAD_DIM, (g + 1) * HEAD_DIM) for g in range(N_KV)]
    vsl = [slice(NKV + g * HEAD_DIM, NKV + (g + 1) * HEAD_DIM) for g in range(N_KV)]
    half_of_lane = lax.broadcasted_iota(jnp.int32, (1, PAGE_SIZE), 1) // SEL_BLOCK
    bpp = PAGE_SIZE // SEL_BLOCK
    ss, vt = [], []
    for g in range(N_KV):
        own = range(g * k_past, (g + 1) * k_past)
        kt = jnp.concatenate([pages[r][g] for r in own], axis=1).astype(BF16)
        vt.append(jnp.concatenate([pages[r][N_KV + g] for r in own], axis=1).astype(BF16))
        keep = jnp.concatenate([jnp.where(half_of_lane == blk_ref[b, r] % bpp, 0.0, NEG) for r in own], axis=1)
        ss.append(_dot(qb, kt) + keep)
    s_sel = _by_group(ss[0], ss[1])

    def new_row(row):
        sn = _by_group(jnp.sum(q * row[:, ksl[0]], axis=1, keepdims=True),
                       jnp.sum(q * row[:, ksl[1]], axis=1, keepdims=True))
        return sn, _by_group(jnp.broadcast_to(row[:, vsl[0]], q.shape), jnp.broadcast_to(row[:, vsl[1]], q.shape))

    def attend(s, sn, vn, vts_):
        m = jnp.maximum(jnp.max(s, axis=1, keepdims=True), sn)
        p = jnp.exp2(s - m)
        pn = jnp.exp2(sn - m)
        pb = p.astype(BF16)
        num = _by_group(_dot_nt(pb, vts_[0]), _dot_nt(pb, vts_[1])) + pn * vn
        return num / (jnp.sum(p, axis=1, keepdims=True) + pn)

    o_sel = attend(s_sel, *new_row(nslc_ref[...]), vt)
    win = win_ref[...]
    wbf = win.astype(BF16)
    sw = _by_group(_dot(qb, wbf[ksl[0]]), _dot(qb, wbf[ksl[1]]))
    dpos = pos - (pos - wb + lax.broadcasted_iota(jnp.int32, (1, wb), 1))
    okw = (dpos >= 0) & (dpos < WINDOW) & (pos - dpos >= 0)
    o_win = attend(jnp.where(okw, sw, NEG), *new_row(nwin_ref[...]), [wbf[vsl[0]], wbf[vsl[1]]])
    gates = jax.nn.sigmoid(ng_ref[...])
    o_ref[...] = gates[:, 0:1] * oc_ref[...] + gates[:, 1:2] * o_sel + gates[:, 2:3] * o_win
    lid = lax.broadcasted_iota(jnp.int32, (1, wb), 1)
    wout_ref[...] = jnp.where(lid == wb - 1, nwcol_ref[...], pltpu.roll(win, wb - 1, 1))


def _nsa_dec_sel(cache_t, page_table, blocks, q, oc, ng, nslc, nwin, cache_win_t):
    nb, n_pages = page_table.shape
    past = n_pages * PAGE_SIZE
    wb = cache_win_t.shape[-1]
    assert wb == WINDOW and past >= WINDOW
    k_past = blocks.shape[1] // N_KV
    bpp = PAGE_SIZE // SEL_BLOCK
    cache4 = cache_t.reshape(cache_t.shape[0], 2 * N_KV, HEAD_DIM, PAGE_SIZE)
    perb = lambda b, pt, bl: (b, 0, 0)

    n_pool = cache_t.shape[0]

    def tile_spec(g, r):
        def index(b, pt, bl):
            col = jnp.clip(bl[b, g * k_past + r] // bpp, 0, n_pages - 1)
            return (jnp.clip(pt[b, col], 0, n_pool - 1), 0, 0, 0)
        return pl.BlockSpec((None, 2 * N_KV, HEAD_DIM, PAGE_SIZE), index)

    tiles = [tile_spec(g, r) for g in range(N_KV) for r in range(k_past)]
    grid_spec = pltpu.PrefetchScalarGridSpec(
        num_scalar_prefetch=2,
        grid=(nb,),
        in_specs=tiles + [
            pl.BlockSpec((None, N_HEADS, HEAD_DIM), perb),
            pl.BlockSpec((None, N_HEADS, HEAD_DIM), perb),
            pl.BlockSpec((None, N_HEADS, LANES), perb),
            pl.BlockSpec((None, 1, ROW_W), perb),
            pl.BlockSpec((None, 1, ROW_W), perb),
            pl.BlockSpec((None, ROW_W, 1), perb),
            pl.BlockSpec((None, ROW_W, wb), perb)],
        out_specs=[pl.BlockSpec((None, N_HEADS, HEAD_DIM), perb),
                   pl.BlockSpec((None, ROW_W, wb), perb)],
    )
    return pl.pallas_call(
        functools.partial(_nsa_dec_sel_kernel, pos=past, wb=wb, k_past=k_past),
        grid_spec=grid_spec,
        out_shape=[jax.ShapeDtypeStruct((nb, N_HEADS, HEAD_DIM), F32),
                   jax.ShapeDtypeStruct((nb, ROW_W, wb), F32)],
        compiler_params=_cparams("parallel"),
        name="nsa_decode_sel",
    )(page_table, blocks, *([cache4] * len(tiles)), q, oc, ng, nslc, nwin, nwin.reshape(nb, ROW_W, 1),
      cache_win_t)


def _merge_kernel(x_ref, ya_ref, yb_ref, gab_ref, ga_ref, wa_ref, wb_ref, wo_ref, g_ref, o_ref):
    d = x_ref.shape[1]
    pa = _dot(ya_ref[...], wa_ref[...])
    pb = _dot(yb_ref[...], wb_ref[...])
    merged = jax.nn.sigmoid(gab_ref[:, 0:d]) * pa + jax.nn.sigmoid(gab_ref[:, d:2 * d]) * pb
    z = _dot(merged.astype(BF16), wo_ref[...])
    o_ref[...] = x_ref[...] + ga_ref[0] * _rms(z, g_ref[...])


def _merge(x, ya, yb, gab, ga1, w_b, wts, g_post, tm, rows_per_mod):
    r, d = x.shape
    wyb = yb.shape[1]
    tpm = rows_per_mod // tm
    mr = ga1.shape[1]
    row = lambda i: (i, 0)
    const2 = lambda i: (0, 0)
    return pl.pallas_call(
        _merge_kernel,
        grid=(r // tm,),
        in_specs=[pl.BlockSpec((tm, d), row), pl.BlockSpec((tm, MV), row), pl.BlockSpec((tm, wyb), row),
                  pl.BlockSpec((tm, 2 * d), row),
                  pl.BlockSpec((1, mr, d), lambda i: (i // tpm, 0, 0)),
                  pl.BlockSpec((MV, d), const2), pl.BlockSpec((wyb, d), const2), pl.BlockSpec((d, d), const2),
                  pl.BlockSpec((1, d), const2)],
        out_specs=pl.BlockSpec((tm, d), row),
        out_shape=jax.ShapeDtypeStruct((r, d), F32),
        compiler_params=_cparams("parallel"),
        name="merge",
    )(x, ya, yb, gab, ga1, wts["w_a"], w_b, wts["w_o"], g_post.reshape(1, d))


def _ffn_kernel(x_ref, sc_ref, sh_ref, ga_ref, g1_ref, g2_ref, wu_ref, wd_ref, o_ref):
    dff = wd_ref.shape[0]
    x = x_ref[...]
    hb = (_rms(x, g1_ref[...]) * (1.0 + sc_ref[0]) + sh_ref[0]).astype(BF16)
    acc = jnp.zeros(x.shape, F32)
    for c in range(dff // FFN_CK):
        lo, hi = c * FFN_CK, (c + 1) * FFN_CK
        gate = _dot(hb, wu_ref[:, lo:hi])
        up = _dot(hb, wu_ref[:, dff + lo:dff + hi])
        acc = acc + _dot((gate * jax.nn.sigmoid(gate) * up).astype(BF16), wd_ref[lo:hi, :])
    o_ref[...] = x + ga_ref[0] * _rms(acc, g2_ref[...])


def _ffn(x, sc, sh, ga2, g_pre, g_post, wts, tm, rows_per_mod):
    r, d = x.shape
    dff = wts["w_down"].shape[0]
    assert dff % FFN_CK == 0
    tpm = rows_per_mod // tm
    mr = sc.shape[1]
    row = lambda i: (i, 0)
    const2 = lambda i: (0, 0)
    mod = pl.BlockSpec((1, mr, d), lambda i: (i // tpm, 0, 0))
    return pl.pallas_call(
        _ffn_kernel,
        grid=(r // tm,),
        in_specs=[pl.BlockSpec((tm, d), row), mod, mod, mod,
                  pl.BlockSpec((1, d), const2), pl.BlockSpec((1, d), const2),
                  pl.BlockSpec((d, 2 * dff), const2), pl.BlockSpec((dff, d), const2)],
        out_specs=pl.BlockSpec((tm, d), row),
        out_shape=jax.ShapeDtypeStruct((r, d), F32),
        compiler_params=_cparams("parallel"),
        name="ffn",
    )(x, sc, sh, ga2, g_pre.reshape(1, d), g_post.reshape(1, d), wts["w_up"], wts["w_down"])


def _prep_weights(w_in, b_in, w_cmp_k, w_cmp_v, w_proj_a, w_proj_b, w_out, w_up, w_down):
    o_mi, o_nq, o_ng, o_ga = 4 * 256 + 2 * 256, 1544, 2824, 2848
    o_nkv = o_nq + NQ
    main_cols = [slice(0, o_mi), slice(o_nq, o_ng), slice(o_ga, w_in.shape[1])]
    w_main = jnp.concatenate([w_in[:, s] for s in main_cols], axis=1).astype(BF16)
    b_main = jnp.concatenate([b_in[s] for s in main_cols]).reshape(1, -1)
    z = lambda n: jnp.zeros((w_in.shape[0], n), w_in.dtype)
    ng0, ng1 = slice(o_ng, o_ng + 12), slice(o_ng + 12, o_ng + 24)
    tg_cols = [w_in[:, o_mi:o_mi + 8], z(8), w_in[:, ng0], z(4), w_in[:, ng1], z(4)]
    tg_b = [b_in[o_mi:o_mi + 8], jnp.zeros(8), b_in[ng0], jnp.zeros(4), b_in[ng1], jnp.zeros(4)]
    vs_, vw_ = slice(o_nkv + 3 * NKV, o_nkv + 4 * NKV), slice(o_nkv + 5 * NKV, o_nkv + 6 * NKV)
    nkv_part = lambda j: slice(o_nkv + j * NKV, o_nkv + (j + 1) * NKV)
    tb_parts = [slice(256, 512), vs_, vw_, nkv_part(0), nkv_part(1), nkv_part(2), nkv_part(4)]
    tb_cols = [w_in[:, s] for s in tb_parts]
    tb_b = [b_in[s] for s in tb_parts]
    w_cmp = jnp.concatenate([w_cmp_k, w_cmp_k, w_cmp_v, w_cmp_v], axis=1)
    return {
        "w_main": w_main, "b_main": b_main,
        "w_tg": jnp.concatenate(tg_cols, axis=1).T.astype(BF16),
        "b_tg": jnp.concatenate(tg_b).reshape(-1, 1).astype(F32),
        "w_tb": jnp.concatenate(tb_cols, axis=1).T.astype(BF16),
        "b_tb": jnp.concatenate(tb_b).reshape(-1, 1),
        "w_cmp": w_cmp,
        "w_a": w_proj_a.astype(BF16), "w_b": w_proj_b.astype(BF16), "w_o": w_out.astype(BF16),
        "w_up": w_up.astype(BF16), "w_down": w_down.astype(BF16),
    }


def _rope_tables(pos):
    half = ROPE_DIM // 2
    n = pos.shape[0]
    inv = ROPE_THETA ** (-jnp.arange(half, dtype=F32) * 2.0 / ROPE_DIM)
    ang = pos.astype(F32)[:, None] * inv[None, :]
    cos, sin = jnp.cos(ang), jnp.sin(ang)
    rest = HEAD_DIM - ROPE_DIM
    zh, zr = jnp.zeros((n, half), F32), jnp.zeros((n, rest), F32)
    a = jnp.concatenate([cos, cos, jnp.ones((n, rest), F32)], axis=1)
    p = jnp.concatenate([zh, sin, zr], axis=1)
    m = jnp.concatenate([-sin, zh, zr], axis=1)
    return tuple(jnp.tile(t, (1, LANES // HEAD_DIM)) for t in (a, p, m)) + (cos.T, sin.T)


def _rows_on_lanes(cache):
    n, rows = cache.shape[0], cache.shape[1]
    return jnp.transpose(cache, (0, 2, 3, 4, 1)).reshape(n, ROW_W, rows)


def _cmp_step_operands(w_cmp, n_pages):
    npc = n_pages // PAGES_PER_STEP
    keys = PAGES_PER_STEP * PAGE_SIZE
    ncmp = n_pages * PAGE_SIZE // CMP_BLOCK
    wt = jnp.tile(w_cmp.T, (1, keys // CMP_BLOCK))
    blk = (jnp.arange(npc)[:, None] * keys + jnp.arange(keys)[None, :]) // CMP_BLOCK
    col = (blk % 2) * (ncmp // 2) + blk // 2
    seg = (col[:, :, None] == jnp.arange(ncmp)[None, None, :]).astype(BF16)
    return wt, seg


def _layer_prompt(x, mod, wts, norms, mlstm_norm_w):
    nb, s, d = x.shape
    r = nb * s
    sh1, sc1, ga1, sh2, sc2, ga2 = [m.reshape(nb, 1, d) for m in jnp.split(mod, 6, axis=-1)]
    g_pre_mix, g_post_mix, g_pre_ffn, g_post_ffn = norms
    tm = TM_PROJ
    assert s % tm == 0
    tabs = _rope_tables(jnp.arange(s, dtype=jnp.int32))
    x2 = x.reshape(r, d)
    p = _in_proj(x2, g_pre_mix, sc1, sh1, wts, tabs, tm, s, True)
    ya, c_new, n_new, m_new = _mlstm_prompt(p, mlstm_norm_w, nb, s)
    kc, vct = _prompt_cmp_operands(p["kcvc"], nb, s)
    yb = _nsa_prompt(p, kc, vct, nb, s)
    x1 = _merge(x2, ya, yb, p["gab"], ga1, wts["w_b"], wts, g_post_mix, tm, s)
    y = _ffn(x1, sc2, sh2, ga2, g_pre_ffn, g_post_ffn, wts, tm, s)
    rows = lambda a: jnp.transpose(a.reshape(nb, 2, N_KV, HEAD_DIM, -1), (0, 4, 1, 2, 3))
    win = rows(p["win"][:, :, s - min(WINDOW, s):])
    return y.reshape(nb, s, d), (rows(p["cmp"]), rows(p["slc"]), win, c_new, n_new, m_new)


def _layer_sample(x, mod, wts, norms, mlstm_norm_w, cache_cmp, cache_slc, cache_win, page_table, c0, n0, m0):
    nb, ts, d = x.shape
    assert ts == 1
    sh1, sc1, ga1, sh2, sc2, ga2 = [m.reshape(1, nb, d) for m in jnp.split(mod, 6, axis=-1)]
    g_pre_mix, g_post_mix, g_pre_ffn, g_post_ffn = norms
    n_pages = page_table.shape[1]
    past = n_pages * PAGE_SIZE
    tm = nb
    tabs = _rope_tables(jnp.full((nb,), past, jnp.int32))
    x2 = x.reshape(nb, d)
    p = _in_proj(x2, g_pre_mix, sc1, sh1, wts, tabs, tm, nb, False)
    ya, c_new, n_new, m_new = _mlstm_step(p, p["gt"].T, c0, n0, m0, mlstm_norm_w)
    qs = jnp.transpose(p["nq"], (1, 0, 2)).astype(F32)
    ng = jnp.transpose(p["ngt"][:, :HPG * 3], (2, 0, 1)).reshape(nb, N_HEADS, 3)
    ng = jnp.pad(ng, ((0, 0), (0, 0), (0, LANES - 3)))
    wt, seg = _cmp_step_operands(wts["w_cmp"], n_pages)
    oc, ranked = _nsa_dec_cmp(_rows_on_lanes(cache_cmp), page_table, qs, wt, seg)
    k_past = min(TOP_K, past // SEL_BLOCK + 1) - 1
    blocks = ranked[:, :, :k_past, 0].astype(jnp.int32).reshape(nb, N_KV * k_past)
    o, win_new = _nsa_dec_sel(_rows_on_lanes(cache_slc), page_table, blocks, qs, oc, ng,
                              p["slc"].reshape(nb, 1, ROW_W), p["win"].reshape(nb, 1, ROW_W),
                              _rows_on_lanes(cache_win))
    yb = o.reshape(nb, NQ).astype(BF16)
    x1 = _merge(x2, ya.astype(BF16), yb, p["gab"], ga1, wts["w_b"], wts, g_post_mix, tm, nb)
    y = _ffn(x1, sc2, sh2, ga2, g_pre_ffn, g_post_ffn, wts, tm, nb)
    rows = lambda a: a.reshape(nb, -1, 2, N_KV, HEAD_DIM)
    win_rows = jnp.transpose(win_new.reshape(nb, 2, N_KV, HEAD_DIM, -1), (0, 4, 1, 2, 3))
    return y.reshape(nb, 1, d), (rows(p["cmp"]), rows(p["slc"]), win_rows, c_new, n_new, m_new)


def kernel(x_prompt, x_sample, c_prompt, c_sample, cache_cmp_kv, cache_slc_kv, cache_win_kv, state_mlstm_C, state_mlstm_n, state_mlstm_m, page_table, g_pre_mix, g_post_mix, g_pre_ffn, g_post_ffn, w_ada, b_ada, w_in, b_in, mlstm_norm_w, w_cmp_k, w_cmp_v, w_proj_a, w_proj_b, w_out, w_up, w_down):
    depth = w_in.shape[0]
    nbp, nbs = x_prompt.shape[0], x_sample.shape[0]
    xp, xs = x_prompt, x_sample
    c_all = jnp.concatenate([c_prompt, c_sample], axis=0)
    pad = (-c_all.shape[0]) % 8
    c_all = jnp.pad(c_all, ((0, pad), (0, 0)))
    st_p = [[] for _ in range(6)]
    st_s = [[] for _ in range(6)]
    for l in range(depth):
        wts = _prep_weights(w_in[l], b_in[l], w_cmp_k[l], w_cmp_v[l], w_proj_a[l], w_proj_b[l], w_out[l],
                            w_up[l], w_down[l])
        norms = (g_pre_mix[l], g_post_mix[l], g_pre_ffn[l], g_post_ffn[l])
        mod = _adaln(c_all, w_ada[l], b_ada[l])
        xp, sp = _layer_prompt(xp, mod[:nbp], wts, norms, mlstm_norm_w[l])
        xs, ss = _layer_sample(xs, mod[nbp:nbp + nbs], wts, norms, mlstm_norm_w[l], cache_cmp_kv[l],
                               cache_slc_kv[l], cache_win_kv[l], page_table, state_mlstm_C[l],
                               state_mlstm_n[l], state_mlstm_m[l])
        for j in range(6):
            st_p[j].append(sp[j])
            st_s[j].append(ss[j])
    outs_p = [jnp.stack(a, axis=0) for a in st_p]
    outs_s = [jnp.stack(a, axis=0) for a in st_s]
    return (xp, xs, *outs_p, *outs_s)
```

```python
import functools

import jax
import jax.numpy as jnp
from jax import lax
from jax.experimental import pallas as pl
from jax.experimental.pallas import tpu as pltpu

F32 = jnp.float32
BF16 = jnp.bfloat16

M_HEADS, M_DQK, M_DV = 4, 64, 128
GATE_CAP = 15.0
N_HEADS, N_KV, HEAD_DIM = 8, 2, 64
HPG = N_HEADS // N_KV
CMP_BLOCK, SEL_BLOCK, TOP_K, WINDOW = 32, 64, 16, 512
ROPE_DIM = HEAD_DIM // 4
ROPE_THETA = 500000.0
PAGE_SIZE = 128
EPS = 1e-6
NEG = -1e30
LOG2E = 1.4426950408889634
MQK, MV = M_HEADS * M_DQK, M_HEADS * M_DV
NQ, NKV = N_HEADS * HEAD_DIM, N_KV * HEAD_DIM
ROW_W = 2 * NKV

LANES = 128
VMEM_LIMIT = 48 * 1024 * 1024

TM_PROJ = 512
T_CHUNK = 256
TQ = 128
TKS = 512
TKV = 128
ONES_ROWS = 16
PAGES_PER_STEP = 16
FFN_CK = 256

IMP_FORCED, IMP_FUTURE, IMP_TAKEN = 1e30, -1e30, -2e30


def _dot(a, b):
    return jnp.dot(a, b, preferred_element_type=F32)


def _dot_nt(a, b):
    return lax.dot_general(a, b, (((1,), (1,)), ((), ())), preferred_element_type=F32)


def _rms(x, g):
    return x * lax.rsqrt(jnp.mean(x * x, axis=-1, keepdims=True) + EPS) * g


def _log_sigmoid(x):
    return jnp.minimum(x, 0.0) - jnp.log1p(jnp.exp(-jnp.abs(x)))


def _cparams(*sem):
    return pltpu.CompilerParams(dimension_semantics=sem, vmem_limit_bytes=VMEM_LIMIT)


def _adaln_kernel(c_ref, w_ref, b_ref, o_ref):
    c = c_ref[...]
    a = (c * jax.nn.sigmoid(c)).astype(BF16)
    o_ref[...] = _dot(a, w_ref[...].astype(BF16)) + b_ref[...]


def _adaln(c, w, b):
    r, d = c.shape
    n = w.shape[1]
    tn = 1536
    return pl.pallas_call(
        _adaln_kernel,
        grid=(n // tn,),
        in_specs=[pl.BlockSpec((r, d), lambda j: (0, 0)),
                  pl.BlockSpec((d, tn), lambda j: (0, j)),
                  pl.BlockSpec((1, tn), lambda j: (0, j))],
        out_specs=pl.BlockSpec((r, tn), lambda j: (0, j)),
        out_shape=jax.ShapeDtypeStruct((r, n), F32),
        compiler_params=_cparams("arbitrary"),
        name="adaln",
    )(c, w, b.reshape(1, n))


_C_MQ, _C_MK, _C_MV, _C_MO, _C_NQ, _C_NKV, _C_GAB, _C_END = 0, 256, 512, 1024, 1536, 2048, 2816, 4864


def _proj_kernel(x_ref, g_ref, sc_ref, sh_ref, wm_ref, bm_ref, wtg_ref, btg_ref, wtb_ref, btb_ref,
                 ra_ref, rp_ref, rm_ref, ct_ref, st_ref, wc_ref,
                 mq_ref, mk_ref, mv_ref, mo_ref, nq_ref, cmp_ref, slc_ref, win_ref, ks_ref, kw_ref,
                 gab_ref, kcvc_ref, gt_ref, ngt_ref, mkt_ref, vst_ref, vwt_ref, *, tks, tkw, rows_t):
    tm = x_ref.shape[0]
    x = x_ref[...]
    h = _rms(x, g_ref[...]) * (1.0 + sc_ref[0]) + sh_ref[0]
    hb = h.astype(BF16)

    def mm(lo, hi):
        return _dot(hb, wm_ref[:, lo:hi]) + bm_ref[:, lo:hi]

    ra, rp, rm = ra_ref[...], rp_ref[...], rm_ref[...]

    def rope(xc):
        return xc * ra + pltpu.roll(xc, 8, 1) * rp + pltpu.roll(xc, LANES - 8, 1) * rm

    scale_m = M_DQK ** -0.5
    scale_n = HEAD_DIM ** -0.5 * LOG2E
    mq_ref[...] = (mm(_C_MQ, _C_MK) * scale_m).astype(BF16)
    mk_ref[...] = mm(_C_MK, _C_MV).astype(BF16)
    mv_ref[...] = mm(_C_MV, _C_MO).astype(BF16)
    mo_ref[...] = mm(_C_MO, _C_NQ)

    nq = mm(_C_NQ, _C_NKV)
    for c in range(NQ // LANES):
        r = (rope(nq[:, c * LANES:(c + 1) * LANES]) * scale_n).astype(BF16)
        nq_ref[2 * c] = r[:, :HEAD_DIM]
        nq_ref[2 * c + 1] = r[:, HEAD_DIM:]

    nkv = mm(_C_NKV, _C_GAB)
    kc = rope(nkv[:, 0:128])
    vc = nkv[:, 128:256]
    ksr = rope(nkv[:, 256:384])
    kwr = rope(nkv[:, 512:640])
    if not rows_t:
        cmp_ref[:, 0:NKV] = kc
        cmp_ref[:, NKV:ROW_W] = vc
        slc_ref[:, 0:NKV] = ksr
        slc_ref[:, NKV:ROW_W] = nkv[:, 384:512]
        win_ref[:, 0:NKV] = kwr
        win_ref[:, NKV:ROW_W] = nkv[:, 640:768]
    for g in range(N_KV):
        ks_ref[g] = ksr[:, g * HEAD_DIM:(g + 1) * HEAD_DIM].astype(BF16)
        kw_ref[g] = kwr[:, g * HEAD_DIM:(g + 1) * HEAD_DIM].astype(BF16)

    wc = wc_ref[...]
    nb = tm // CMP_BLOCK
    kcvc_ref[:, 0:NKV] = jnp.sum(kc.reshape(nb, CMP_BLOCK, NKV) * wc[None, :, 0:NKV], axis=1)
    kcvc_ref[:, NKV:ROW_W] = jnp.sum(vc.reshape(nb, CMP_BLOCK, NKV) * wc[None, :, NKV:ROW_W], axis=1)

    gab_ref[...] = mm(_C_GAB, _C_END)

    tg = _dot_nt(wtg_ref[...], hb) + btg_ref[...]
    gt_ref[...] = tg[0:8]
    ngt_ref[0] = tg[16:32]
    ngt_ref[1] = tg[32:48]
    tb = _dot_nt(wtb_ref[...], hb) + btb_ref[...]
    mkt_ref[...] = tb[0:MQK].astype(BF16)
    for j in range(tm // tks):
        vst_ref[j] = tb[MQK:MQK + NKV, j * tks:(j + 1) * tks].astype(BF16)
    for j in range(tm // tkw):
        vwt_ref[j] = tb[MQK + NKV:MQK + 2 * NKV, j * tkw:(j + 1) * tkw].astype(BF16)

    if rows_t:
        ct, st = ct_ref[...], st_ref[...]
        half = ROPE_DIM // 2

        def rope_t(kt):
            parts = []
            for g in range(N_KV):
                blk = kt[g * HEAD_DIM:(g + 1) * HEAD_DIM]
                x1, x2 = blk[0:half], blk[half:ROPE_DIM]
                parts += [x1 * ct - x2 * st, x2 * ct + x1 * st, blk[ROPE_DIM:HEAD_DIM]]
            return jnp.concatenate(parts, axis=0)

        o = MQK + 2 * NKV
        cmp_ref[0:NKV] = rope_t(tb[o:o + NKV])
        cmp_ref[NKV:ROW_W] = tb[o + NKV:o + 2 * NKV]
        slc_ref[0:NKV] = rope_t(tb[o + 2 * NKV:o + 3 * NKV])
        slc_ref[NKV:ROW_W] = tb[MQK:MQK + NKV]
        win_ref[0:NKV] = rope_t(tb[o + 3 * NKV:o + 4 * NKV])
        win_ref[NKV:ROW_W] = tb[MQK + NKV:MQK + 2 * NKV]


def _in_proj(x, g_pre, sc, sh, wts, rope_tabs, tm, rows_per_mod, rows_t):
    r, d = x.shape
    nt = r // tm
    tks = tkw = min(TKV, tm)
    tpm = rows_per_mod // tm
    tpr = rope_tabs[0].shape[0] // tm
    mr = sc.shape[1]
    row = lambda i: (i, 0)
    const2 = lambda i: (0, 0)
    if rows_t:
        rows_shape = (r // rows_per_mod, ROW_W, rows_per_mod)
        rows_spec = pl.BlockSpec((None, ROW_W, tm), lambda i: (i // tpm, 0, i % tpm))
    else:
        rows_shape = (r, ROW_W)
        rows_spec = pl.BlockSpec((tm, ROW_W), row)
    ntb = wts["w_tb"].shape[0]
    in_specs = [
        pl.BlockSpec((tm, d), row),
        pl.BlockSpec((1, d), const2),
        pl.BlockSpec((1, mr, d), lambda i: (i // tpm, 0, 0)),
        pl.BlockSpec((1, mr, d), lambda i: (i // tpm, 0, 0)),
        pl.BlockSpec((d, _C_END), const2),
        pl.BlockSpec((1, _C_END), const2),
        pl.BlockSpec((48, d), const2),
        pl.BlockSpec((48, 1), const2),
        pl.BlockSpec((ntb, d), const2),
        pl.BlockSpec((ntb, 1), const2),
        pl.BlockSpec((tm, LANES), lambda i: (i % tpr, 0)),
        pl.BlockSpec((tm, LANES), lambda i: (i % tpr, 0)),
        pl.BlockSpec((tm, LANES), lambda i: (i % tpr, 0)),
        pl.BlockSpec((ROPE_DIM // 2, tm), lambda i: (0, i % tpr)),
        pl.BlockSpec((ROPE_DIM // 2, tm), lambda i: (0, i % tpr)),
        pl.BlockSpec((CMP_BLOCK, ROW_W), const2),
    ]
    outs = [
        ("mq", (r, MQK), BF16, pl.BlockSpec((tm, MQK), row)),
        ("mk", (r, MQK), BF16, pl.BlockSpec((tm, MQK), row)),
        ("mv", (r, MV), BF16, pl.BlockSpec((tm, MV), row)),
        ("mo", (r, MV), F32, pl.BlockSpec((tm, MV), row)),
        ("nq", (N_HEADS, r, HEAD_DIM), BF16, pl.BlockSpec((N_HEADS, tm, HEAD_DIM), lambda i: (0, i, 0))),
        ("cmp", rows_shape, F32, rows_spec),
        ("slc", rows_shape, F32, rows_spec),
        ("win", rows_shape, F32, rows_spec),
        ("ks", (N_KV, r, HEAD_DIM), BF16, pl.BlockSpec((N_KV, tm, HEAD_DIM), lambda i: (0, i, 0))),
        ("kw", (N_KV, r, HEAD_DIM), BF16, pl.BlockSpec((N_KV, tm, HEAD_DIM), lambda i: (0, i, 0))),
        ("gab", (r, 2 * d), F32, pl.BlockSpec((tm, 2 * d), row)),
        ("kcvc", (r // CMP_BLOCK, ROW_W), F32, pl.BlockSpec((tm // CMP_BLOCK, ROW_W), row)),
        ("gt", (8, r), F32, pl.BlockSpec((8, tm), lambda i: (0, i))),
        ("ngt", (N_KV, 16, r), F32, pl.BlockSpec((N_KV, 16, tm), lambda i: (0, 0, i))),
        ("mkt", (MQK, r), BF16, pl.BlockSpec((MQK, tm), lambda i: (0, i))),
        ("vst", (r // tks, NKV, tks), BF16, pl.BlockSpec((tm // tks, NKV, tks), lambda i: (i, 0, 0))),
        ("vwt", (r // tkw, NKV, tkw), BF16, pl.BlockSpec((tm // tkw, NKV, tkw), lambda i: (i, 0, 0))),
    ]
    res = pl.pallas_call(
        functools.partial(_proj_kernel, tks=tks, tkw=tkw, rows_t=rows_t),
        grid=(nt,),
        in_specs=in_specs,
        out_specs=[o[3] for o in outs],
        out_shape=[jax.ShapeDtypeStruct(o[1], o[2]) for o in outs],
        compiler_params=_cparams("parallel"),
        name="in_proj",
    )(x, g_pre.reshape(1, d), sc, sh, wts["w_main"], wts["b_main"], wts["w_tg"], wts["b_tg"],
      wts["w_tb"], wts["b_tb"], *rope_tabs, wts["w_cmp"])
    return {o[0]: v for o, v in zip(outs, res)}


def _mlstm_chunk_kernel(q_ref, k_ref, kt_ref, v_ref, mo_ref, gt_ref, nw_ref,
                        ya_ref, c_out_ref, m_out_ref, caug, mstate):
    t = q_ref.shape[0]
    c = pl.program_id(1)

    @pl.when(c == 0)
    def _():
        caug[...] = jnp.zeros_like(caug)
        mstate[...] = jnp.zeros_like(mstate)

    gc = GATE_CAP * jnp.tanh(gt_ref[...] / GATE_CAP)
    lane = lax.broadcasted_iota(jnp.int32, (8, t), 1)
    b = _log_sigmoid(gc)
    k = 1
    while k < t:
        b = b + jnp.where(lane >= k, pltpu.roll(b, k, 1), 0.0)
        k *= 2

    row = lax.broadcasted_iota(jnp.int32, (t, t), 0)
    col = lax.broadcasted_iota(jnp.int32, (t, t), 1)
    causal = col <= row
    eye = col == row
    lane_v = lax.broadcasted_iota(jnp.int32, (t, M_DV), 1)

    def to_col(r):
        return jnp.sum(jnp.where(eye, r, 0.0), axis=1, keepdims=True)

    for h in range(M_HEADS):
        qs, vs_ = slice(h * M_DQK, (h + 1) * M_DQK), slice(h * M_DV, (h + 1) * M_DV)
        irow = gc[h:h + 1, :]
        brow = b[M_HEADS + h:M_HEADS + h + 1, :]
        bcol = to_col(brow)
        mprev = mstate[h:h + 1, 0:1]
        d = jnp.where(causal, bcol - brow + irow, NEG)
        minter = bcol + mprev
        mt = jnp.maximum(minter, jnp.max(d, axis=1, keepdims=True))
        qh, kh, vh = q_ref[:, qs], k_ref[:, qs], v_ref[:, vs_]
        w = _dot_nt(qh, kh) * jnp.exp(d - mt)
        ainter = jnp.exp(minter - mt)
        ca = caug[h]
        qc = _dot(qh, ca.astype(BF16))
        num = _dot(w.astype(BF16), vh) + ainter * qc[:, 0:M_DV]
        den = jnp.sum(w, axis=1, keepdims=True) + ainter * qc[:, M_DV:M_DV + 1]
        hh = num / jnp.maximum(jnp.abs(den), jnp.exp(-mt))
        y = _rms(hh, nw_ref[:, vs_])
        ya_ref[:, vs_] = (jax.nn.sigmoid(mo_ref[:, vs_]) * y).astype(BF16)
        b_last = brow[:, t - 1:t]
        grow = b_last - brow + irow
        mnew = jnp.maximum(b_last + mprev, jnp.max(grow, axis=1, keepdims=True))
        ain = to_col(jnp.exp(grow - mnew))
        ast = jnp.exp(b_last + mprev - mnew)
        vsa = jnp.concatenate([ain * vh.astype(F32), jnp.where(lane_v == 0, ain, 0.0)], axis=1)
        caug[h] = ast * ca + _dot(kt_ref[qs, :], vsa.astype(BF16))
        mstate[h:h + 1, :] = jnp.broadcast_to(mnew, (1, LANES))

    @pl.when(c == pl.num_programs(1) - 1)
    def _():
        c_out_ref[0] = caug[...]
        m_out_ref[0] = mstate[...]


def _mlstm_prompt(p, norm_w, nb, s):
    t = T_CHUNK
    assert s % t == 0
    nc = s // t
    r = nb * s
    rowc = lambda b, c: (b * nc + c, 0)
    ya, caug, mst = pl.pallas_call(
        _mlstm_chunk_kernel,
        grid=(nb, nc),
        in_specs=[pl.BlockSpec((t, MQK), rowc),
                  pl.BlockSpec((t, MQK), rowc),
                  pl.BlockSpec((MQK, t), lambda b, c: (0, b * nc + c)),
                  pl.BlockSpec((t, MV), rowc),
                  pl.BlockSpec((t, MV), rowc),
                  pl.BlockSpec((8, t), lambda b, c: (0, b * nc + c)),
                  pl.BlockSpec((1, MV), lambda b, c: (0, 0))],
        out_specs=[pl.BlockSpec((t, MV), rowc),
                   pl.BlockSpec((1, M_HEADS, M_DQK, 2 * M_DV), lambda b, c: (b, 0, 0, 0)),
                   pl.BlockSpec((1, 8, LANES), lambda b, c: (b, 0, 0))],
        out_shape=[jax.ShapeDtypeStruct((r, MV), BF16),
                   jax.ShapeDtypeStruct((nb, M_HEADS, M_DQK, 2 * M_DV), F32),
                   jax.ShapeDtypeStruct((nb, 8, LANES), F32)],
        scratch_shapes=[pltpu.VMEM((M_HEADS, M_DQK, 2 * M_DV), F32), pltpu.VMEM((8, LANES), F32)],
        compiler_params=_cparams("parallel", "arbitrary"),
        name="mlstm_prompt",
    )(p["mq"], p["mk"], p["mkt"], p["mv"], p["mo"], p["gt"], norm_w.reshape(1, MV))
    return ya, caug[..., :M_DV], caug[..., M_DV], mst[:, :M_HEADS, 0]


def _mlstm_step_kernel(q_ref, k_ref, v_ref, mo_ref, g_ref, c_ref, n_ref, m_ref, nw_ref,
                       ya_ref, co_ref, no_ref, mo_out_ref):
    bb = q_ref.shape[0]
    q, k, v = q_ref[...], k_ref[...], v_ref[...]
    gc = GATE_CAP * jnp.tanh(g_ref[...] / GATE_CAP)
    lf = _log_sigmoid(gc)
    m0 = m_ref[...]
    eye = lax.broadcasted_iota(jnp.int32, (M_DQK, M_DQK), 0) == lax.broadcasted_iota(jnp.int32, (M_DQK, M_DQK), 1)

    def to_col(r):
        return jnp.sum(jnp.where(eye, r, 0.0), axis=1, keepdims=True)

    ipre = gc[:, 0:M_HEADS]
    minter = lf[:, M_HEADS:2 * M_HEADS] + m0
    mt = jnp.maximum(minter, ipre)
    ain_all, ast_all, emt_all = jnp.exp(ipre - mt), jnp.exp(minter - mt), jnp.exp(-mt)
    mo_out_ref[...] = mt
    pairs = [(b, h) for b in range(bb) for h in range(M_HEADS)]
    qsl = lambda h: slice(h * M_DQK, (h + 1) * M_DQK)
    vsl = lambda h: slice(h * M_DV, (h + 1) * M_DV)
    qh = [q[b:b + 1, qsl(h)] for b, h in pairs]
    kh = [k[b:b + 1, qsl(h)] for b, h in pairs]
    vh = [v[b:b + 1, vsl(h)] for b, h in pairs]
    n0 = [n_ref[b, h:h + 1, :] for b, h in pairs]
    qcol = [to_col(x) for x in qh]
    kcol = [to_col(x) for x in kh]
    qk = [jnp.sum(a * b_, axis=1, keepdims=True) for a, b_ in zip(qh, kh)]
    qn = [jnp.sum(a * b_, axis=1, keepdims=True) for a, b_ in zip(qh, n0)]
    for i, (b, h) in enumerate(pairs):
        c0 = c_ref[b, h]
        ain, ast = ain_all[b:b + 1, h:h + 1], ast_all[b:b + 1, h:h + 1]
        w = qk[i] * ain
        num = w * vh[i] + ast * jnp.sum(qcol[i] * c0, axis=0, keepdims=True)
        den = w + ast * qn[i]
        hh = num / jnp.maximum(jnp.abs(den), emt_all[b:b + 1, h:h + 1])
        y = _rms(hh, nw_ref[:, vsl(h)])
        ya_ref[b:b + 1, vsl(h)] = (jax.nn.sigmoid(mo_ref[b:b + 1, vsl(h)]) * y).astype(ya_ref.dtype)
        co_ref[b, h] = ast * c0 + kcol[i] * (ain * vh[i])
        no_ref[b, h:h + 1, :] = ast * n0[i] + ain * kh[i]


def _mlstm_step(p, gcol, c0, n0, m0, norm_w):
    nb = c0.shape[0]
    bb = 8
    row = lambda i: (i, 0)
    return pl.pallas_call(
        _mlstm_step_kernel,
        grid=(nb // bb,),
        in_specs=[pl.BlockSpec((bb, MQK), row), pl.BlockSpec((bb, MQK), row), pl.BlockSpec((bb, MV), row),
                  pl.BlockSpec((bb, MV), row), pl.BlockSpec((bb, 8), row),
                  pl.BlockSpec((bb, M_HEADS, M_DQK, M_DV), lambda i: (i, 0, 0, 0)),
                  pl.BlockSpec((bb, M_HEADS, M_DQK), lambda i: (i, 0, 0)),
                  pl.BlockSpec((bb, M_HEADS), row),
                  pl.BlockSpec((1, MV), lambda i: (0, 0))],
        out_specs=[pl.BlockSpec((bb, MV), row),
                   pl.BlockSpec((bb, M_HEADS, M_DQK, M_DV), lambda i: (i, 0, 0, 0)),
                   pl.BlockSpec((bb, M_HEADS, M_DQK), lambda i: (i, 0, 0)),
                   pl.BlockSpec((bb, M_HEADS), row)],
        out_shape=[jax.ShapeDtypeStruct((nb, MV), F32),
                   jax.ShapeDtypeStruct(c0.shape, F32),
                   jax.ShapeDtypeStruct(n0.shape, F32),
                   jax.ShapeDtypeStruct(m0.shape, F32)],
        compiler_params=_cparams("parallel"),
        name="mlstm_step",
    )(p["mq"].astype(F32), p["mk"].astype(F32), p["mv"].astype(F32), p["mo"], gcol, c0, n0, m0,
      norm_w.reshape(1, MV))


def _nsa_prompt_kernel(q_ref, kc_ref, vct_ref, ks_ref, vst_ref, kw_ref, vwt_ref, ng_ref, wb_ref, o_ref,
                       bias_ref, s_ref, *, tks):
    tq = q_ref.shape[1]
    r4 = HPG * tq
    gw = HPG * HEAD_DIM
    tkv = vst_ref.shape[-1]
    ncmp = kc_ref.shape[1]
    nsel = ncmp // 2
    groups = range(N_KV)
    i = pl.program_id(1)
    t0 = i * tq
    qs = [q_ref[g * HPG:(g + 1) * HPG].reshape(r4, HEAD_DIM) for g in groups]
    lane_t = t0 + (lax.broadcasted_iota(jnp.int32, (1, r4), 1) & (tq - 1))

    ones_rows = jnp.where(lax.broadcasted_iota(jnp.int32, (ONES_ROWS, tkv), 0) == 0, 1.0, 0.0).astype(BF16)

    def with_ones(vt):
        return jnp.concatenate([vt, ones_rows], axis=0)

    nwv = (WINDOW + tq) // tkv
    w0 = jnp.maximum(t0 - WINDOW, 0) // tkv
    sws = [_dot_nt(kw_ref[g, pl.ds(pl.multiple_of(w0 * tkv, tkv), nwv * tkv), :], qs[g]) + wb_ref[...]
           for g in groups]
    pws = [jnp.exp2(sw - jnp.max(sw, axis=0, keepdims=True)).astype(BF16) for sw in sws]
    accws = []
    for g in groups:
        accw = _dot(with_ones(vwt_ref[w0, g]), pws[g][0:tkv])
        for u in range(1, nwv):
            accw = accw + _dot(with_ones(vwt_ref[w0 + u, g]), pws[g][u * tkv:(u + 1) * tkv])
        accws.append(accw)

    rr = lax.broadcasted_iota(jnp.int32, (ncmp, 1), 0)
    jc = jnp.where(rr < nsel, 2 * rr, 2 * (rr - nsel) + 1)
    valid = ((jc + 1) * CMP_BLOCK - 1) <= lane_t
    jb = lax.broadcasted_iota(jnp.int32, (nsel, tq), 0)
    tt = t0 + lax.broadcasted_iota(jnp.int32, (nsel, tq), 1)
    forced = (jb == (tt >> 6)) | (jb == 0)
    future = jb * SEL_BLOCK > tt
    jf = jb.astype(F32)
    ocs, keys = [], []
    for g in groups:
        sm = jnp.where(valid, _dot_nt(kc_ref[g], qs[g]), NEG)
        e = jnp.exp2(sm - jnp.max(sm, axis=0, keepdims=True))
        pc = e * jnp.where(lane_t >= CMP_BLOCK - 1, 1.0 / jnp.sum(e, axis=0, keepdims=True), 0.0)
        ocs.append(_dot(vct_ref[g], pc.astype(BF16)))
        pp = pc[0:nsel] + pc[nsel:ncmp]
        imp = pp[:, 0:tq]
        for h in range(1, HPG):
            imp = imp + pp[:, h * tq:(h + 1) * tq]
        keys.append(jnp.where(forced, IMP_FORCED, jnp.where(future, IMP_FUTURE, imp)))

    biases = [jnp.full((nsel, tq), NEG, F32) for _ in groups]
    for _ in range(min(TOP_K, nsel)):
        for g in groups:
            cur = jnp.max(keys[g], axis=0, keepdims=True)
            first = jnp.min(jnp.where(keys[g] == cur, jf, float(nsel)), axis=0, keepdims=True)
            pick = jf == first
            biases[g] = jnp.where(pick, 0.0, biases[g])
            keys[g] = jnp.where(pick, IMP_TAKEN, keys[g])
    bpq = tq // SEL_BLOCK
    own = [(jb == bpq * i + u) for u in range(bpq)]
    own_bias = []
    for g in groups:
        bg = jnp.where(future, NEG, biases[g])
        own_bias.append([jnp.max(jnp.where(o, bg, NEG), axis=0, keepdims=True) for o in own])
        for o in own:
            bg = jnp.where(o, NEG, bg)
        bias4 = jnp.concatenate([bg] * HPG, axis=1)
        for j in range(nsel):
            bias_ref[g, j] = jnp.broadcast_to(bias4[j:j + 1, :], (8, r4))

    bpv = tkv // SEL_BLOCK
    vpt = tks // tkv
    nbt = vpt * bpv
    sub = SEL_BLOCK // 8

    def sel_scores(slot, kt):
        k0 = pl.multiple_of(kt * tks, tks)
        for g in groups:
            s = _dot_nt(ks_ref[g, pl.ds(k0, tks), :], qs[g])
            s_ref[slot, g] = (s.reshape(nbt, sub, 8, r4) + bias_ref[g, pl.ds(kt * nbt, nbt)][:, None]).reshape(tks, r4)

    def sel_update(state, ss, v0):
        mid = []
        for g in groups:
            m, acc = state[g]
            mn = jnp.maximum(m, jnp.max(ss[g], axis=0, keepdims=True))
            alpha = jnp.exp2(m - mn)
            mid.append((mn, alpha * acc, jnp.exp2(ss[g] - mn).astype(BF16)))
        out = []
        for g in groups:
            mn, acc, pb = mid[g]
            nv = ss[g].shape[0] // tkv
            step = 2 if nv % 2 == 0 else 1
            for u in range(0, nv, step):
                vt = jnp.concatenate([with_ones(vst_ref[v0 + u + w, g]) for w in range(step)], axis=1)
                acc = acc + _dot(vt, pb[u * tkv:(u + step) * tkv])
            out.append((mn, acc))
        return tuple(out)

    init = tuple((jnp.full((1, r4), NEG, F32), jnp.zeros((HEAD_DIM + ONES_ROWS, r4), F32)) for _ in groups)
    kd = t0 // tks

    def from_slot(slot):
        return tuple(s_ref[slot, g] for g in groups)

    def sel_pair(j, state):
        sel_scores(1, jnp.minimum(2 * j + 1, kd))
        state = sel_update(state, from_slot(0), 2 * j * vpt)
        sel_scores(0, jnp.minimum(2 * j + 2, kd))
        return sel_update(state, from_slot(1), (2 * j + 1) * vpt)

    sel_scores(0, 0)
    state = lax.fori_loop(0, (kd + 1) // 2, sel_pair, init)
    state = lax.cond(kd % 2 == 0, lambda st: sel_update(st, from_slot(0), kd * vpt), lambda st: st, state)
    kl = lax.broadcasted_iota(jnp.int32, (tq, 1), 0)
    tri = kl <= (lane_t - t0)
    s_own = []
    for g in groups:
        s = _dot_nt(ks_ref[g, pl.ds(pl.multiple_of(t0, tq), tq), :], qs[g])
        ob = jnp.concatenate([jnp.broadcast_to(jnp.concatenate([b] * HPG, axis=1), (SEL_BLOCK, r4))
                              for b in own_bias[g]], axis=0)
        s_own.append(jnp.where(tri, s + ob, NEG))
    sel = sel_update(state, tuple(s_own), i * (tq // tkv))

    ng = jax.nn.sigmoid(ng_ref[...])
    for g in groups:
        accw = accws[g]

        def gate(br):
            return jnp.concatenate([ng[g, h * 3 + br:h * 3 + br + 1, :] for h in range(HPG)], axis=1)

        accs = sel[g][1]
        ls, lw = accs[HEAD_DIM:HEAD_DIM + 1], accw[HEAD_DIM:HEAD_DIM + 1]
        out_t = (gate(0) * ocs[g] + (gate(1) / ls) * accs[0:HEAD_DIM]
                 + (gate(2) / lw) * accw[0:HEAD_DIM])
        stacked = jnp.concatenate([out_t[:, h * tq:(h + 1) * tq] for h in range(HPG)], axis=0)
        o_ref[:, g * gw:(g + 1) * gw] = stacked.T.astype(o_ref.dtype)


def _window_bias(tq):
    nvar = WINDOW // tq + 1
    v = jnp.arange(nvar)[:, None, None]
    u = jnp.arange(WINDOW + tq)[None, :, None]
    tt = jnp.arange(tq)[None, None, :]
    dpos = v * tq + tt - u
    ok = (dpos >= 0) & (dpos < WINDOW)
    return jnp.tile(jnp.where(ok, 0.0, NEG).astype(F32), (1, 1, HPG))


def _nsa_prompt(p, kc, vct, nb, s):
    tq = TQ
    tks = min(TKS, s)
    assert s % tks == 0 and tq == TKV and tq & (tq - 1) == 0 and tks % TKV == 0 and s >= WINDOW + tq
    nqb = s // tq
    r = nb * s
    ncmp = s // CMP_BLOCK
    gw = HPG * HEAD_DIM
    vst = p["vst"].reshape(r // TKV, N_KV, HEAD_DIM, TKV)
    vwt = p["vwt"].reshape(r // TKV, N_KV, HEAD_DIM, TKV)
    nvar = WINDOW // tq
    return pl.pallas_call(
        functools.partial(_nsa_prompt_kernel, tks=tks),
        grid=(nb, nqb),
        in_specs=[pl.BlockSpec((N_HEADS, tq, HEAD_DIM), lambda b, i: (0, b * nqb + i, 0)),
                  pl.BlockSpec((None, N_KV, ncmp, HEAD_DIM), lambda b, i: (b, 0, 0, 0)),
                  pl.BlockSpec((None, N_KV, HEAD_DIM, ncmp), lambda b, i: (b, 0, 0, 0)),
                  pl.BlockSpec((N_KV, s, HEAD_DIM), lambda b, i: (0, b, 0)),
                  pl.BlockSpec((s // TKV, N_KV, HEAD_DIM, TKV), lambda b, i: (b, 0, 0, 0)),
                  pl.BlockSpec((N_KV, s, HEAD_DIM), lambda b, i: (0, b, 0)),
                  pl.BlockSpec((s // TKV, N_KV, HEAD_DIM, TKV), lambda b, i: (b, 0, 0, 0)),
                  pl.BlockSpec((N_KV, 16, tq), lambda b, i: (0, 0, b * nqb + i)),
                  pl.BlockSpec((None, WINDOW + tq, HPG * tq), lambda b, i: (jnp.minimum(i, nvar), 0, 0))],
        out_specs=pl.BlockSpec((tq, NQ), lambda b, i: (b * nqb + i, 0)),
        out_shape=jax.ShapeDtypeStruct((r, NQ), BF16),
        scratch_shapes=[pltpu.VMEM((N_KV, s // SEL_BLOCK, 8, HPG * tq), F32),
                        pltpu.VMEM((2, N_KV, tks, HPG * tq), F32)],
        compiler_params=_cparams("parallel", "arbitrary"),
        name="nsa_prompt",
    )(p["nq"], kc, vct, p["ks"], vst, p["kw"], vwt, p["ngt"], _window_bias(tq))


def _prompt_cmp_operands(kcvc, nb, s):
    ncmp = s // CMP_BLOCK
    a = kcvc.reshape(nb, ncmp // 2, 2, 2, N_KV, HEAD_DIM)
    a = jnp.transpose(a, (0, 3, 4, 2, 1, 5)).reshape(nb, 2, N_KV, ncmp, HEAD_DIM)
    kc = a[:, 0].astype(BF16)
    vct = jnp.swapaxes(a[:, 1], -1, -2).astype(BF16)
    return kc, vct


def _by_group(x0, x1):
    return jnp.where(lax.broadcasted_iota(jnp.int32, x0.shape, 0) < HPG, x0, x1)


def _nsa_dec_cmp_kernel(pt_ref, *refs, pos):
    del pt_ref
    pages = refs[:PAGES_PER_STEP]
    wt_ref, seg_ref, q_ref, oc_ref, sel_ref, kcvc = refs[PAGES_PER_STEP:]
    pc = pl.program_id(1)

    @pl.when(pc == 0)
    def _():
        kcvc[...] = jnp.zeros_like(kcvc)

    x = jnp.concatenate([pg[...] for pg in pages], axis=1)
    kcvc[...] += _dot((x * wt_ref[...]).astype(BF16), seg_ref[...])

    @pl.when(pc == pl.num_programs(1) - 1)
    def _():
        ncmp = kcvc.shape[1]
        half = ncmp // 2
        qb = q_ref[...].astype(BF16)
        kv = kcvc[...].astype(BF16)
        kt = [kv[g * HEAD_DIM:(g + 1) * HEAD_DIM] for g in range(N_KV)]
        vt = [kv[NKV + g * HEAD_DIM:NKV + (g + 1) * HEAD_DIM] for g in range(N_KV)]
        s = _by_group(_dot(qb, kt[0]), _dot(qb, kt[1]))
        cc = lax.broadcasted_iota(jnp.int32, (1, ncmp), 1)
        jc = jnp.where(cc < half, 2 * cc, 2 * (cc - half) + 1)
        valid = ((jc + 1) * CMP_BLOCK - 1) <= pos
        sm = jnp.where(valid, s, NEG)
        e = jnp.exp2(sm - jnp.max(sm, axis=1, keepdims=True))
        p = jnp.where(valid, e / jnp.sum(e, axis=1, keepdims=True), 0.0)
        pb = p.astype(BF16)
        oc_ref[...] = _by_group(_dot_nt(pb, vt[0]), _dot_nt(pb, vt[1]))
        pp = p[:, 0:half] + p[:, half:ncmp]
        jl = lax.broadcasted_iota(jnp.int32, (1, half), 1)
        ii = lax.broadcasted_iota(jnp.int32, (half, half), 0)
        jj = lax.broadcasted_iota(jnp.int32, (half, half), 1)
        for g in range(N_KV):
            imp = jnp.sum(pp[g * HPG:(g + 1) * HPG], axis=0, keepdims=True)
            forced = (jl == pos // SEL_BLOCK) | (jl == 0)
            key = jnp.where(forced, IMP_FORCED, jnp.where(jl * SEL_BLOCK > pos, IMP_FUTURE, imp))
            kcol = jnp.sum(jnp.where(ii == jj, key, 0.0), axis=1, keepdims=True)
            ahead = (kcol > key) | ((kcol == key) & (ii < jj))
            rank = jnp.sum(jnp.where(ahead, 1.0, 0.0), axis=0, keepdims=True)
            rr = lax.broadcasted_iota(jnp.int32, (TOP_K, half), 0).astype(F32)
            blk = jnp.sum(jnp.where(rank == rr, jl.astype(F32), 0.0), axis=1, keepdims=True)
            sel_ref[g] = jnp.broadcast_to(blk, (TOP_K, LANES))


def _page_specs():
    def spec(kk):
        return pl.BlockSpec((None, ROW_W, PAGE_SIZE), lambda b, pc, pt: (pt[b, pc * PAGES_PER_STEP + kk], 0, 0))
    return [spec(kk) for kk in range(PAGES_PER_STEP)]


def _nsa_dec_cmp(cache_t, page_table, q, wt, seg):
    nb, n_pages = page_table.shape
    assert n_pages % PAGES_PER_STEP == 0
    npc = n_pages // PAGES_PER_STEP
    past = n_pages * PAGE_SIZE
    ncmp = past // CMP_BLOCK
    nselp = ncmp // 2
    keys = PAGES_PER_STEP * PAGE_SIZE
    perb = lambda b, pc, pt: (b, 0, 0)
    grid_spec = pltpu.PrefetchScalarGridSpec(
        num_scalar_prefetch=1,
        grid=(nb, npc),
        in_specs=_page_specs() + [
            pl.BlockSpec((ROW_W, keys), lambda b, pc, pt: (0, 0)),
            pl.BlockSpec((None, keys, ncmp), lambda b, pc, pt: (pc, 0, 0)),
            pl.BlockSpec((None, N_HEADS, HEAD_DIM), perb)],
        out_specs=[pl.BlockSpec((None, N_HEADS, HEAD_DIM), perb),
                   pl.BlockSpec((None, N_KV, TOP_K, LANES), lambda b, pc, pt: (b, 0, 0, 0))],
        scratch_shapes=[pltpu.VMEM((ROW_W, ncmp), F32)],
    )
    del nselp
    return pl.pallas_call(
        functools.partial(_nsa_dec_cmp_kernel, pos=past),
        grid_spec=grid_spec,
        out_shape=[jax.ShapeDtypeStruct((nb, N_HEADS, HEAD_DIM), F32),
                   jax.ShapeDtypeStruct((nb, N_KV, TOP_K, LANES), F32)],
        compiler_params=_cparams("parallel", "arbitrary"),
        name="nsa_decode_cmp",
    )(page_table, *([cache_t] * PAGES_PER_STEP), wt, seg, q)


def _nsa_dec_sel_kernel(pt_ref, blk_ref, *refs, pos, wb, k_past):
    del pt_ref
    nsel = N_KV * k_past
    pages = refs[:nsel]
    (q_ref, oc_ref, ng_ref, nslc_ref, nwin_ref, nwcol_ref, win_ref, o_ref, wout_ref) = refs[nsel:]
    b = pl.program_id(0)
    q = q_ref[...]
    qb = q.astype(BF16)
    ksl = [slice(g * HEAD_DIM, (g + 1) * HEAD_DIM) for g in range(N_KV)]
    vsl = [slice(NKV + g * HEAD_DIM, NKV + (g + 1) * HEAD_DIM) for g in range(N_KV)]
    half_of_lane = lax.broadcasted_iota(jnp.int32, (1, PAGE_SIZE), 1) // SEL_BLOCK
    bpp = PAGE_SIZE // SEL_BLOCK
    ss, vt = [], []
    for g in range(N_KV):
        own = range(g * k_past, (g + 1) * k_past)
        kt = jnp.concatenate([pages[r][g] for r in own], axis=1).astype(BF16)
        vt.append(jnp.concatenate([pages[r][N_KV + g] for r in own], axis=1).astype(BF16))
        keep = jnp.concatenate([jnp.where(half_of_lane == blk_ref[b, r] % bpp, 0.0, NEG) for r in own], axis=1)
        ss.append(_dot(qb, kt) + keep)
    s_sel = _by_group(ss[0], ss[1])

    def new_row(row):
        sn = _by_group(jnp.sum(q * row[:, ksl[0]], axis=1, keepdims=True),
                       jnp.sum(q * row[:, ksl[1]], axis=1, keepdims=True))
        return sn, _by_group(jnp.broadcast_to(row[:, vsl[0]], q.shape), jnp.broadcast_to(row[:, vsl[1]], q.shape))

    def attend(s, sn, vn, vts_):
        m = jnp.maximum(jnp.max(s, axis=1, keepdims=True), sn)
        p = jnp.exp2(s - m)
        pn = jnp.exp2(sn - m)
        pb = p.astype(BF16)
        num = _by_group(_dot_nt(pb, vts_[0]), _dot_nt(pb, vts_[1])) + pn * vn
        return num / (jnp.sum(p, axis=1, keepdims=True) + pn)

    o_sel = attend(s_sel, *new_row(nslc_ref[...]), vt)
    win = win_ref[...]
    wbf = win.astype(BF16)
    sw = _by_group(_dot(qb, wbf[ksl[0]]), _dot(qb, wbf[ksl[1]]))
    dpos = pos - (pos - wb + lax.broadcasted_iota(jnp.int32, (1, wb), 1))
    okw = (dpos >= 0) & (dpos < WINDOW) & (pos - dpos >= 0)
    o_win = attend(jnp.where(okw, sw, NEG), *new_row(nwin_ref[...]), [wbf[vsl[0]], wbf[vsl[1]]])
    gates = jax.nn.sigmoid(ng_ref[...])
    o_ref[...] = gates[:, 0:1] * oc_ref[...] + gates[:, 1:2] * o_sel + gates[:, 2:3] * o_win
    lid = lax.broadcasted_iota(jnp.int32, (1, wb), 1)
    wout_ref[...] = jnp.where(lid == wb - 1, nwcol_ref[...], pltpu.roll(win, wb - 1, 1))


def _nsa_dec_sel(cache_t, page_table, blocks, q, oc, ng, nslc, nwin, cache_win_t):
    nb, n_pages = page_table.shape
    past = n_pages * PAGE_SIZE
    wb = cache_win_t.shape[-1]
    assert wb == WINDOW and past >= WINDOW
    k_past = blocks.shape[1] // N_KV
    bpp = PAGE_SIZE // SEL_BLOCK
    cache4 = cache_t.reshape(cache_t.shape[0], 2 * N_KV, HEAD_DIM, PAGE_SIZE)
    perb = lambda b, pt, bl: (b, 0, 0)

    n_pool = cache_t.shape[0]

    def tile_spec(g, r):
        def index(b, pt, bl):
            col = jnp.clip(bl[b, g * k_past + r] // bpp, 0, n_pages - 1)
            return (jnp.clip(pt[b, col], 0, n_pool - 1), 0, 0, 0)
        return pl.BlockSpec((None, 2 * N_KV, HEAD_DIM, PAGE_SIZE), index)

    tiles = [tile_spec(g, r) for g in range(N_KV) for r in range(k_past)]
    grid_spec = pltpu.PrefetchScalarGridSpec(
        num_scalar_prefetch=2,
        grid=(nb,),
        in_specs=tiles + [
            pl.BlockSpec((None, N_HEADS, HEAD_DIM), perb),
            pl.BlockSpec((None, N_HEADS, HEAD_DIM), perb),
            pl.BlockSpec((None, N_HEADS, LANES), perb),
            pl.BlockSpec((None, 1, ROW_W), perb),
            pl.BlockSpec((None, 1, ROW_W), perb),
            pl.BlockSpec((None, ROW_W, 1), perb),
            pl.BlockSpec((None, ROW_W, wb), perb)],
        out_specs=[pl.BlockSpec((None, N_HEADS, HEAD_DIM), perb),
                   pl.BlockSpec((None, ROW_W, wb), perb)],
    )
    return pl.pallas_call(
        functools.partial(_nsa_dec_sel_kernel, pos=past, wb=wb, k_past=k_past),
        grid_spec=grid_spec,
        out_shape=[jax.ShapeDtypeStruct((nb, N_HEADS, HEAD_DIM), F32),
                   jax.ShapeDtypeStruct((nb, ROW_W, wb), F32)],
        compiler_params=_cparams("parallel"),
        name="nsa_decode_sel",
    )(page_table, blocks, *([cache4] * len(tiles)), q, oc, ng, nslc, nwin, nwin.reshape(nb, ROW_W, 1),
      cache_win_t)


def _merge_kernel(x_ref, ya_ref, yb_ref, gab_ref, ga_ref, wa_ref, wb_ref, wo_ref, g_ref, o_ref):
    d = x_ref.shape[1]
    pa = _dot(ya_ref[...], wa_ref[...])
    pb = _dot(yb_ref[...], wb_ref[...])
    merged = jax.nn.sigmoid(gab_ref[:, 0:d]) * pa + jax.nn.sigmoid(gab_ref[:, d:2 * d]) * pb
    z = _dot(merged.astype(BF16), wo_ref[...])
    o_ref[...] = x_ref[...] + ga_ref[0] * _rms(z, g_ref[...])


def _merge(x, ya, yb, gab, ga1, w_b, wts, g_post, tm, rows_per_mod):
    r, d = x.shape
    wyb = yb.shape[1]
    tpm = rows_per_mod // tm
    mr = ga1.shape[1]
    row = lambda i: (i, 0)
    const2 = lambda i: (0, 0)
    return pl.pallas_call(
        _merge_kernel,
        grid=(r // tm,),
        in_specs=[pl.BlockSpec((tm, d), row), pl.BlockSpec((tm, MV), row), pl.BlockSpec((tm, wyb), row),
                  pl.BlockSpec((tm, 2 * d), row),
                  pl.BlockSpec((1, mr, d), lambda i: (i // tpm, 0, 0)),
                  pl.BlockSpec((MV, d), const2), pl.BlockSpec((wyb, d), const2), pl.BlockSpec((d, d), const2),
                  pl.BlockSpec((1, d), const2)],
        out_specs=pl.BlockSpec((tm, d), row),
        out_shape=jax.ShapeDtypeStruct((r, d), F32),
        compiler_params=_cparams("parallel"),
        name="merge",
    )(x, ya, yb, gab, ga1, wts["w_a"], w_b, wts["w_o"], g_post.reshape(1, d))


def _ffn_kernel(x_ref, sc_ref, sh_ref, ga_ref, g1_ref, g2_ref, wu_ref, wd_ref, o_ref):
    dff = wd_ref.shape[0]
    x = x_ref[...]
    hb = (_rms(x, g1_ref[...]) * (1.0 + sc_ref[0]) + sh_ref[0]).astype(BF16)
    acc = jnp.zeros(x.shape, F32)
    for c in range(dff // FFN_CK):
        lo, hi = c * FFN_CK, (c + 1) * FFN_CK
        gate = _dot(hb, wu_ref[:, lo:hi])
        up = _dot(hb, wu_ref[:, dff + lo:dff + hi])
        acc = acc + _dot((gate * jax.nn.sigmoid(gate) * up).astype(BF16), wd_ref[lo:hi, :])
    o_ref[...] = x + ga_ref[0] * _rms(acc, g2_ref[...])


def _ffn(x, sc, sh, ga2, g_pre, g_post, wts, tm, rows_per_mod):
    r, d = x.shape
    dff = wts["w_down"].shape[0]
    assert dff % FFN_CK == 0
    tpm = rows_per_mod // tm
    mr = sc.shape[1]
    row = lambda i: (i, 0)
    const2 = lambda i: (0, 0)
    mod = pl.BlockSpec((1, mr, d), lambda i: (i // tpm, 0, 0))
    return pl.pallas_call(
        _ffn_kernel,
        grid=(r // tm,),
        in_specs=[pl.BlockSpec((tm, d), row), mod, mod, mod,
                  pl.BlockSpec((1, d), const2), pl.BlockSpec((1, d), const2),
                  pl.BlockSpec((d, 2 * dff), const2), pl.BlockSpec((dff, d), const2)],
        out_specs=pl.BlockSpec((tm, d), row),
        out_shape=jax.ShapeDtypeStruct((r, d), F32),
        compiler_params=_cparams("parallel"),
        name="ffn",
    )(x, sc, sh, ga2, g_pre.reshape(1, d), g_post.reshape(1, d), wts["w_up"], wts["w_down"])


def _merge_ffn_kernel(x_ref, ya_ref, yb_ref, gab_ref, ga1_ref, sc_ref, sh_ref, ga2_ref, wa_ref, wb_ref, wo_ref,
                      gm_ref, g1_ref, g2_ref, wu_ref, wd_ref, o_ref):
    d = x_ref.shape[1]
    dff = wd_ref.shape[0]
    pa = _dot(ya_ref[...], wa_ref[...])
    pb = _dot(yb_ref[...], wb_ref[...])
    merged = jax.nn.sigmoid(gab_ref[:, 0:d]) * pa + jax.nn.sigmoid(gab_ref[:, d:2 * d]) * pb
    z = _dot(merged.astype(BF16), wo_ref[...])
    x = x_ref[...] + ga1_ref[0] * _rms(z, gm_ref[...])
    hb = (_rms(x, g1_ref[...]) * (1.0 + sc_ref[0]) + sh_ref[0]).astype(BF16)
    acc = jnp.zeros(x.shape, F32)
    for c in range(dff // FFN_CK):
        lo, hi = c * FFN_CK, (c + 1) * FFN_CK
        gate = _dot(hb, wu_ref[:, lo:hi])
        up = _dot(hb, wu_ref[:, dff + lo:dff + hi])
        acc = acc + _dot((gate * jax.nn.sigmoid(gate) * up).astype(BF16), wd_ref[lo:hi, :])
    o_ref[...] = x + ga2_ref[0] * _rms(acc, g2_ref[...])


def _merge_ffn(x, ya, yb, gab, ga1, sc, sh, ga2, wts, g_post_mix, g_pre_ffn, g_post_ffn, tm, rows_per_mod):
    r, d = x.shape
    dff = wts["w_down"].shape[0]
    assert dff % FFN_CK == 0
    tpm = rows_per_mod // tm
    mr = sc.shape[1]
    row = lambda i: (i, 0)
    mod = pl.BlockSpec((1, mr, d), lambda i: (i // tpm, 0, 0))
    res = lambda shape: pl.BlockSpec(shape, lambda i: (0, 0), pipeline_mode=pl.Buffered(1))
    return pl.pallas_call(
        _merge_ffn_kernel,
        grid=(r // tm,),
        in_specs=[pl.BlockSpec((tm, d), row), pl.BlockSpec((tm, MV), row), pl.BlockSpec((tm, NQ), row),
                  pl.BlockSpec((tm, 2 * d), row), mod, mod, mod, mod,
                  res((MV, d)), res((NQ, d)), res((d, d)), res((1, d)), res((1, d)), res((1, d)),
                  res((d, 2 * dff)), res((dff, d))],
        out_specs=pl.BlockSpec((tm, d), row),
        out_shape=jax.ShapeDtypeStruct((r, d), F32),
        compiler_params=_cparams("parallel"),
        name="merge_ffn",
    )(x, ya, yb, gab, ga1, sc, sh, ga2, wts["w_a"], wts["w_b"], wts["w_o"], g_post_mix.reshape(1, d),
      g_pre_ffn.reshape(1, d), g_post_ffn.reshape(1, d), wts["w_up"], wts["w_down"])


def _prep_weights(w_in, b_in, w_cmp_k, w_cmp_v, w_proj_a, w_proj_b, w_out, w_up, w_down):
    o_mi, o_nq, o_ng, o_ga = 4 * 256 + 2 * 256, 1544, 2824, 2848
    o_nkv = o_nq + NQ
    main_cols = [slice(0, o_mi), slice(o_nq, o_ng), slice(o_ga, w_in.shape[1])]
    w_main = jnp.concatenate([w_in[:, s] for s in main_cols], axis=1).astype(BF16)
    b_main = jnp.concatenate([b_in[s] for s in main_cols]).reshape(1, -1)
    z = lambda n: jnp.zeros((w_in.shape[0], n), w_in.dtype)
    ng0, ng1 = slice(o_ng, o_ng + 12), slice(o_ng + 12, o_ng + 24)
    tg_cols = [w_in[:, o_mi:o_mi + 8], z(8), w_in[:, ng0], z(4), w_in[:, ng1], z(4)]
    tg_b = [b_in[o_mi:o_mi + 8], jnp.zeros(8), b_in[ng0], jnp.zeros(4), b_in[ng1], jnp.zeros(4)]
    vs_, vw_ = slice(o_nkv + 3 * NKV, o_nkv + 4 * NKV), slice(o_nkv + 5 * NKV, o_nkv + 6 * NKV)
    nkv_part = lambda j: slice(o_nkv + j * NKV, o_nkv + (j + 1) * NKV)
    tb_parts = [slice(256, 512), vs_, vw_, nkv_part(0), nkv_part(1), nkv_part(2), nkv_part(4)]
    tb_cols = [w_in[:, s] for s in tb_parts]
    tb_b = [b_in[s] for s in tb_parts]
    w_cmp = jnp.concatenate([w_cmp_k, w_cmp_k, w_cmp_v, w_cmp_v], axis=1)
    return {
        "w_main": w_main, "b_main": b_main,
        "w_tg": jnp.concatenate(tg_cols, axis=1).T.astype(BF16),
        "b_tg": jnp.concatenate(tg_b).reshape(-1, 1).astype(F32),
        "w_tb": jnp.concatenate(tb_cols, axis=1).T.astype(BF16),
        "b_tb": jnp.concatenate(tb_b).reshape(-1, 1),
        "w_cmp": w_cmp,
        "w_a": w_proj_a.astype(BF16), "w_b": w_proj_b.astype(BF16), "w_o": w_out.astype(BF16),
        "w_up": w_up.astype(BF16), "w_down": w_down.astype(BF16),
    }


def _rope_tables(pos):
    half = ROPE_DIM // 2
    n = pos.shape[0]
    inv = ROPE_THETA ** (-jnp.arange(half, dtype=F32) * 2.0 / ROPE_DIM)
    ang = pos.astype(F32)[:, None] * inv[None, :]
    cos, sin = jnp.cos(ang), jnp.sin(ang)
    rest = HEAD_DIM - ROPE_DIM
    zh, zr = jnp.zeros((n, half), F32), jnp.zeros((n, rest), F32)
    a = jnp.concatenate([cos, cos, jnp.ones((n, rest), F32)], axis=1)
    p = jnp.concatenate([zh, sin, zr], axis=1)
    m = jnp.concatenate([-sin, zh, zr], axis=1)
    return tuple(jnp.tile(t, (1, LANES // HEAD_DIM)) for t in (a, p, m)) + (cos.T, sin.T)


def _rows_on_lanes(cache):
    n, rows = cache.shape[0], cache.shape[1]
    return jnp.transpose(cache, (0, 2, 3, 4, 1)).reshape(n, ROW_W, rows)


def _cmp_step_operands(w_cmp, n_pages):
    npc = n_pages // PAGES_PER_STEP
    keys = PAGES_PER_STEP * PAGE_SIZE
    ncmp = n_pages * PAGE_SIZE // CMP_BLOCK
    wt = jnp.tile(w_cmp.T, (1, keys // CMP_BLOCK))
    blk = (jnp.arange(npc)[:, None] * keys + jnp.arange(keys)[None, :]) // CMP_BLOCK
    col = (blk % 2) * (ncmp // 2) + blk // 2
    seg = (col[:, :, None] == jnp.arange(ncmp)[None, None, :]).astype(BF16)
    return wt, seg


def _layer_prompt(x, mod, wts, norms, mlstm_norm_w):
    nb, s, d = x.shape
    r = nb * s
    sh1, sc1, ga1, sh2, sc2, ga2 = [m.reshape(nb, 1, d) for m in jnp.split(mod, 6, axis=-1)]
    g_pre_mix, g_post_mix, g_pre_ffn, g_post_ffn = norms
    tm = TM_PROJ
    assert s % tm == 0
    tabs = _rope_tables(jnp.arange(s, dtype=jnp.int32))
    x2 = x.reshape(r, d)
    p = _in_proj(x2, g_pre_mix, sc1, sh1, wts, tabs, tm, s, True)
    ya, c_new, n_new, m_new = _mlstm_prompt(p, mlstm_norm_w, nb, s)
    kc, vct = _prompt_cmp_operands(p["kcvc"], nb, s)
    yb = _nsa_prompt(p, kc, vct, nb, s)
    y = _merge_ffn(x2, ya, yb, p["gab"], ga1, sc2, sh2, ga2, wts, g_post_mix, g_pre_ffn, g_post_ffn, tm, s)
    rows = lambda a: jnp.transpose(a.reshape(nb, 2, N_KV, HEAD_DIM, -1), (0, 4, 1, 2, 3))
    win = rows(p["win"][:, :, s - min(WINDOW, s):])
    return y.reshape(nb, s, d), (rows(p["cmp"]), rows(p["slc"]), win, c_new, n_new, m_new)


def _layer_sample(x, mod, wts, norms, mlstm_norm_w, cache_cmp, cache_slc, cache_win, page_table, c0, n0, m0):
    nb, ts, d = x.shape
    assert ts == 1
    sh1, sc1, ga1, sh2, sc2, ga2 = [m.reshape(1, nb, d) for m in jnp.split(mod, 6, axis=-1)]
    g_pre_mix, g_post_mix, g_pre_ffn, g_post_ffn = norms
    n_pages = page_table.shape[1]
    past = n_pages * PAGE_SIZE
    tm = nb
    tabs = _rope_tables(jnp.full((nb,), past, jnp.int32))
    x2 = x.reshape(nb, d)
    p = _in_proj(x2, g_pre_mix, sc1, sh1, wts, tabs, tm, nb, False)
    ya, c_new, n_new, m_new = _mlstm_step(p, p["gt"].T, c0, n0, m0, mlstm_norm_w)
    qs = jnp.transpose(p["nq"], (1, 0, 2)).astype(F32)
    ng = jnp.transpose(p["ngt"][:, :HPG * 3], (2, 0, 1)).reshape(nb, N_HEADS, 3)
    ng = jnp.pad(ng, ((0, 0), (0, 0), (0, LANES - 3)))
    wt, seg = _cmp_step_operands(wts["w_cmp"], n_pages)
    oc, ranked = _nsa_dec_cmp(_rows_on_lanes(cache_cmp), page_table, qs, wt, seg)
    k_past = min(TOP_K, past // SEL_BLOCK + 1) - 1
    blocks = ranked[:, :, :k_past, 0].astype(jnp.int32).reshape(nb, N_KV * k_past)
    o, win_new = _nsa_dec_sel(_rows_on_lanes(cache_slc), page_table, blocks, qs, oc, ng,
                              p["slc"].reshape(nb, 1, ROW_W), p["win"].reshape(nb, 1, ROW_W),
                              _rows_on_lanes(cache_win))
    yb = o.reshape(nb, NQ).astype(BF16)
    x1 = _merge(x2, ya.astype(BF16), yb, p["gab"], ga1, wts["w_b"], wts, g_post_mix, tm, nb)
    y = _ffn(x1, sc2, sh2, ga2, g_pre_ffn, g_post_ffn, wts, tm, nb)
    rows = lambda a: a.reshape(nb, -1, 2, N_KV, HEAD_DIM)
    win_rows = jnp.transpose(win_new.reshape(nb, 2, N_KV, HEAD_DIM, -1), (0, 4, 1, 2, 3))
    return y.reshape(nb, 1, d), (rows(p["cmp"]), rows(p["slc"]), win_rows, c_new, n_new, m_new)


def kernel(x_prompt, x_sample, c_prompt, c_sample, cache_cmp_kv, cache_slc_kv, cache_win_kv, state_mlstm_C, state_mlstm_n, state_mlstm_m, page_table, g_pre_mix, g_post_mix, g_pre_ffn, g_post_ffn, w_ada, b_ada, w_in, b_in, mlstm_norm_w, w_cmp_k, w_cmp_v, w_proj_a, w_proj_b, w_out, w_up, w_down):
    depth = w_in.shape[0]
    nbp, nbs = x_prompt.shape[0], x_sample.shape[0]
    xp, xs = x_prompt, x_sample
    c_all = jnp.concatenate([c_prompt, c_sample], axis=0)
    pad = (-c_all.shape[0]) % 8
    c_all = jnp.pad(c_all, ((0, pad), (0, 0)))
    st_p = [[] for _ in range(6)]
    st_s = [[] for _ in range(6)]
    for l in range(depth):
        wts = _prep_weights(w_in[l], b_in[l], w_cmp_k[l], w_cmp_v[l], w_proj_a[l], w_proj_b[l], w_out[l],
                            w_up[l], w_down[l])
        norms = (g_pre_mix[l], g_post_mix[l], g_pre_ffn[l], g_post_ffn[l])
        mod = _adaln(c_all, w_ada[l], b_ada[l])
        xp, sp = _layer_prompt(xp, mod[:nbp], wts, norms, mlstm_norm_w[l])
        xs, ss = _layer_sample(xs, mod[nbp:nbp + nbs], wts, norms, mlstm_norm_w[l], cache_cmp_kv[l],
                               cache_slc_kv[l], cache_win_kv[l], page_table, state_mlstm_C[l],
                               state_mlstm_n[l], state_mlstm_m[l])
        for j in range(6):
            st_p[j].append(sp[j])
            st_s[j].append(ss[j])
    outs_p = [jnp.stack(a, axis=0) for a in st_p]
    outs_s = [jnp.stack(a, axis=0) for a in st_s]
    return (xp, xs, *outs_p, *outs_s)
```
